```python
import jax
import jax.numpy as jnp
from jax import lax
import numpy as np

D_MODEL = 1024
BATCH = 8
SEQ = 4096
DEPTH = 4
DEC_BATCH = 8
DEC_SEQ = 64
PAST_LEN = 2048

CHUNK = 64
ATT_HEADS = 8
KV_HEADS = 2
HEAD_DIM = D_MODEL // 16
ATT_WIDTH = ATT_HEADS * HEAD_DIM
IDX_HEADS = 4
IDX_DIM = 64
TOPK_MAX = 256
Q_BLOCK = 128
SG_LEN = 128
SG_GROUPS = 4
SG_WIDTH = D_MODEL // 2
HG_HEADS = 4
HG_DK = 128
HG_DV = 128
HG_WIDTH = HG_HEADS * HG_DV
N_BRANCH = 3
N_EXPERTS = 16
N_GROUPS = 4
EXP_PER_GROUP = N_EXPERTS // N_GROUPS
TOP_K_EXPERTS = 2
D_FF_EXPERT = D_MODEL // 4

ALPHA = (2 * DEPTH) ** 0.25
BETA = (8 * DEPTH) ** -0.25
LN_EPS = 1e-5

SPLIT_SIZES = (
    ATT_WIDTH,
    KV_HEADS * HEAD_DIM,
    KV_HEADS * HEAD_DIM,
    IDX_HEADS * IDX_DIM,
    IDX_DIM,
    IDX_HEADS,
    SG_WIDTH,
    SG_WIDTH,
    HG_HEADS * HG_DK,
    HG_HEADS * HG_DK,
    HG_WIDTH,
    HG_WIDTH,
    N_BRANCH * D_MODEL,
)
IN_COLS = sum(SPLIT_SIZES)

kernel_name = 'hybrid_dsa_gmlp_hgrn2_moe_stream_step'


def layer_norm(x, g, b):
    xf = x.astype(jnp.float32)
    mu = jnp.mean(xf, axis=-1, keepdims=True)
    var = jnp.mean(jnp.square(xf - mu), axis=-1, keepdims=True)
    return ((xf - mu) * lax.rsqrt(var + LN_EPS) * g + b).astype(x.dtype)


def split_cols(z):
    parts = []
    off = 0
    for n in SPLIT_SIZES:
        parts.append(z[..., off:off + n])
        off += n
    return parts


def project_in(x, w_in_l):
    B, T, _ = x.shape
    (q, k, v, qi, ki, wi, u_raw, v_raw, hq, hf, hi, hg, gates) = split_cols(
        jnp.einsum('btd,dn->btn', x, w_in_l))
    q = q.reshape(B, T, KV_HEADS, ATT_HEADS // KV_HEADS, HEAD_DIM) * HEAD_DIM ** -0.5
    k = k.reshape(B, T, KV_HEADS, HEAD_DIM)
    v = v.reshape(B, T, KV_HEADS, HEAD_DIM)
    qi = qi.reshape(B, T, IDX_HEADS, IDX_DIM) * IDX_DIM ** -0.5
    wi = wi * IDX_HEADS ** -0.5
    return q, k, v, qi, ki, wi, u_raw, v_raw, hq, hf, hi, hg, gates


def dsa_block(q, qi, wi, q_pos, k_all, v_all, ki_all, topk):
    B, T = q.shape[:2]
    L = k_all.shape[1]
    raw = jnp.einsum('bthd,bsd->bhts', qi, ki_all).astype(jnp.float32)
    idx_score = jnp.einsum('bhts,bth->bts', jax.nn.relu(raw), wi.astype(jnp.float32))
    chunk_end = (q_pos // CHUNK + 1) * CHUNK
    visible = jnp.arange(L)[None, :] < chunk_end[:, None]
    idx_score = jnp.where(visible[None], idx_score, -jnp.inf)
    top_val, top_idx = lax.top_k(idx_score, topk)
    valid = jnp.isfinite(top_val)
    gather = jax.vmap(lambda a, i: a[i])
    k_sel = gather(k_all, top_idx)
    v_sel = gather(v_all, top_idx)
    s = jnp.einsum('btgrd,btkgd->btgrk', q, k_sel).astype(jnp.float32)
    s = jnp.where(valid[:, :, None, None, :], s, -jnp.inf)
    p = jax.nn.softmax(s, axis=-1).astype(v_sel.dtype)
    o = jnp.einsum('btgrk,btkgd->btgrd', p, v_sel)
    return o.reshape(B, T, ATT_WIDTH)


def dsa_prompt(q, qi, wi, k, v, ki, topk):
    B, S = q.shape[:2]
    nblk = S // Q_BLOCK

    def to_blocks(a):
        return jnp.moveaxis(a.reshape((B, nblk, Q_BLOCK) + a.shape[2:]), 1, 0)

    def one_block(args):
        qb, qib, wib, start = args
        return dsa_block(qb, qib, wib, start + jnp.arange(Q_BLOCK), k, v, ki, topk)

    starts = jnp.arange(nblk) * Q_BLOCK
    out = lax.map(one_block, (to_blocks(q), to_blocks(qi), to_blocks(wi), starts))
    return jnp.moveaxis(out, 0, 1).reshape(B, S, ATT_WIDTH)


def gmlp_mixer(u_raw, v_raw, ln_g, ln_b, w_s, b_s):
    B, T, _ = u_raw.shape
    u = jax.nn.gelu(u_raw)
    v = layer_norm(jax.nn.gelu(v_raw), ln_g, ln_b)
    Lc = min(T, SG_LEN)
    n = T // Lc
    w = jnp.where(jnp.tril(jnp.ones((Lc, Lc), bool)), w_s[:, :Lc, :Lc], 0)
    vb = v.reshape(B, n, Lc, SG_GROUPS, SG_WIDTH // SG_GROUPS)
    mixed = jnp.einsum('gts,bnsgc->bntgc', w, vb) + jnp.transpose(b_s[:, :Lc])[:, :, None]
    return u * mixed.reshape(B, T, SG_WIDTH), v


def hgrn2_chunk_step(S, inp):
    q, k, v, log_f = inp
    C = q.shape[1]
    b = jnp.cumsum(log_f, axis=1)
    causal = jnp.tril(jnp.ones((C, C), bool))[None, :, :, None, None]
    decay = jnp.exp(jnp.where(causal, b[:, :, None] - b[:, None, :], -jnp.inf))
    attn = jnp.einsum('bthc,btshc,bshc->bhts', q, decay, k)
    o = jnp.einsum('bhts,bshv->bthv', attn, v) + jnp.einsum('bthc,bhcv->bthv', q * jnp.exp(b), S)
    b_last = b[:, -1]
    S_new = jnp.exp(b_last)[..., None] * S + jnp.einsum(
        'bshc,bshv->bhcv', k * jnp.exp(b_last[:, None] - b), v)
    return S_new, o


def hgrn2_mixer(hq, hf, hi, hg, lb, norm_g, S0):
    B, T, _ = hq.shape
    f32 = jnp.float32
    q = jax.nn.silu(hq.astype(f32)).reshape(B, T, HG_HEADS, HG_DK) * HG_DK ** -0.5
    z = hf.astype(f32).reshape(B, T, HG_HEADS, HG_DK)
    lb = lb.reshape(HG_HEADS, HG_DK)
    log_f = jnp.logaddexp(jnp.log(lb), jnp.log1p(-lb) + jax.nn.log_sigmoid(z))
    k = (1.0 - lb) * jax.nn.sigmoid(-z)
    v = hi.astype(f32).reshape(B, T, HG_HEADS, HG_DV)
    Lc = min(T, CHUNK)
    n = T // Lc

    def to_chunks(a):
        return jnp.moveaxis(a.reshape(B, n, Lc, HG_HEADS, a.shape[-1]), 1, 0)

    S_fin, o = lax.scan(hgrn2_chunk_step, S0.astype(f32),
                        (to_chunks(q), to_chunks(k), to_chunks(v), to_chunks(log_f)))
    o = jnp.moveaxis(o, 0, 1).reshape(B, T, HG_HEADS, HG_DV)
    o = o * lax.rsqrt(jnp.mean(o * o, axis=-1, keepdims=True) + LN_EPS)
    o = o * norm_g.astype(f32).reshape(HG_HEADS, HG_DV)
    o = o * jax.nn.silu(hg.astype(f32).reshape(B, T, HG_HEADS, HG_DV))
    return o.reshape(B, T, HG_WIDTH).astype(hq.dtype), S_fin


def merge_branches(a, b, c, gates, wa, wb, wc, w_out):
    B, T, _ = a.shape
    g = jax.nn.sigmoid(gates.astype(jnp.float32)).astype(a.dtype).reshape(B, T, N_BRANCH, D_MODEL)
    m = g[:, :, 0] * (a @ wa) + g[:, :, 1] * (b @ wb) + g[:, :, 2] * (c @ wc)
    return m @ w_out


def moe(x, w_router, b_router, wg, wu, wd):
    f32 = jnp.float32
    logits = jnp.einsum('btd,de->bte', x, w_router).astype(f32) + b_router.astype(f32)
    probs = jax.nn.softmax(logits, axis=-1)
    pg = probs.reshape(x.shape[0], x.shape[1], N_GROUPS, EXP_PER_GROUP)
    group_score = jnp.sum(lax.top_k(pg, TOP_K_EXPERTS)[0], axis=-1)
    gsel = jnp.argmax(group_score, axis=-1)
    in_grp = jnp.einsum('btge,btg->bte', pg, jax.nn.one_hot(gsel, N_GROUPS, dtype=f32))
    tv, ti = lax.top_k(in_grp, TOP_K_EXPERTS)
    tw = tv / jnp.sum(tv, axis=-1, keepdims=True)
    eid = gsel[..., None] * EXP_PER_GROUP + ti
    gate = jnp.einsum('btk,btke->bte', tw, jax.nn.one_hot(eid, N_EXPERTS, dtype=f32)).astype(x.dtype)
    h = jnp.einsum('btd,edf->btef', x, wg)
    up = jnp.einsum('btd,edf->btef', x, wu)
    act = jax.nn.silu(h) * up * gate[..., None]
    return jnp.einsum('btef,efd->btd', act, wd)


def setup_inputs(seed: int = 0) -> dict:
    key = jax.random.key(seed)
    ks = iter(jax.random.split(key, 32))

    def nrm(shape, scale=1.0):
        return jax.random.normal(next(ks), shape, jnp.float32) * scale

    return {
        'x_prompt': nrm((BATCH, SEQ, D_MODEL)),
        'x_sample': nrm((DEC_BATCH, DEC_SEQ, D_MODEL)),
        'cache_k': nrm((DEPTH, DEC_BATCH, PAST_LEN, KV_HEADS, HEAD_DIM)),
        'cache_v': nrm((DEPTH, DEC_BATCH, PAST_LEN, KV_HEADS, HEAD_DIM)),
        'cache_kidx': nrm((DEPTH, DEC_BATCH, PAST_LEN, IDX_DIM)),
        'state_hgrn': nrm((DEPTH, DEC_BATCH, HG_HEADS, HG_DK, HG_DV), 0.5),
        'w_in': nrm((DEPTH, D_MODEL, IN_COLS), D_MODEL ** -0.5),
        'w_sg': nrm((DEPTH, SG_GROUPS, SG_LEN, SG_LEN), SG_LEN ** -0.5),
        'b_sg': 1.0 + nrm((DEPTH, SG_GROUPS, SG_LEN), 0.01),
        'ln_sg_g': 1.0 + nrm((DEPTH, SG_WIDTH), 0.02),
        'ln_sg_b': nrm((DEPTH, SG_WIDTH), 0.02),
        'hg_lb_logits': nrm((DEPTH, HG_HEADS * HG_DK), 0.5),
        'hg_norm_g': 1.0 + nrm((DEPTH, HG_WIDTH), 0.02),
        'w_branch_a': nrm((DEPTH, ATT_WIDTH, D_MODEL), ATT_WIDTH ** -0.5),
        'w_branch_b': nrm((DEPTH, SG_WIDTH, D_MODEL), SG_WIDTH ** -0.5),
        'w_branch_c': nrm((DEPTH, HG_WIDTH, D_MODEL), HG_WIDTH ** -0.5),
        'w_out': nrm((DEPTH, D_MODEL, D_MODEL), BETA * D_MODEL ** -0.5),
        'ln1_g': 1.0 + nrm((DEPTH, D_MODEL), 0.02),
        'ln1_b': nrm((DEPTH, D_MODEL), 0.02),
        'w_router': nrm((D_MODEL, N_EXPERTS), D_MODEL ** -0.5),
        'b_router': nrm((N_EXPERTS,), 0.01),
        'w_exp_gate': nrm((DEPTH, N_EXPERTS, D_MODEL, D_FF_EXPERT), D_MODEL ** -0.5),
        'w_exp_up': nrm((DEPTH, N_EXPERTS, D_MODEL, D_FF_EXPERT), D_MODEL ** -0.5),
        'w_exp_down': nrm((DEPTH, N_EXPERTS, D_FF_EXPERT, D_MODEL), BETA * D_FF_EXPERT ** -0.5),
        'ln2_g': 1.0 + nrm((DEPTH, D_MODEL), 0.02),
        'ln2_b': nrm((DEPTH, D_MODEL), 0.02),
    }


def reference(x_prompt, x_sample, cache_k, cache_v, cache_kidx, state_hgrn, w_in, w_sg, b_sg,
              ln_sg_g, ln_sg_b, hg_lb_logits, hg_norm_g, w_branch_a, w_branch_b, w_branch_c,
              w_out, ln1_g, ln1_b, w_router, b_router, w_exp_gate, w_exp_up, w_exp_down,
              ln2_g, ln2_b):
    f32 = jnp.float32
    lb_all = jnp.cumsum(jax.nn.softmax(hg_lb_logits.astype(f32), axis=0), axis=0)
    lb_all = lb_all - lb_all[0:1]
    seq_p = x_prompt.shape[1]
    seq_s = x_sample.shape[1]
    past = cache_k.shape[2]
    topk_p = min(TOPK_MAX, seq_p // 4)
    topk_s = min(TOPK_MAX, (past + seq_s) // 4)
    pos_s = past + jnp.arange(seq_s)

    xp, xs = x_prompt, x_sample
    kp_rows, vp_rows, kip_rows, sp_rows = [], [], [], []
    ks_rows, vs_rows, kis_rows, ss_rows, gv_rows = [], [], [], [], []
    for l in range(DEPTH):
        (q, k, v, qi, ki, wi, u_raw, v_raw, hq, hf, hi, hg, gates) = project_in(xp, w_in[l])
        a = dsa_prompt(q, qi, wi, k, v, ki, topk_p)
        b, _ = gmlp_mixer(u_raw, v_raw, ln_sg_g[l], ln_sg_b[l], w_sg[l], b_sg[l])
        S0 = jnp.zeros((xp.shape[0], HG_HEADS, HG_DK, HG_DV), f32)
        c, S_p = hgrn2_mixer(hq, hf, hi, hg, lb_all[l], hg_norm_g[l], S0)
        mix = merge_branches(a, b, c, gates, w_branch_a[l], w_branch_b[l], w_branch_c[l], w_out[l])
        xp = layer_norm(ALPHA * xp + mix, ln1_g[l], ln1_b[l])
        ffn = moe(xp, w_router, b_router, w_exp_gate[l], w_exp_up[l], w_exp_down[l])
        xp = layer_norm(ALPHA * xp + ffn, ln2_g[l], ln2_b[l])
        kp_rows.append(k)
        vp_rows.append(v)
        kip_rows.append(ki)
        sp_rows.append(S_p.astype(x_prompt.dtype))

        (q, k, v, qi, ki, wi, u_raw, v_raw, hq, hf, hi, hg, gates) = project_in(xs, w_in[l])
        k_all = jnp.concatenate([cache_k[l].astype(k.dtype), k], axis=1)
        v_all = jnp.concatenate([cache_v[l].astype(v.dtype), v], axis=1)
        ki_all = jnp.concatenate([cache_kidx[l].astype(ki.dtype), ki], axis=1)
        a = dsa_block(q, qi, wi, pos_s, k_all, v_all, ki_all, topk_s)
        b, v_gm = gmlp_mixer(u_raw, v_raw, ln_sg_g[l], ln_sg_b[l], w_sg[l], b_sg[l])
        c, S_s = hgrn2_mixer(hq, hf, hi, hg, lb_all[l], hg_norm_g[l], state_hgrn[l])
        mix = merge_branches(a, b, c, gates, w_branch_a[l], w_branch_b[l], w_branch_c[l], w_out[l])
        xs = layer_norm(ALPHA * xs + mix, ln1_g[l], ln1_b[l])
        ffn = moe(xs, w_router, b_router, w_exp_gate[l], w_exp_up[l], w_exp_down[l])
        xs = layer_norm(ALPHA * xs + ffn, ln2_g[l], ln2_b[l])
        ks_rows.append(k)
        vs_rows.append(v)
        kis_rows.append(ki)
        ss_rows.append(S_s.astype(x_sample.dtype))
        gv_rows.append(v_gm)

    return (xp, xs,
            jnp.stack(kp_rows), jnp.stack(vp_rows), jnp.stack(kip_rows), jnp.stack(sp_rows),
            jnp.stack(ks_rows), jnp.stack(vs_rows), jnp.stack(kis_rows), jnp.stack(ss_rows),
            jnp.stack(gv_rows))
```

```python
import functools

import jax
import jax.numpy as jnp
from jax import lax
from jax.experimental import pallas as pl
from jax.experimental.pallas import tpu as pltpu

F32 = jnp.float32
BF16 = jnp.bfloat16
I32 = jnp.int32

D_MODEL = 1024
DEPTH = 4
CHUNK = 64
ATT_HEADS = 8
KV_HEADS = 2
HEAD_DIM = 64
ATT_WIDTH = ATT_HEADS * HEAD_DIM
KV_WIDTH = KV_HEADS * HEAD_DIM
IDX_HEADS = 4
IDX_DIM = 64
TOPK_MAX = 256
SG_LEN = 128
SG_GROUPS = 4
SG_WIDTH = 512
SG_GW = SG_WIDTH // SG_GROUPS
HG_HEADS = 4
HG_DK = 128
HG_DV = 128
HG_WIDTH = HG_HEADS * HG_DV
HG_BLOCK = 16
N_BRANCH = 3
N_EXPERTS = 16
N_GROUPS = 4
EXP_PER_GROUP = 4
D_FF = 256
ALPHA = (2 * DEPTH) ** 0.25
LN_EPS = 1e-5

SPLIT_SIZES = (ATT_WIDTH, KV_WIDTH, KV_WIDTH, IDX_HEADS * IDX_DIM, IDX_DIM, IDX_HEADS,
               SG_WIDTH, SG_WIDTH, HG_HEADS * HG_DK, HG_HEADS * HG_DK, HG_WIDTH, HG_WIDTH,
               N_BRANCH * D_MODEL)

LANES = 128
INT_MIN = -2 ** 31
NEG_BIG = -1e30
DSA_KEY_BLOCK = 512
VMEM_LIMIT = 56 * 1024 * 1024

_NT = (((1,), (1,)), ((), ()))
_TN = (((0,), (0,)), ((), ()))


def _params(sem):
    return pltpu.CompilerParams(dimension_semantics=sem, vmem_limit_bytes=VMEM_LIMIT)


def _layer_norm(x, g, b):
    mu = jnp.mean(x, axis=-1, keepdims=True)
    d = x - mu
    var = jnp.mean(d * d, axis=-1, keepdims=True)
    return d * lax.rsqrt(var + LN_EPS) * g + b


def _gelu_tanh(x):
    return 0.5 * x * (1.0 + jnp.tanh(0.7978845608028654 * (x + 0.044715 * (x * x * x))))


def _sigmoid(x):
    return 1.0 / (1.0 + jnp.exp(-x))


def _attn_proj_kernel(x_ref, w_ref, q_ref, k_ref, v_ref, qi_ref, ki_ref, kiw_ref,
                      kb_ref, vb_ref, kib_ref):
    xb = x_ref[...].astype(BF16)
    z = jnp.dot(xb, w_ref[...], preferred_element_type=F32)
    o = 0
    q_ref[...] = (z[:, o:o + ATT_WIDTH] * HEAD_DIM ** -0.5).astype(BF16)
    o += ATT_WIDTH
    k = z[:, o:o + KV_WIDTH]
    o += KV_WIDTH
    v = z[:, o:o + KV_WIDTH]
    o += KV_WIDTH
    k_ref[...] = k
    v_ref[...] = v
    kb_ref[...] = k.astype(BF16)
    vb_ref[...] = v.astype(BF16)
    qi_ref[...] = (z[:, o:o + IDX_HEADS * IDX_DIM] * IDX_DIM ** -0.5).astype(BF16)
    o += IDX_HEADS * IDX_DIM
    kiw = z[:, o:o + LANES]
    kiw_ref[...] = kiw
    ki_ref[...] = kiw[:, :IDX_DIM]
    kib_ref[...] = kiw[:, :IDX_DIM].astype(BF16)


def _attn_proj(x, w, tm):
    n = x.shape[0]
    ncol = w.shape[1]
    row = lambda i: (i, 0)
    shapes = (
        (ATT_WIDTH, BF16), (KV_WIDTH, F32), (KV_WIDTH, F32), (IDX_HEADS * IDX_DIM, BF16),
        (IDX_DIM, F32), (LANES, F32), (KV_WIDTH, BF16), (KV_WIDTH, BF16), (IDX_DIM, BF16))
    return pl.pallas_call(
        _attn_proj_kernel,
        grid=(n // tm,),
        in_specs=[pl.BlockSpec((tm, D_MODEL), row), pl.BlockSpec((D_MODEL, ncol), lambda i: (0, 0))],
        out_specs=[pl.BlockSpec((tm, c), row) for c, _ in shapes],
        out_shape=[jax.ShapeDtypeStruct((n, c), d) for c, d in shapes],
        compiler_params=_params(("parallel",)),
        name="attn_proj",
    )(x, w)


def _dsa_kernel(q_ref, qi_ref, kiw_ref, kb_ref, vb_ref, kib_ref, o_ref,
                key_scr, acc_scr, m_scr, l_scr, *, tq, kblk, pos0, topk, idx_bits):
    j = pl.program_id(1)
    qpos0 = pos0 + j * tq
    row = lax.broadcasted_iota(I32, (tq, 1), 0)
    chunk_end = ((qpos0 + row) // CHUNK + 1) * CHUNK
    lvis = ((qpos0 + tq - 1) // CHUNK + 1) * CHUNK
    nblk = (lvis + kblk - 1) // kblk
    lane = lax.broadcasted_iota(I32, (tq, kblk), 1)
    nsub = kblk // LANES

    wi = kiw_ref[0][:, IDX_DIM:IDX_DIM + IDX_HEADS] * IDX_HEADS ** -0.5
    qi = qi_ref[0]

    def score_block(i, carry):
        off = pl.multiple_of(i * kblk, kblk)
        kib = kib_ref[0, pl.ds(off, kblk), :]
        sc = jnp.zeros((tq, kblk), F32)
        for h in range(IDX_HEADS):
            raw = lax.dot_general(qi[:, h * IDX_DIM:(h + 1) * IDX_DIM], kib, _NT,
                                  preferred_element_type=F32)
            sc = sc + jnp.maximum(raw, 0.0) * wi[:, h:h + 1]
        bits = pltpu.bitcast(sc, I32)
        key = bits ^ ((bits >> 31) & 0x7FFFFFFF)
        key_scr[i] = jnp.where(off + lane < chunk_end, key, INT_MIN)
        return carry

    lax.fori_loop(0, nblk, score_block, 0)

    def count(hits):
        def body(i, acc):
            hit = hits(i, key_scr[i])
            for c in range(nsub):
                acc = acc + hit[:, c * LANES:(c + 1) * LANES]
            return acc
        acc = lax.fori_loop(0, nblk, body, jnp.zeros((tq, LANES), F32))
        return jnp.sum(acc, axis=1, keepdims=True)

    def key_bit(i, thr):
        cand = thr + jnp.left_shift(jnp.int32(1), 31 - i)
        cnt = count(lambda _, kb: jnp.where(kb >= cand, 1.0, 0.0))
        return jnp.where(cnt >= topk, cand, thr)

    thr = lax.fori_loop(0, 32, key_bit, jnp.full((tq, 1), INT_MIN, I32))

    need = topk - count(lambda _, kb: jnp.where(kb > thr, 1.0, 0.0))

    def idx_bit(i, jmax):
        cand = jmax + jnp.left_shift(jnp.int32(1), idx_bits - 1 - i)
        cnt = count(lambda b, kb: jnp.where(
            kb == thr, jnp.where(b * kblk + lane < cand, 1.0, 0.0), 0.0))
        return jnp.where(cnt < need, cand, jmax)

    jmax = lax.fori_loop(0, idx_bits, idx_bit, jnp.zeros((tq, 1), I32))

    m_scr[...] = jnp.full(m_scr.shape, NEG_BIG, F32)
    l_scr[...] = jnp.zeros(l_scr.shape, F32)
    acc_scr[...] = jnp.zeros(acc_scr.shape, F32)
    q = q_ref[0]
    rep = ATT_HEADS // KV_HEADS

    def attn_block(i, carry):
        off = pl.multiple_of(i * kblk, kblk)
        kb = key_scr[i]
        col = off + lane
        tie = jnp.where(col <= jmax, 0.0, NEG_BIG)
        bias = jnp.where(kb > thr, 0.0, jnp.where(kb == thr, tie, NEG_BIG))
        bias = jnp.where(col < chunk_end, bias, NEG_BIG)
        kk = kb_ref[0, pl.ds(off, kblk), :]
        vv = vb_ref[0, pl.ds(off, kblk), :]
        for hd in range(ATT_HEADS):
            g = hd // rep
            kg = kk[:, g * HEAD_DIM:(g + 1) * HEAD_DIM]
            vg = vv[:, g * HEAD_DIM:(g + 1) * HEAD_DIM]
            s = lax.dot_general(q[:, hd * HEAD_DIM:(hd + 1) * HEAD_DIM], kg, _NT,
                                preferred_element_type=F32)
            s = s + bias
            m_old = m_scr[hd]
            m_new = jnp.maximum(m_old, jnp.max(s, axis=1, keepdims=True))
            alpha = jnp.exp(m_old - m_new)
            p = jnp.exp(s - m_new)
            l_scr[hd] = alpha * l_scr[hd] + jnp.sum(p, axis=1, keepdims=True)
            acc_scr[hd] = alpha * acc_scr[hd] + jnp.dot(p.astype(BF16), vg,
                                                        preferred_element_type=F32)
            m_scr[hd] = m_new
        return carry

    lax.fori_loop(0, nblk, attn_block, 0)
    for hd in range(ATT_HEADS):
        o_ref[0, :, hd * HEAD_DIM:(hd + 1) * HEAD_DIM] = (acc_scr[hd] / l_scr[hd]).astype(BF16)


def _dsa(q, qi, kiw, kb, vb, kib, *, tq, pos0, topk):
    b, t, _ = q.shape
    lp = kb.shape[1]
    kblk = DSA_KEY_BLOCK
    assert lp % kblk == 0 and t % tq == 0
    kern = functools.partial(_dsa_kernel, tq=tq, kblk=kblk, pos0=pos0, topk=topk,
                             idx_bits=(lp - 1).bit_length())
    qmap = lambda bi, j: (bi, j, 0)
    kmap = lambda bi, j: (bi, 0, 0)
    return pl.pallas_call(
        kern,
        grid=(b, t // tq),
        in_specs=[pl.BlockSpec((1, tq, ATT_WIDTH), qmap),
                  pl.BlockSpec((1, tq, IDX_HEADS * IDX_DIM), qmap),
                  pl.BlockSpec((1, tq, LANES), qmap),
                  pl.BlockSpec((1, lp, KV_WIDTH), kmap),
                  pl.BlockSpec((1, lp, KV_WIDTH), kmap),
                  pl.BlockSpec((1, lp, IDX_DIM), kmap)],
        out_specs=pl.BlockSpec((1, tq, ATT_WIDTH), qmap),
        out_shape=jax.ShapeDtypeStruct((b, t, ATT_WIDTH), BF16),
        scratch_shapes=[pltpu.VMEM((lp // kblk, tq, kblk), I32),
                        pltpu.VMEM((ATT_HEADS, tq, HEAD_DIM), F32),
                        pltpu.VMEM((ATT_HEADS, tq, 1), F32),
                        pltpu.VMEM((ATT_HEADS, tq, 1), F32)],
        compiler_params=_params(("parallel", "arbitrary")),
        name="dsa",
    )(q, qi, kiw, kb, vb, kib)


def _gmlp_kernel(x_ref, wu_ref, wv_ref, g_ref, b_ref, ws_ref, bst_ref, o_ref, v_ref, *, tm, lc):
    xb = x_ref[...].astype(BF16)
    u = _gelu_tanh(jnp.dot(xb, wu_ref[...], preferred_element_type=F32))
    v = _layer_norm(_gelu_tanh(jnp.dot(xb, wv_ref[...], preferred_element_type=F32)),
                    g_ref[...], b_ref[...])
    v_ref[...] = v
    vb = v.astype(BF16)
    r = lax.broadcasted_iota(I32, (lc, lc), 0)
    c = lax.broadcasted_iota(I32, (lc, lc), 1)
    for g in range(SG_GROUPS):
        w = jnp.where(r >= c, ws_ref[g], 0.0).astype(BF16)
        bias = bst_ref[:, g:g + 1]
        cs = slice(g * SG_GW, (g + 1) * SG_GW)
        for n in range(tm // lc):
            rs = slice(n * lc, (n + 1) * lc)
            mixed = jnp.dot(w, vb[rs, cs], preferred_element_type=F32) + bias
            o_ref[rs, cs] = (u[rs, cs] * mixed).astype(BF16)


def _gmlp(x, wu, wv, ln_g, ln_b, ws, bst, *, tm, lc):
    n = x.shape[0]
    row = lambda i: (i, 0)
    full2 = lambda i: (0, 0)
    return pl.pallas_call(
        functools.partial(_gmlp_kernel, tm=tm, lc=lc),
        grid=(n // tm,),
        in_specs=[pl.BlockSpec((tm, D_MODEL), row),
                  pl.BlockSpec((D_MODEL, SG_WIDTH), full2),
                  pl.BlockSpec((D_MODEL, SG_WIDTH), full2),
                  pl.BlockSpec((1, SG_WIDTH), full2),
                  pl.BlockSpec((1, SG_WIDTH), full2),
                  pl.BlockSpec((SG_GROUPS, lc, lc), lambda i: (0, 0, 0)),
                  pl.BlockSpec((lc, SG_GROUPS), full2)],
        out_specs=[pl.BlockSpec((tm, SG_WIDTH), row), pl.BlockSpec((tm, SG_WIDTH), row)],
        out_shape=[jax.ShapeDtypeStruct((n, SG_WIDTH), BF16),
                   jax.ShapeDtypeStruct((n, SG_WIDTH), F32)],
        compiler_params=_params(("parallel",)),
        name="gmlp",
    )(x, wu, wv, ln_g, ln_b, ws, bst)


def _hgrn_kernel(x_ref, w_ref, loglb_ref, log1mlb_ref, omlb_ref, ng_ref, s0_ref, c_ref, sout_ref,
                 q_scr, k_scr, v_scr, lf_scr, o_scr, st_scr, *, tc):
    t = pl.program_id(1)

    @pl.when(t == 0)
    def _():
        for h in range(HG_HEADS):
            st_scr[h] = s0_ref[0, h].T

    xb = x_ref[0].astype(BF16)
    z = jnp.dot(xb, w_ref[...], preferred_element_type=F32)
    hq = z[:, 0:HG_WIDTH]
    hf = z[:, HG_WIDTH:2 * HG_WIDTH]
    hg = z[:, 3 * HG_WIDTH:4 * HG_WIDTH]
    q_scr[...] = hq * _sigmoid(hq) * HG_DK ** -0.5
    log_sig = jnp.minimum(hf, 0.0) - jnp.log1p(jnp.exp(-jnp.abs(hf)))
    y = log1mlb_ref[...] + log_sig
    a = loglb_ref[...]
    lf_scr[...] = jnp.maximum(a, y) + jnp.log1p(jnp.exp(-jnp.abs(a - y)))
    k_scr[...] = omlb_ref[...] * _sigmoid(-hf)
    v_scr[...] = z[:, 2 * HG_WIDTH:3 * HG_WIDTH]

    nb = HG_BLOCK
    tri = jnp.where(lax.broadcasted_iota(I32, (nb, nb), 0) >= lax.broadcasted_iota(I32, (nb, nb), 1),
                    1.0, 0.0).astype(F32)
    trow = lax.broadcasted_iota(I32, (nb, 1), 0)

    def block(i, carry):
        r0 = pl.multiple_of(i * nb, nb)
        rows = pl.ds(r0, nb)
        for h in range(HG_HEADS):
            cs = slice(h * HG_DK, (h + 1) * HG_DK)
            bcum = jnp.dot(tri, lf_scr[rows, cs], preferred_element_type=F32,
                           precision=lax.Precision.HIGHEST)
            qb = q_scr[rows, cs]
            kb = k_scr[rows, cs]
            vb = v_scr[rows, cs]
            o = jnp.zeros((nb, HG_DV), F32)
            for s in range(nb):
                e = jnp.exp(jnp.where(trow >= s, bcum - bcum[s:s + 1, :], -jnp.inf))
                a_ts = jnp.sum(qb * kb[s:s + 1, :] * e, axis=1, keepdims=True)
                o = o + a_ts * vb[s:s + 1, :]
            st = st_scr[h]
            qd = (qb * jnp.exp(bcum)).astype(BF16)
            o = o + lax.dot_general(qd, st.astype(BF16), _NT, preferred_element_type=F32)
            blast = bcum[nb - 1:nb, :]
            kd = (kb * jnp.exp(blast - bcum)).astype(BF16)
            upd = lax.dot_general(vb.astype(BF16), kd, _TN, preferred_element_type=F32)
            st_scr[h] = jnp.exp(blast) * st + upd
            o_scr[rows, cs] = o
        return carry

    lax.fori_loop(0, tc // nb, block, 0)

    for h in range(HG_HEADS):
        cs = slice(h * HG_DV, (h + 1) * HG_DV)
        o = o_scr[:, cs]
        o = o * lax.rsqrt(jnp.mean(o * o, axis=-1, keepdims=True) + LN_EPS)
        g = hg[:, cs]
        c_ref[0, :, cs] = (o * ng_ref[:, cs] * (g * _sigmoid(g))).astype(BF16)

    @pl.when(t == pl.num_programs(1) - 1)
    def _():
        for h in range(HG_HEADS):
            sout_ref[0, h] = st_scr[h].T


def _hgrn(x, w, loglb, log1mlb, omlb, ng, s0, *, tc):
    b, t, _ = x.shape
    xmap = lambda bi, j: (bi, j, 0)
    vec = pl.BlockSpec((1, HG_WIDTH), lambda bi, j: (0, 0))
    smap = lambda bi, j: (bi, 0, 0, 0)
    return pl.pallas_call(
        functools.partial(_hgrn_kernel, tc=tc),
        grid=(b, t // tc),
        in_specs=[pl.BlockSpec((1, tc, D_MODEL), xmap),
                  pl.BlockSpec((D_MODEL, 4 * HG_WIDTH), lambda bi, j: (0, 0)),
                  vec, vec, vec, vec,
                  pl.BlockSpec((1, HG_HEADS, HG_DK, HG_DV), smap)],
        out_specs=[pl.BlockSpec((1, tc, HG_WIDTH), xmap),
                   pl.BlockSpec((1, HG_HEADS, HG_DK, HG_DV), smap)],
        out_shape=[jax.ShapeDtypeStruct((b, t, HG_WIDTH), BF16),
                   jax.ShapeDtypeStruct((b, HG_HEADS, HG_DK, HG_DV), F32)],
        scratch_shapes=[pltpu.VMEM((tc, HG_WIDTH), F32)] * 5
                       + [pltpu.VMEM((HG_HEADS, HG_DV, HG_DK), F32)],
        compiler_params=_params(("parallel", "arbitrary")),
        name="hgrn",
    )(x, w, loglb, log1mlb, omlb, ng, s0)


def _merge_kernel(x_ref, a_ref, b_ref, c_ref, wg_ref, wa_ref, wb_ref, wc_ref, wo_ref,
                  g_ref, bt_ref, y_ref):
    x = x_ref[...]
    xb = x.astype(BF16)
    m = None
    for i, (br, w) in enumerate(((a_ref, wa_ref), (b_ref, wb_ref), (c_ref, wc_ref))):
        gate = _sigmoid(jnp.dot(xb, wg_ref[:, i * D_MODEL:(i + 1) * D_MODEL],
                                preferred_element_type=F32))
        term = gate * jnp.dot(br[...], w[...], preferred_element_type=F32)
        m = term if m is None else m + term
    mo = jnp.dot(m.astype(BF16), wo_ref[...], preferred_element_type=F32)
    y_ref[...] = _layer_norm(ALPHA * x + mo, g_ref[...], bt_ref[...])


def _merge(x, a, b, c, wg, wa, wb, wc, wo, ln_g, ln_b, *, tm):
    n = x.shape[0]
    row = lambda i: (i, 0)
    full = lambda i: (0, 0)
    br = pl.BlockSpec((tm, ATT_WIDTH), row)
    bw = pl.BlockSpec((ATT_WIDTH, D_MODEL), full)
    vec = pl.BlockSpec((1, D_MODEL), full)
    return pl.pallas_call(
        _merge_kernel,
        grid=(n // tm,),
        in_specs=[pl.BlockSpec((tm, D_MODEL), row), br, br, br,
                  pl.BlockSpec((D_MODEL, N_BRANCH * D_MODEL), full), bw, bw, bw,
                  pl.BlockSpec((D_MODEL, D_MODEL), full), vec, vec],
        out_specs=pl.BlockSpec((tm, D_MODEL), row),
        out_shape=jax.ShapeDtypeStruct((n, D_MODEL), F32),
        compiler_params=_params(("parallel",)),
        name="merge",
    )(x, a, b, c, wg, wa, wb, wc, wo, ln_g, ln_b)


def _route(logits):
    ex = jnp.exp(logits - jnp.max(logits, axis=0, keepdims=True))
    probs = ex / jnp.sum(ex, axis=0, keepdims=True)
    p = [[probs[g * EXP_PER_GROUP + k:g * EXP_PER_GROUP + k + 1, :] for k in range(EXP_PER_GROUP)]
         for g in range(N_GROUPS)]
    score = []
    for g in range(N_GROUPS):
        best = None
        for k1 in range(EXP_PER_GROUP):
            for k2 in range(k1 + 1, EXP_PER_GROUP):
                pair = p[g][k1] + p[g][k2]
                best = pair if best is None else jnp.maximum(best, pair)
        score.append(best)
    gsel = jnp.zeros(score[0].shape, I32)
    top = score[0]
    for g in range(1, N_GROUPS):
        better = score[g] > top
        top = jnp.where(better, score[g], top)
        gsel = jnp.where(better, g, gsel)
    val = []
    for k in range(EXP_PER_GROUP):
        v = p[0][k]
        for g in range(1, N_GROUPS):
            v = jnp.where(gsel == g, p[g][k], v)
        val.append(v)
    v1, i1 = val[0], jnp.zeros(gsel.shape, I32)
    for k in range(1, EXP_PER_GROUP):
        better = val[k] > v1
        v1 = jnp.where(better, val[k], v1)
        i1 = jnp.where(better, k, i1)
    v2, i2 = jnp.full(v1.shape, -1.0, F32), jnp.zeros(gsel.shape, I32)
    for k in range(EXP_PER_GROUP):
        better = (i1 != k) & (val[k] > v2)
        v2 = jnp.where(better, val[k], v2)
        i2 = jnp.where(better, k, i2)
    den = v1 + v2
    w1, w2 = v1 / den, v2 / den
    out = []
    for g in range(N_GROUPS):
        rows = [jnp.where(gsel == g, jnp.where(i1 == k, w1, jnp.where(i2 == k, w2, 0.0)), 0.0)
                for k in range(EXP_PER_GROUP)]
        out.append(jnp.concatenate(rows, axis=0))
    return out


def _moe_kernel(x_ref, wr_ref, br_ref, wg_ref, wu_ref, wd_ref, g_ref, bt_ref, y_ref,
                xb_scr, gate_scr, acc_scr, *, tm):
    grp = pl.program_id(1)

    @pl.when(grp == 0)
    def _():
        xb = x_ref[...].astype(BF16)
        xb_scr[...] = xb
        logits = lax.dot_general(wr_ref[...], xb, _NT, preferred_element_type=F32) + br_ref[...]
        gates = _route(logits)
        pad = jnp.zeros((LANES - EXP_PER_GROUP, tm), F32)
        for g in range(N_GROUPS):
            gate_scr[g] = jnp.concatenate([gates[g], pad], axis=0).T
        acc_scr[...] = jnp.zeros(acc_scr.shape, F32)

    xb = xb_scr[...]
    gate = gate_scr[grp]
    acc = acc_scr[...]
    for k in range(EXP_PER_GROUP):
        h = jnp.dot(xb, wg_ref[k], preferred_element_type=F32)
        up = jnp.dot(xb, wu_ref[k], preferred_element_type=F32)
        act = h * _sigmoid(h) * up * gate[:, k:k + 1]
        acc = acc + jnp.dot(act.astype(BF16), wd_ref[k], preferred_element_type=F32)
    acc_scr[...] = acc

    @pl.when(grp == N_GROUPS - 1)
    def _():
        y_ref[...] = _layer_norm(ALPHA * x_ref[...] + acc, g_ref[...], bt_ref[...])


def _moe(x, wr_t, br, wg, wu, wd, ln_g, ln_b, *, tm):
    n = x.shape[0]
    row = lambda i, g: (i, 0)
    full = lambda i, g: (0, 0)
    vec = pl.BlockSpec((1, D_MODEL), full)
    return pl.pallas_call(
        functools.partial(_moe_kernel, tm=tm),
        grid=(n // tm, N_GROUPS),
        in_specs=[pl.BlockSpec((tm, D_MODEL), row),
                  pl.BlockSpec((N_EXPERTS, D_MODEL), full),
                  pl.BlockSpec((N_EXPERTS, 1), full),
                  pl.BlockSpec((EXP_PER_GROUP, D_MODEL, D_FF), lambda i, g: (g, 0, 0)),
                  pl.BlockSpec((EXP_PER_GROUP, D_MODEL, D_FF), lambda i, g: (g, 0, 0)),
                  pl.BlockSpec((EXP_PER_GROUP, D_FF, D_MODEL), lambda i, g: (g, 0, 0)),
                  vec, vec],
        out_specs=pl.BlockSpec((tm, D_MODEL), row),
        out_shape=jax.ShapeDtypeStruct((n, D_MODEL), F32),
        scratch_shapes=[pltpu.VMEM((tm, D_MODEL), BF16),
                        pltpu.VMEM((N_GROUPS, tm, LANES), F32),
                        pltpu.VMEM((tm, D_MODEL), F32)],
        compiler_params=_params(("parallel", "arbitrary")),
        name="moe",
    )(x, wr_t, br, wg, wu, wd, ln_g, ln_b)


def _split_w_in(w_in_l):
    offs = [0]
    for n in SPLIT_SIZES:
        offs.append(offs[-1] + n)
    seg = lambda i, j: w_in_l[:, offs[i]:offs[j]]
    pad = jnp.zeros((D_MODEL, LANES - IDX_DIM - IDX_HEADS), w_in_l.dtype)
    w_att = jnp.concatenate([seg(0, 6), pad], axis=1).astype(BF16)
    return (w_att, seg(6, 7).astype(BF16), seg(7, 8).astype(BF16), seg(8, 12).astype(BF16),
            seg(12, 13).astype(BF16))


def _row_tile(n, want):
    while n % want:
        want //= 2
    return want


def _layer(x, bsz, t, lw, *, pos0, topk, cache, s0):
    n = bsz * t
    lc = min(t, SG_LEN)
    q, k, v, qi, ki, kiw, kb, vb, kib = _attn_proj(x, lw["w_att"], _row_tile(n, 512))
    r3 = lambda a: a.reshape(bsz, t, a.shape[-1])
    kb3, vb3, kib3 = r3(kb), r3(vb), r3(kib)
    if cache is not None:
        ck, cv, cki = cache
        kb3 = jnp.concatenate([ck, kb3], axis=1)
        vb3 = jnp.concatenate([cv, vb3], axis=1)
        kib3 = jnp.concatenate([cki, kib3], axis=1)
    lp = -(-kb3.shape[1] // DSA_KEY_BLOCK) * DSA_KEY_BLOCK
    padk = lambda a: jnp.pad(a, ((0, 0), (0, lp - a.shape[1]), (0, 0)))
    a = _dsa(r3(q), r3(qi), r3(kiw), padk(kb3), padk(vb3), padk(kib3),
             tq=min(t, 128), pos0=pos0, topk=topk)
    b, v_gm = _gmlp(x, lw["w_u"], lw["w_v"], lw["ln_sg_g"], lw["ln_sg_b"],
                    lw["w_sg"][:, :lc, :lc], lw["b_sg"][:, :lc].T, tm=_row_tile(n, 256), lc=lc)
    c, s_new = _hgrn(x.reshape(bsz, t, D_MODEL), lw["w_hg"], lw["log_lb"], lw["log1m_lb"],
                     lw["om_lb"], lw["hg_norm_g"], s0, tc=min(t, 256))
    x1 = _merge(x, a.reshape(n, ATT_WIDTH), b, c.reshape(n, HG_WIDTH), lw["w_gates"],
                lw["w_branch_a"], lw["w_branch_b"], lw["w_branch_c"], lw["w_out"],
                lw["ln1_g"], lw["ln1_b"], tm=_row_tile(n, 256))
    x2 = _moe(x1, lw["w_router_t"], lw["b_router"], lw["w_exp_gate"], lw["w_exp_up"],
              lw["w_exp_down"], lw["ln2_g"], lw["ln2_b"], tm=_row_tile(n, 512))
    return x2, k, v, ki, s_new, v_gm


def kernel(x_prompt, x_sample, cache_k, cache_v, cache_kidx, state_hgrn, w_in, w_sg, b_sg, ln_sg_g, ln_sg_b, hg_lb_logits, hg_norm_g, w_branch_a, w_branch_b, w_branch_c, w_out, ln1_g, ln1_b, w_router, b_router, w_exp_gate, w_exp_up, w_exp_down, ln2_g, ln2_b):
    bp, sp, _ = x_prompt.shape
    bs, ss, _ = x_sample.shape
    past = cache_k.shape[2]
    topk_p = min(TOPK_MAX, sp // 4)
    topk_s = min(TOPK_MAX, (past + ss) // 4)

    lb_all = jnp.cumsum(jax.nn.softmax(hg_lb_logits.astype(F32), axis=0), axis=0)
    lb_all = lb_all - lb_all[0:1]
    vec = lambda a: a.reshape(1, -1)

    xp = x_prompt.reshape(bp * sp, D_MODEL)
    xs = x_sample.reshape(bs * ss, D_MODEL)
    s0_p = jnp.zeros((bp, HG_HEADS, HG_DK, HG_DV), F32)
    outs_p, outs_s = [], []
    for l in range(DEPTH):
        w_att, w_u, w_v, w_hg, w_gates = _split_w_in(w_in[l])
        lw = dict(
            w_att=w_att, w_u=w_u, w_v=w_v, w_hg=w_hg, w_gates=w_gates,
            ln_sg_g=vec(ln_sg_g[l]), ln_sg_b=vec(ln_sg_b[l]), w_sg=w_sg[l], b_sg=b_sg[l],
            log_lb=vec(jnp.log(lb_all[l])), log1m_lb=vec(jnp.log1p(-lb_all[l])),
            om_lb=vec(1.0 - lb_all[l]), hg_norm_g=vec(hg_norm_g[l].astype(F32)),
            w_branch_a=w_branch_a[l].astype(BF16), w_branch_b=w_branch_b[l].astype(BF16),
            w_branch_c=w_branch_c[l].astype(BF16), w_out=w_out[l].astype(BF16),
            ln1_g=vec(ln1_g[l]), ln1_b=vec(ln1_b[l]),
            w_router_t=w_router.T.astype(BF16), b_router=b_router.astype(F32).reshape(-1, 1),
            w_exp_gate=w_exp_gate[l].astype(BF16), w_exp_up=w_exp_up[l].astype(BF16),
            w_exp_down=w_exp_down[l].astype(BF16),
            ln2_g=vec(ln2_g[l]), ln2_b=vec(ln2_b[l]))
        xp, k, v, ki, s_new, _ = _layer(xp, bp, sp, lw, pos0=0, topk=topk_p, cache=None, s0=s0_p)
        outs_p.append((k.reshape(bp, sp, KV_HEADS, HEAD_DIM), v.reshape(bp, sp, KV_HEADS, HEAD_DIM),
                       ki.reshape(bp, sp, IDX_DIM), s_new))
        cache = (cache_k[l].reshape(bs, past, KV_WIDTH).astype(BF16),
                 cache_v[l].reshape(bs, past, KV_WIDTH).astype(BF16),
                 cache_kidx[l].astype(BF16))
        xs, k, v, ki, s_new, v_gm = _layer(xs, bs, ss, lw, pos0=past, topk=topk_s, cache=cache,
                                           s0=state_hgrn[l].astype(F32))
        outs_s.append((k.reshape(bs, ss, KV_HEADS, HEAD_DIM), v.reshape(bs, ss, KV_HEADS, HEAD_DIM),
                       ki.reshape(bs, ss, IDX_DIM), s_new, v_gm.reshape(bs, ss, SG_WIDTH)))

    stack = lambda rows, i: jnp.stack([r[i] for r in rows])
    return (xp.reshape(bp, sp, D_MODEL), xs.reshape(bs, ss, D_MODEL),
            stack(outs_p, 0), stack(outs_p, 1), stack(outs_p, 2), stack(outs_p, 3),
            stack(outs_s, 0), stack(outs_s, 1), stack(outs_s, 2), stack(outs_s, 3),
            stack(outs_s, 4))
```

```python
import functools

import jax
import jax.numpy as jnp
from jax import lax
from jax.experimental import pallas as pl
from jax.experimental.pallas import tpu as pltpu

F32 = jnp.float32
BF16 = jnp.bfloat16
I32 = jnp.int32

D_MODEL = 1024
DEPTH = 4
CHUNK = 64
ATT_HEADS = 8
KV_HEADS = 2
HEAD_DIM = 64
ATT_WIDTH = ATT_HEADS * HEAD_DIM
KV_WIDTH = KV_HEADS * HEAD_DIM
IDX_HEADS = 4
IDX_DIM = 64
TOPK_MAX = 256
SG_LEN = 128
SG_GROUPS = 4
SG_WIDTH = 512
SG_GW = SG_WIDTH // SG_GROUPS
HG_HEADS = 4
HG_DK = 128
HG_DV = 128
HG_WIDTH = HG_HEADS * HG_DV
HG_BLOCK = 16
HG_FACTOR_RANGE = 80.0
N_BRANCH = 3
N_EXPERTS = 16
N_GROUPS = 4
EXP_PER_GROUP = 4
D_FF = 256
ALPHA = (2 * DEPTH) ** 0.25
LN_EPS = 1e-5

SPLIT_SIZES = (ATT_WIDTH, KV_WIDTH, KV_WIDTH, IDX_HEADS * IDX_DIM, IDX_DIM, IDX_HEADS,
               SG_WIDTH, SG_WIDTH, HG_HEADS * HG_DK, HG_HEADS * HG_DK, HG_WIDTH, HG_WIDTH,
               N_BRANCH * D_MODEL)

LANES = 128
INT_MIN = -2 ** 31
NEG_BIG = -1e30
DSA_KEY_BLOCK = 512
VMEM_LIMIT = 56 * 1024 * 1024

_NT = (((1,), (1,)), ((), ()))
_TN = (((0,), (0,)), ((), ()))


def _params(sem):
    return pltpu.CompilerParams(dimension_semantics=sem, vmem_limit_bytes=VMEM_LIMIT)


def _layer_norm(x, g, b):
    mu = jnp.mean(x, axis=-1, keepdims=True)
    d = x - mu
    var = jnp.mean(d * d, axis=-1, keepdims=True)
    return d * lax.rsqrt(var + LN_EPS) * g + b


def _gelu_tanh(x):
    return 0.5 * x * (1.0 + jnp.tanh(0.7978845608028654 * (x + 0.044715 * (x * x * x))))


def _sigmoid(x):
    return 1.0 / (1.0 + jnp.exp(-x))


def _fold_rows(x, op):
    parts = [x[i:i + 8] for i in range(0, x.shape[0], 8)]
    while len(parts) > 1:
        parts = [op(parts[i], parts[i + 1]) for i in range(0, len(parts), 2)]
    return parts[0]


ATT_T_ROWS = ATT_WIDTH + IDX_HEADS * IDX_DIM + KV_WIDTH + 8
ATT_N_COLS = 3 * LANES


def _attn_proj_kernel(x_ref, wn_ref, wt_ref, k_ref, v_ref, ki_ref, kb_ref, kib_ref,
                      qt_ref, qit_ref, vt_ref, wit_ref):
    xb = x_ref[...].astype(BF16)
    z = jnp.dot(xb, wn_ref[...], preferred_element_type=F32)
    k = z[:, 0:KV_WIDTH]
    ki = z[:, 2 * KV_WIDTH:2 * KV_WIDTH + IDX_DIM]
    k_ref[...] = k
    v_ref[...] = z[:, KV_WIDTH:2 * KV_WIDTH]
    ki_ref[...] = ki
    kb_ref[...] = k.astype(BF16)
    kib_ref[...] = ki.astype(BF16)
    zt = lax.dot_general(wt_ref[...], xb, _NT, preferred_element_type=F32)
    o = 0
    qt_ref[0] = (zt[o:o + ATT_WIDTH] * HEAD_DIM ** -0.5).astype(BF16)
    o += ATT_WIDTH
    qit_ref[0] = (zt[o:o + IDX_HEADS * IDX_DIM] * IDX_DIM ** -0.5).astype(BF16)
    o += IDX_HEADS * IDX_DIM
    vt_ref[0] = zt[o:o + KV_WIDTH].astype(BF16)
    o += KV_WIDTH
    wit_ref[0] = zt[o:o + 8] * IDX_HEADS ** -0.5


def _attn_proj(x, wn, wt, bsz, t, tm):
    n = bsz * t
    per = t // tm
    row = lambda i: (i, 0)
    tmap = lambda i: (i // per, 0, i % per)
    nat = ((KV_WIDTH, F32), (KV_WIDTH, F32), (IDX_DIM, F32), (KV_WIDTH, BF16), (IDX_DIM, BF16))
    tr = ((ATT_WIDTH, BF16), (IDX_HEADS * IDX_DIM, BF16), (KV_WIDTH, BF16), (8, F32))
    return pl.pallas_call(
        _attn_proj_kernel,
        grid=(n // tm,),
        in_specs=[pl.BlockSpec((tm, D_MODEL), row),
                  pl.BlockSpec((D_MODEL, ATT_N_COLS), lambda i: (0, 0)),
                  pl.BlockSpec((ATT_T_ROWS, D_MODEL), lambda i: (0, 0))],
        out_specs=[pl.BlockSpec((tm, c), row) for c, _ in nat]
                  + [pl.BlockSpec((1, c, tm), tmap) for c, _ in tr],
        out_shape=[jax.ShapeDtypeStruct((n, c), d) for c, d in nat]
                  + [jax.ShapeDtypeStruct((bsz, c, t), d) for c, d in tr],
        compiler_params=_params(("parallel",)),
        name="attn_proj",
    )(x, wn, wt)


def _dsa_kernel(qt_ref, qit_ref, wit_ref, kb_ref, vt_ref, kib_ref, o_ref,
                key_scr, s_scr, acc_scr, m_scr, l_scr, a_scr, j_scr, *, tq, kblk, pos0, topk, idx_bits):
    j = pl.program_id(1)
    qpos0 = pos0 + j * tq
    qcol = lax.broadcasted_iota(I32, (1, tq), 1)
    chunk_end = ((qpos0 + qcol) // CHUNK + 1) * CHUNK
    lvis = ((qpos0 + tq - 1) // CHUNK + 1) * CHUNK
    nblk = (lvis + kblk - 1) // kblk
    krow = lax.broadcasted_iota(I32, (kblk, tq), 0)

    wi = wit_ref[0]
    qit = qit_ref[0]

    def score_block(i, carry):
        off = pl.multiple_of(i * kblk, kblk)
        kib = kib_ref[0, pl.ds(off, kblk), :]
        sc = jnp.zeros((kblk, tq), F32)
        for h in range(IDX_HEADS):
            raw = jnp.dot(kib, qit[h * IDX_DIM:(h + 1) * IDX_DIM, :], preferred_element_type=F32)
            sc = sc + jnp.maximum(raw, 0.0) * wi[h:h + 1, :]
        bits = pltpu.bitcast(sc, I32)
        key = bits ^ ((bits >> 31) & 0x7FFFFFFF)
        key_scr[i] = jnp.where(off + krow < chunk_end, key, INT_MIN)
        return carry

    lax.fori_loop(0, nblk, score_block, 0)

    def count(hits):
        def body(i, acc):
            return acc + _fold_rows(hits(i, key_scr[i]), jnp.add)
        acc = lax.fori_loop(0, nblk, body, jnp.zeros((8, tq), F32))
        return jnp.sum(acc, axis=0, keepdims=True)

    def key_bit(i, thr):
        cand = thr + jnp.left_shift(jnp.int32(1), 31 - i)
        cnt = count(lambda _, kb: jnp.where(kb >= cand, 1.0, 0.0))
        return jnp.where(cnt >= topk, cand, thr)

    thr = lax.fori_loop(0, 32, key_bit, jnp.full((1, tq), INT_MIN, I32))

    n_ge = count(lambda _, kb: jnp.where(kb >= thr, 1.0, 0.0))
    j_scr[...] = jnp.full((1, tq), 2 ** 30, I32)

    @pl.when(jnp.max(n_ge) > topk)
    def _():
        need = topk - count(lambda _, kb: jnp.where(kb > thr, 1.0, 0.0))

        def idx_bit(i, jmax):
            cand = jmax + jnp.left_shift(jnp.int32(1), idx_bits - 1 - i)
            cnt = count(lambda b, kb: jnp.where(
                kb == thr, jnp.where(b * kblk + krow < cand, 1.0, 0.0), 0.0))
            return jnp.where(cnt < need, cand, jmax)

        j_scr[...] = lax.fori_loop(0, idx_bits, idx_bit, jnp.zeros((1, tq), I32))

    jmax = j_scr[...]

    m_scr[...] = jnp.full(m_scr.shape, NEG_BIG, F32)
    l_scr[...] = jnp.zeros(l_scr.shape, F32)
    acc_scr[...] = jnp.zeros(acc_scr.shape, F32)
    qt = qt_ref[0]
    rep = ATT_HEADS // KV_HEADS
    ones = jnp.ones((16, kblk), BF16)

    def attn_block(i, carry):
        off = pl.multiple_of(i * kblk, kblk)
        kb = key_scr[i]
        kpos = off + krow
        tie = jnp.where(kpos <= jmax, 0.0, NEG_BIG)
        bias = jnp.where(kb > thr, 0.0, jnp.where(kb == thr, tie, NEG_BIG))
        bias = jnp.where(kpos < chunk_end, bias, NEG_BIG)
        kk = kb_ref[0, pl.ds(off, kblk), :]
        vv = vt_ref[0, :, pl.ds(off, kblk)]
        for hd in range(ATT_HEADS):
            g = hd // rep
            s = jnp.dot(kk[:, g * HEAD_DIM:(g + 1) * HEAD_DIM],
                        qt[hd * HEAD_DIM:(hd + 1) * HEAD_DIM, :], preferred_element_type=F32) + bias
            s_scr[hd] = s
            m_blk = jnp.max(_fold_rows(s, jnp.maximum), axis=0, keepdims=True)
            m_old = m_scr[hd]
            m_new = jnp.maximum(m_old, m_blk)
            a_scr[hd] = jnp.exp(m_old - m_new)
            m_scr[hd] = m_new
        for hd in range(ATT_HEADS):
            g = hd // rep
            p = jnp.exp(s_scr[hd] - m_scr[hd]).astype(BF16)
            vg = jnp.concatenate([vv[g * HEAD_DIM:(g + 1) * HEAD_DIM, :], ones], axis=0)
            pv = jnp.dot(vg, p, preferred_element_type=F32)
            alpha = a_scr[hd]
            l_scr[hd] = alpha * l_scr[hd] + pv[HEAD_DIM:HEAD_DIM + 1, :]
            acc_scr[hd] = alpha * acc_scr[hd] + pv[:HEAD_DIM, :]
        return carry

    lax.fori_loop(0, nblk, attn_block, 0)
    ot = jnp.concatenate([acc_scr[hd] / l_scr[hd] for hd in range(ATT_HEADS)], axis=0)
    o_ref[0] = ot.T.astype(BF16)


def _dsa(qt, qit, wit, kb, vt, kib, *, tq, pos0, topk):
    b, _, t = qt.shape
    lp = kb.shape[1]
    kblk = DSA_KEY_BLOCK
    assert lp % kblk == 0 and t % tq == 0 and tq % LANES == 0
    kern = functools.partial(_dsa_kernel, tq=tq, kblk=kblk, pos0=pos0, topk=topk,
                             idx_bits=(lp - 1).bit_length())
    qmap = lambda bi, j: (bi, 0, j)
    kmap = lambda bi, j: (bi, 0, 0)
    return pl.pallas_call(
        kern,
        grid=(b, t // tq),
        in_specs=[pl.BlockSpec((1, ATT_WIDTH, tq), qmap),
                  pl.BlockSpec((1, IDX_HEADS * IDX_DIM, tq), qmap),
                  pl.BlockSpec((1, 8, tq), qmap),
                  pl.BlockSpec((1, lp, KV_WIDTH), kmap),
                  pl.BlockSpec((1, KV_WIDTH, lp), kmap),
                  pl.BlockSpec((1, lp, IDX_DIM), kmap)],
        out_specs=pl.BlockSpec((1, tq, ATT_WIDTH), lambda bi, j: (bi, j, 0)),
        out_shape=jax.ShapeDtypeStruct((b, t, ATT_WIDTH), BF16),
        scratch_shapes=[pltpu.VMEM((lp // kblk, kblk, tq), I32),
                        pltpu.VMEM((ATT_HEADS, kblk, tq), F32),
                        pltpu.VMEM((ATT_HEADS, HEAD_DIM, tq), F32),
                        pltpu.VMEM((ATT_HEADS, 1, tq), F32),
                        pltpu.VMEM((ATT_HEADS, 1, tq), F32),
                        pltpu.VMEM((ATT_HEADS, 1, tq), F32),
                        pltpu.VMEM((1, tq), I32)],
        compiler_params=_params(("parallel", "arbitrary")),
        name="dsa",
    )(qt, qit, wit, kb, vt, kib)


def _gmlp_kernel(x_ref, wu_ref, wv_ref, g_ref, b_ref, ws_ref, bst_ref, o_ref, v_ref, *, tm, lc):
    xb = x_ref[...].astype(BF16)
    u = _gelu_tanh(jnp.dot(xb, wu_ref[...], preferred_element_type=F32))
    v = _layer_norm(_gelu_tanh(jnp.dot(xb, wv_ref[...], preferred_element_type=F32)),
                    g_ref[...], b_ref[...])
    v_ref[...] = v
    vb = v.astype(BF16)
    r = lax.broadcasted_iota(I32, (lc, lc), 0)
    c = lax.broadcasted_iota(I32, (lc, lc), 1)
    for g in range(SG_GROUPS):
        w = jnp.where(r >= c, ws_ref[g], 0.0).astype(BF16)
        bias = bst_ref[:, g:g + 1]
        cs = slice(g * SG_GW, (g + 1) * SG_GW)
        for n in range(tm // lc):
            rs = slice(n * lc, (n + 1) * lc)
            mixed = jnp.dot(w, vb[rs, cs], preferred_element_type=F32) + bias
            o_ref[rs, cs] = (u[rs, cs] * mixed).astype(BF16)


def _gmlp(x, wu, wv, ln_g, ln_b, ws, bst, *, tm, lc):
    n = x.shape[0]
    row = lambda i: (i, 0)
    full2 = lambda i: (0, 0)
    return pl.pallas_call(
        functools.partial(_gmlp_kernel, tm=tm, lc=lc),
        grid=(n // tm,),
        in_specs=[pl.BlockSpec((tm, D_MODEL), row),
                  pl.BlockSpec((D_MODEL, SG_WIDTH), full2),
                  pl.BlockSpec((D_MODEL, SG_WIDTH), full2),
                  pl.BlockSpec((1, SG_WIDTH), full2),
                  pl.BlockSpec((1, SG_WIDTH), full2),
                  pl.BlockSpec((SG_GROUPS, lc, lc), lambda i: (0, 0, 0)),
                  pl.BlockSpec((lc, SG_GROUPS), full2)],
        out_specs=[pl.BlockSpec((tm, SG_WIDTH), row), pl.BlockSpec((tm, SG_WIDTH), row)],
        out_shape=[jax.ShapeDtypeStruct((n, SG_WIDTH), BF16),
                   jax.ShapeDtypeStruct((n, SG_WIDTH), F32)],
        compiler_params=_params(("parallel",)),
        name="gmlp",
    )(x, wu, wv, ln_g, ln_b, ws, bst)


def _hgrn_kernel(x_ref, w_ref, loglb_ref, log1mlb_ref, omlb_ref, ng_ref, s0_ref, c_ref, sout_ref,
                 q_scr, k_scr, v_scr, lf_scr, o_scr, st_scr, *, tc):
    t = pl.program_id(1)

    @pl.when(t == 0)
    def _():
        for h in range(HG_HEADS):
            st_scr[h] = s0_ref[0, h].T

    xb = x_ref[0].astype(BF16)
    z = jnp.dot(xb, w_ref[...], preferred_element_type=F32)
    hq = z[:, 0:HG_WIDTH]
    hf = z[:, HG_WIDTH:2 * HG_WIDTH]
    hg = z[:, 3 * HG_WIDTH:4 * HG_WIDTH]
    q_scr[...] = hq * _sigmoid(hq) * HG_DK ** -0.5
    log_sig = jnp.minimum(hf, 0.0) - jnp.log1p(jnp.exp(-jnp.abs(hf)))
    y = log1mlb_ref[...] + log_sig
    a = loglb_ref[...]
    lf_scr[...] = jnp.maximum(a, y) + jnp.log1p(jnp.exp(-jnp.abs(a - y)))
    k_scr[...] = omlb_ref[...] * _sigmoid(-hf)
    v_scr[...] = z[:, 2 * HG_WIDTH:3 * HG_WIDTH]

    ch = min(tc, CHUNK)
    ri = lax.broadcasted_iota(I32, (tc, tc), 0)
    ci = lax.broadcasted_iota(I32, (tc, tc), 1)
    same = (ri // ch) == (ci // ch)
    causal = same & (ri >= ci)
    lf = lf_scr[...]
    bcum_all = jnp.dot(jnp.where(causal, 1.0, 0.0), lf, preferred_element_type=F32,
                       precision=lax.Precision.HIGHEST)
    factored = jnp.min(bcum_all) > -HG_FACTOR_RANGE

    @pl.when(factored)
    def _():
        btot = jnp.dot(jnp.where(same, 1.0, 0.0), lf, preferred_element_type=F32,
                       precision=lax.Precision.HIGHEST)
        qa = q_scr[...]
        ka = k_scr[...]
        qd = (qa * jnp.exp(bcum_all)).astype(BF16)
        ki = (ka * jnp.exp(-bcum_all)).astype(BF16)
        kd = (ka * jnp.exp(btot - bcum_all)).astype(BF16)
        va = v_scr[...].astype(BF16)
        for h in range(HG_HEADS):
            cs = slice(h * HG_DK, (h + 1) * HG_DK)
            att = lax.dot_general(qd[:, cs], ki[:, cs], _NT, preferred_element_type=F32)
            att = jnp.where(causal, att, 0.0).astype(BF16)
            o = jnp.dot(att, va[:, cs], preferred_element_type=F32)
            st = st_scr[h]
            for n in range(tc // ch):
                rs = slice(n * ch, (n + 1) * ch)
                o_scr[rs, cs] = o[rs] + lax.dot_general(qd[rs, cs], st.astype(BF16), _NT,
                                                        preferred_element_type=F32)
                upd = lax.dot_general(va[rs, cs], kd[rs, cs], _TN, preferred_element_type=F32)
                st = jnp.exp(btot[n * ch:n * ch + 1, cs]) * st + upd
            st_scr[h] = st

    nb = HG_BLOCK
    tri = jnp.where(lax.broadcasted_iota(I32, (nb, nb), 0) >= lax.broadcasted_iota(I32, (nb, nb), 1),
                    1.0, 0.0).astype(F32)
    trow = lax.broadcasted_iota(I32, (nb, 1), 0)

    def block(i, carry):
        r0 = pl.multiple_of(i * nb, nb)
        rows = pl.ds(r0, nb)
        for h in range(HG_HEADS):
            cs = slice(h * HG_DK, (h + 1) * HG_DK)
            bcum = jnp.dot(tri, lf_scr[rows, cs], preferred_element_type=F32,
                           precision=lax.Precision.HIGHEST)
            qb = q_scr[rows, cs]
            kb = k_scr[rows, cs]
            vb = v_scr[rows, cs]
            o = jnp.zeros((nb, HG_DV), F32)
            for s in range(nb):
                e = jnp.exp(jnp.where(trow >= s, bcum - bcum[s:s + 1, :], -jnp.inf))
                a_ts = jnp.sum(qb * kb[s:s + 1, :] * e, axis=1, keepdims=True)
                o = o + a_ts * vb[s:s + 1, :]
            st = st_scr[h]
            qd = (qb * jnp.exp(bcum)).astype(BF16)
            o = o + lax.dot_general(qd, st.astype(BF16), _NT, preferred_element_type=F32)
            blast = bcum[nb - 1:nb, :]
            kd = (kb * jnp.exp(blast - bcum)).astype(BF16)
            upd = lax.dot_general(vb.astype(BF16), kd, _TN, preferred_element_type=F32)
            st_scr[h] = jnp.exp(blast) * st + upd
            o_scr[rows, cs] = o
        return carry

    @pl.when(jnp.logical_not(factored))
    def _():
        lax.fori_loop(0, tc // nb, block, 0)

    for h in range(HG_HEADS):
        cs = slice(h * HG_DV, (h + 1) * HG_DV)
        o = o_scr[:, cs]
        o = o * lax.rsqrt(jnp.mean(o * o, axis=-1, keepdims=True) + LN_EPS)
        g = hg[:, cs]
        c_ref[0, :, cs] = (o * ng_ref[:, cs] * (g * _sigmoid(g))).astype(BF16)

    @pl.when(t == pl.num_programs(1) - 1)
    def _():
        for h in range(HG_HEADS):
            sout_ref[0, h] = st_scr[h].T


def _hgrn(x, w, loglb, log1mlb, omlb, ng, s0, *, tc):
    b, t, _ = x.shape
    xmap = lambda bi, j: (bi, j, 0)
    vec = pl.BlockSpec((1, HG_WIDTH), lambda bi, j: (0, 0))
    smap = lambda bi, j: (bi, 0, 0, 0)
    return pl.pallas_call(
        functools.partial(_hgrn_kernel, tc=tc),
        grid=(b, t // tc),
        in_specs=[pl.BlockSpec((1, tc, D_MODEL), xmap),
                  pl.BlockSpec((D_MODEL, 4 * HG_WIDTH), lambda bi, j: (0, 0)),
                  vec, vec, vec, vec,
                  pl.BlockSpec((1, HG_HEADS, HG_DK, HG_DV), smap)],
        out_specs=[pl.BlockSpec((1, tc, HG_WIDTH), xmap),
                   pl.BlockSpec((1, HG_HEADS, HG_DK, HG_DV), smap)],
        out_shape=[jax.ShapeDtypeStruct((b, t, HG_WIDTH), BF16),
                   jax.ShapeDtypeStruct((b, HG_HEADS, HG_DK, HG_DV), F32)],
        scratch_shapes=[pltpu.VMEM((tc, HG_WIDTH), F32)] * 5
                       + [pltpu.VMEM((HG_HEADS, HG_DV, HG_DK), F32)],
        compiler_params=_params(("parallel", "arbitrary")),
        name="hgrn",
    )(x, w, loglb, log1mlb, omlb, ng, s0)


def _merge_kernel(x_ref, a_ref, b_ref, c_ref, wg_ref, wa_ref, wb_ref, wc_ref, wo_ref,
                  g_ref, bt_ref, y_ref):
    x = x_ref[...]
    xb = x.astype(BF16)
    m = None
    for i, (br, w) in enumerate(((a_ref, wa_ref), (b_ref, wb_ref), (c_ref, wc_ref))):
        gate = _sigmoid(jnp.dot(xb, wg_ref[:, i * D_MODEL:(i + 1) * D_MODEL],
                                preferred_element_type=F32))
        term = gate * jnp.dot(br[...], w[...], preferred_element_type=F32)
        m = term if m is None else m + term
    mo = jnp.dot(m.astype(BF16), wo_ref[...], preferred_element_type=F32)
    y_ref[...] = _layer_norm(ALPHA * x + mo, g_ref[...], bt_ref[...])


def _merge(x, a, b, c, wg, wa, wb, wc, wo, ln_g, ln_b, *, tm):
    n = x.shape[0]
    row = lambda i: (i, 0)
    full = lambda i: (0, 0)
    br = pl.BlockSpec((tm, ATT_WIDTH), row)
    bw = pl.BlockSpec((ATT_WIDTH, D_MODEL), full)
    vec = pl.BlockSpec((1, D_MODEL), full)
    return pl.pallas_call(
        _merge_kernel,
        grid=(n // tm,),
        in_specs=[pl.BlockSpec((tm, D_MODEL), row), br, br, br,
                  pl.BlockSpec((D_MODEL, N_BRANCH * D_MODEL), full), bw, bw, bw,
                  pl.BlockSpec((D_MODEL, D_MODEL), full), vec, vec],
        out_specs=pl.BlockSpec((tm, D_MODEL), row),
        out_shape=jax.ShapeDtypeStruct((n, D_MODEL), F32),
        compiler_params=_params(("parallel",)),
        name="merge",
    )(x, a, b, c, wg, wa, wb, wc, wo, ln_g, ln_b)


def _route(logits):
    ex = jnp.exp(logits - jnp.max(logits, axis=0, keepdims=True))
    probs = ex / jnp.sum(ex, axis=0, keepdims=True)
    p = [[probs[g * EXP_PER_GROUP + k:g * EXP_PER_GROUP + k + 1, :] for k in range(EXP_PER_GROUP)]
         for g in range(N_GROUPS)]
    score = []
    for g in range(N_GROUPS):
        best = None
        for k1 in range(EXP_PER_GROUP):
            for k2 in range(k1 + 1, EXP_PER_GROUP):
                pair = p[g][k1] + p[g][k2]
                best = pair if best is None else jnp.maximum(best, pair)
        score.append(best)
    gsel = jnp.zeros(score[0].shape, I32)
    top = score[0]
    for g in range(1, N_GROUPS):
        better = score[g] > top
        top = jnp.where(better, score[g], top)
        gsel = jnp.where(better, g, gsel)
    val = []
    for k in range(EXP_PER_GROUP):
        v = p[0][k]
        for g in range(1, N_GROUPS):
            v = jnp.where(gsel == g, p[g][k], v)
        val.append(v)
    v1, i1 = val[0], jnp.zeros(gsel.shape, I32)
    for k in range(1, EXP_PER_GROUP):
        better = val[k] > v1
        v1 = jnp.where(better, val[k], v1)
        i1 = jnp.where(better, k, i1)
    v2, i2 = jnp.full(v1.shape, -1.0, F32), jnp.zeros(gsel.shape, I32)
    for k in range(EXP_PER_GROUP):
        better = (i1 != k) & (val[k] > v2)
        v2 = jnp.where(better, val[k], v2)
        i2 = jnp.where(better, k, i2)
    den = v1 + v2
    w1, w2 = v1 / den, v2 / den
    out = []
    for g in range(N_GROUPS):
        rows = [jnp.where(gsel == g, jnp.where(i1 == k, w1, jnp.where(i2 == k, w2, 0.0)), 0.0)
                for k in range(EXP_PER_GROUP)]
        out.append(jnp.concatenate(rows, axis=0))
    return out


def _moe_kernel(x_ref, wr_ref, br_ref, wg_ref, wu_ref, wd_ref, g_ref, bt_ref, y_ref,
                xb_scr, gate_scr, acc_scr, *, tm):
    grp = pl.program_id(1)

    @pl.when(grp == 0)
    def _():
        xb = x_ref[...].astype(BF16)
        xb_scr[...] = xb
        logits = lax.dot_general(wr_ref[...], xb, _NT, preferred_element_type=F32) + br_ref[...]
        gates = _route(logits)
        pad = jnp.zeros((LANES - EXP_PER_GROUP, tm), F32)
        for g in range(N_GROUPS):
            gate_scr[g] = jnp.concatenate([gates[g], pad], axis=0).T
        acc_scr[...] = jnp.zeros(acc_scr.shape, F32)

    xb = xb_scr[...]
    gate = gate_scr[grp]
    acc = acc_scr[...]
    for k in range(EXP_PER_GROUP):
        h = jnp.dot(xb, wg_ref[k], preferred_element_type=F32)
        up = jnp.dot(xb, wu_ref[k], preferred_element_type=F32)
        act = h * _sigmoid(h) * up * gate[:, k:k + 1]
        acc = acc + jnp.dot(act.astype(BF16), wd_ref[k], preferred_element_type=F32)
    acc_scr[...] = acc

    @pl.when(grp == N_GROUPS - 1)
    def _():
        y_ref[...] = _layer_norm(ALPHA * x_ref[...] + acc, g_ref[...], bt_ref[...])


def _moe(x, wr_t, br, wg, wu, wd, ln_g, ln_b, *, tm):
    n = x.shape[0]
    row = lambda i, g: (i, 0)
    full = lambda i, g: (0, 0)
    vec = pl.BlockSpec((1, D_MODEL), full)
    return pl.pallas_call(
        functools.partial(_moe_kernel, tm=tm),
        grid=(n // tm, N_GROUPS),
        in_specs=[pl.BlockSpec((tm, D_MODEL), row),
                  pl.BlockSpec((N_EXPERTS, D_MODEL), full),
                  pl.BlockSpec((N_EXPERTS, 1), full),
                  pl.BlockSpec((EXP_PER_GROUP, D_MODEL, D_FF), lambda i, g: (g, 0, 0)),
                  pl.BlockSpec((EXP_PER_GROUP, D_MODEL, D_FF), lambda i, g: (g, 0, 0)),
                  pl.BlockSpec((EXP_PER_GROUP, D_FF, D_MODEL), lambda i, g: (g, 0, 0)),
                  vec, vec],
        out_specs=pl.BlockSpec((tm, D_MODEL), row),
        out_shape=jax.ShapeDtypeStruct((n, D_MODEL), F32),
        scratch_shapes=[pltpu.VMEM((tm, D_MODEL), BF16),
                        pltpu.VMEM((N_GROUPS, tm, LANES), F32),
                        pltpu.VMEM((tm, D_MODEL), F32)],
        compiler_params=_params(("parallel", "arbitrary")),
        name="moe",
    )(x, wr_t, br, wg, wu, wd, ln_g, ln_b)


def _split_w_in(w_in_l):
    offs = [0]
    for n in SPLIT_SIZES:
        offs.append(offs[-1] + n)
    seg = lambda i, j=None: w_in_l[:, offs[i]:offs[i + 1 if j is None else j]]
    zeros = lambda c: jnp.zeros((D_MODEL, c), w_in_l.dtype)
    w_att_n = jnp.concatenate([seg(1), seg(2), seg(4), zeros(LANES - IDX_DIM)], axis=1)
    w_att_t = jnp.concatenate([seg(0), seg(3), seg(2), seg(5), zeros(8 - IDX_HEADS)], axis=1).T
    return (w_att_n.astype(BF16), w_att_t.astype(BF16), seg(6).astype(BF16), seg(7).astype(BF16),
            seg(8, 12).astype(BF16), seg(12).astype(BF16))


def _row_tile(n, want):
    while n % want:
        want //= 2
    return want


def _layer(x, bsz, t, lw, *, pos0, topk, cache, s0):
    n = bsz * t
    lc = min(t, SG_LEN)
    k, v, ki, kb, kib, qt, qit, vt, wit = _attn_proj(x, lw["w_att_n"], lw["w_att_t"], bsz, t,
                                                     _row_tile(t, 512))
    kb3 = kb.reshape(bsz, t, KV_WIDTH)
    kib3 = kib.reshape(bsz, t, IDX_DIM)
    if cache is not None:
        ck, cvt, cki = cache
        kb3 = jnp.concatenate([ck, kb3], axis=1)
        kib3 = jnp.concatenate([cki, kib3], axis=1)
        vt = jnp.concatenate([cvt, vt], axis=2)
    ltot = kb3.shape[1]
    lp = -(-ltot // DSA_KEY_BLOCK) * DSA_KEY_BLOCK
    tq = LANES
    tp = -(-t // tq) * tq
    padq = lambda a: jnp.pad(a, ((0, 0), (0, 0), (0, tp - t)))
    a = _dsa(padq(qt), padq(qit), padq(wit), jnp.pad(kb3, ((0, 0), (0, lp - ltot), (0, 0))),
             jnp.pad(vt, ((0, 0), (0, 0), (0, lp - ltot))),
             jnp.pad(kib3, ((0, 0), (0, lp - ltot), (0, 0))), tq=tq, pos0=pos0, topk=topk)[:, :t]
    b, v_gm = _gmlp(x, lw["w_u"], lw["w_v"], lw["ln_sg_g"], lw["ln_sg_b"],
                    lw["w_sg"][:, :lc, :lc], lw["b_sg"][:, :lc].T, tm=_row_tile(n, 256), lc=lc)
    c, s_new = _hgrn(x.reshape(bsz, t, D_MODEL), lw["w_hg"], lw["log_lb"], lw["log1m_lb"],
                     lw["om_lb"], lw["hg_norm_g"], s0, tc=min(t, 256))
    x1 = _merge(x, a.reshape(n, ATT_WIDTH), b, c.reshape(n, HG_WIDTH), lw["w_gates"],
                lw["w_branch_a"], lw["w_branch_b"], lw["w_branch_c"], lw["w_out"],
                lw["ln1_g"], lw["ln1_b"], tm=_row_tile(n, 256))
    x2 = _moe(x1, lw["w_router_t"], lw["b_router"], lw["w_exp_gate"], lw["w_exp_up"],
              lw["w_exp_down"], lw["ln2_g"], lw["ln2_b"], tm=_row_tile(n, 512))
    return x2, k, v, ki, s_new, v_gm


def kernel(x_prompt, x_sample, cache_k, cache_v, cache_kidx, state_hgrn, w_in, w_sg, b_sg, ln_sg_g, ln_sg_b, hg_lb_logits, hg_norm_g, w_branch_a, w_branch_b, w_branch_c, w_out, ln1_g, ln1_b, w_router, b_router, w_exp_gate, w_exp_up, w_exp_down, ln2_g, ln2_b):
    bp, sp, _ = x_prompt.shape
    bs, ss, _ = x_sample.shape
    past = cache_k.shape[2]
    topk_p = min(TOPK_MAX, sp // 4)
    topk_s = min(TOPK_MAX, (past + ss) // 4)

    lb_all = jnp.cumsum(jax.nn.softmax(hg_lb_logits.astype(F32), axis=0), axis=0)
    lb_all = lb_all - lb_all[0:1]
    vec = lambda a: a.reshape(1, -1)

    xp = x_prompt.reshape(bp * sp, D_MODEL)
    xs = x_sample.reshape(bs * ss, D_MODEL)
    s0_p = jnp.zeros((bp, HG_HEADS, HG_DK, HG_DV), F32)
    outs_p, outs_s = [], []
    for l in range(DEPTH):
        w_att_n, w_att_t, w_u, w_v, w_hg, w_gates = _split_w_in(w_in[l])
        lw = dict(
            w_att_n=w_att_n, w_att_t=w_att_t, w_u=w_u, w_v=w_v, w_hg=w_hg, w_gates=w_gates,
            ln_sg_g=vec(ln_sg_g[l]), ln_sg_b=vec(ln_sg_b[l]), w_sg=w_sg[l], b_sg=b_sg[l],
            log_lb=vec(jnp.log(lb_all[l])), log1m_lb=vec(jnp.log1p(-lb_all[l])),
            om_lb=vec(1.0 - lb_all[l]), hg_norm_g=vec(hg_norm_g[l].astype(F32)),
            w_branch_a=w_branch_a[l].astype(BF16), w_branch_b=w_branch_b[l].astype(BF16),
            w_branch_c=w_branch_c[l].astype(BF16), w_out=w_out[l].astype(BF16),
            ln1_g=vec(ln1_g[l]), ln1_b=vec(ln1_b[l]),
            w_router_t=w_router.T.astype(BF16), b_router=b_router.astype(F32).reshape(-1, 1),
            w_exp_gate=w_exp_gate[l].astype(BF16), w_exp_up=w_exp_up[l].astype(BF16),
            w_exp_down=w_exp_down[l].astype(BF16),
            ln2_g=vec(ln2_g[l]), ln2_b=vec(ln2_b[l]))
        xp, k, v, ki, s_new, _ = _layer(xp, bp, sp, lw, pos0=0, topk=topk_p, cache=None, s0=s0_p)
        outs_p.append((k.reshape(bp, sp, KV_HEADS, HEAD_DIM), v.reshape(bp, sp, KV_HEADS, HEAD_DIM),
                       ki.reshape(bp, sp, IDX_DIM), s_new))
        cache = (cache_k[l].reshape(bs, past, KV_WIDTH).astype(BF16),
                 jnp.swapaxes(cache_v[l].reshape(bs, past, KV_WIDTH), 1, 2).astype(BF16),
                 cache_kidx[l].astype(BF16))
        xs, k, v, ki, s_new, v_gm = _layer(xs, bs, ss, lw, pos0=past, topk=topk_s, cache=cache,
                                           s0=state_hgrn[l].astype(F32))
        outs_s.append((k.reshape(bs, ss, KV_HEADS, HEAD_DIM), v.reshape(bs, ss, KV_HEADS, HEAD_DIM),
                       ki.reshape(bs, ss, IDX_DIM), s_new, v_gm.reshape(bs, ss, SG_WIDTH)))

    stack = lambda rows, i: jnp.stack([r[i] for r in rows])
    return (xp.reshape(bp, sp, D_MODEL), xs.reshape(bs, ss, D_MODEL),
            stack(outs_p, 0), stack(outs_p, 1), stack(outs_p, 2), stack(outs_p, 3),
            stack(outs_s, 0), stack(outs_s, 1), stack(outs_s, 2), stack(outs_s, 3),
            stack(outs_s, 4))
```

```python
import functools

import jax
import jax.numpy as jnp
from jax import lax
from jax.experimental import pallas as pl
from jax.experimental.pallas import tpu as pltpu

F32 = jnp.float32
BF16 = jnp.bfloat16
I32 = jnp.int32

D_MODEL = 1024
DEPTH = 4
CHUNK = 64
ATT_HEADS = 8
KV_HEADS = 2
HEAD_DIM = 64
ATT_WIDTH = ATT_HEADS * HEAD_DIM
KV_WIDTH = KV_HEADS * HEAD_DIM
IDX_HEADS = 4
IDX_DIM = 64
TOPK_MAX = 256
SG_LEN = 128
SG_GROUPS = 4
SG_WIDTH = 512
SG_GW = SG_WIDTH // SG_GROUPS
HG_HEADS = 4
HG_DK = 128
HG_DV = 128
HG_WIDTH = HG_HEADS * HG_DV
HG_BLOCK = 16
HG_FACTOR_RANGE = 80.0
N_BRANCH = 3
N_EXPERTS = 16
N_GROUPS = 4
EXP_PER_GROUP = 4
D_FF = 256
ALPHA = (2 * DEPTH) ** 0.25
LN_EPS = 1e-5

SPLIT_SIZES = (ATT_WIDTH, KV_WIDTH, KV_WIDTH, IDX_HEADS * IDX_DIM, IDX_DIM, IDX_HEADS,
               SG_WIDTH, SG_WIDTH, HG_HEADS * HG_DK, HG_HEADS * HG_DK, HG_WIDTH, HG_WIDTH,
               N_BRANCH * D_MODEL)

LANES = 128
INT_MIN = -2 ** 31
NEG_BIG = -1e30
LOG2_E = 1.4426950408889634
DSA_KEY_BLOCK = 512
VMEM_LIMIT = 56 * 1024 * 1024

_NT = (((1,), (1,)), ((), ()))
_TN = (((0,), (0,)), ((), ()))


def _params(sem):
    return pltpu.CompilerParams(dimension_semantics=sem, vmem_limit_bytes=VMEM_LIMIT)


def _layer_norm(x, g, b):
    mu = jnp.mean(x, axis=-1, keepdims=True)
    d = x - mu
    var = jnp.mean(d * d, axis=-1, keepdims=True)
    return d * lax.rsqrt(var + LN_EPS) * g + b


def _gelu_tanh(x):
    return 0.5 * x * (1.0 + jnp.tanh(0.7978845608028654 * (x + 0.044715 * (x * x * x))))


def _sigmoid(x):
    return 1.0 / (1.0 + jnp.exp(-x))


def _fold_rows(x, op):
    parts = [x[i:i + 8] for i in range(0, x.shape[0], 8)]
    while len(parts) > 1:
        parts = [op(parts[i], parts[i + 1]) for i in range(0, len(parts), 2)]
    return parts[0]


ATT_T_ROWS = ATT_WIDTH + IDX_HEADS * IDX_DIM + KV_WIDTH + 8
ATT_T_PAD = -(-ATT_T_ROWS // LANES) * LANES
ATT_N_COLS = 3 * LANES


def _attn_proj_kernel(x_ref, wn_ref, wt_ref, k_ref, v_ref, ki_ref, kb_ref, kib_ref,
                      qt_ref, qit_ref, vt_ref, wit_ref):
    xb = x_ref[...].astype(BF16)
    z = jnp.dot(xb, wn_ref[...], preferred_element_type=F32)
    k = z[:, 0:KV_WIDTH]
    ki = z[:, 2 * KV_WIDTH:2 * KV_WIDTH + IDX_DIM]
    k_ref[...] = k
    v_ref[...] = z[:, KV_WIDTH:2 * KV_WIDTH]
    ki_ref[...] = ki
    kb_ref[...] = k.astype(BF16)
    kib_ref[...] = ki.astype(BF16)
    zt = lax.dot_general(wt_ref[...], xb, _NT, preferred_element_type=F32)
    o = 0
    qt_ref[0] = (zt[o:o + ATT_WIDTH] * (HEAD_DIM ** -0.5 * LOG2_E)).astype(BF16)
    o += ATT_WIDTH
    qit_ref[0] = (zt[o:o + IDX_HEADS * IDX_DIM] * IDX_DIM ** -0.5).astype(BF16)
    o += IDX_HEADS * IDX_DIM
    vt_ref[0] = zt[o:o + KV_WIDTH].astype(BF16)
    o += KV_WIDTH
    wit_ref[0] = zt[o:o + 8] * IDX_HEADS ** -0.5


def _attn_proj(x, wn, wt, bsz, t, tm):
    n = bsz * t
    per = t // tm
    row = lambda i: (i, 0)
    tmap = lambda i: (i // per, 0, i % per)
    nat = ((KV_WIDTH, F32), (KV_WIDTH, F32), (IDX_DIM, F32), (KV_WIDTH, BF16), (IDX_DIM, BF16))
    tr = ((ATT_WIDTH, BF16), (IDX_HEADS * IDX_DIM, BF16), (KV_WIDTH, BF16), (8, F32))
    return pl.pallas_call(
        _attn_proj_kernel,
        grid=(n // tm,),
        in_specs=[pl.BlockSpec((tm, D_MODEL), row),
                  pl.BlockSpec((D_MODEL, ATT_N_COLS), lambda i: (0, 0)),
                  pl.BlockSpec((ATT_T_ROWS, D_MODEL), lambda i: (0, 0))],
        out_specs=[pl.BlockSpec((tm, c), row) for c, _ in nat]
                  + [pl.BlockSpec((1, c, tm), tmap) for c, _ in tr],
        out_shape=[jax.ShapeDtypeStruct((n, c), d) for c, d in nat]
                  + [jax.ShapeDtypeStruct((bsz, c, t), d) for c, d in tr],
        compiler_params=_params(("parallel",)),
        name="attn_proj",
    )(x, wn, wt)


def _dsa_kernel(qt_ref, qit_ref, wit_ref, kb_ref, vt_ref, kib_ref, o_ref,
                key_scr, hi_scr, lo_scr, j_scr, s_scr, acc_scr, m_scr, a_scr,
                *, tq, kblk, pos0, topk, idx_bits):
    j = pl.program_id(1)
    qpos0 = pos0 + j * tq
    qcol = lax.broadcasted_iota(I32, (1, tq), 1)
    chunk_end = ((qpos0 + qcol) // CHUNK + 1) * CHUNK
    lvis = ((qpos0 + tq - 1) // CHUNK + 1) * CHUNK
    nblk = (lvis + kblk - 1) // kblk
    krow = lax.broadcasted_iota(I32, (kblk, tq), 0)

    wi = wit_ref[0]
    qit = qit_ref[0]

    def score_block(i, carry):
        off = pl.multiple_of(i * kblk, kblk)
        kib = kib_ref[0, pl.ds(off, kblk), :]
        sc = jnp.zeros((kblk, tq), F32)
        for h in range(IDX_HEADS):
            raw = jnp.dot(kib, qit[h * IDX_DIM:(h + 1) * IDX_DIM, :], preferred_element_type=F32)
            sc = sc + jnp.maximum(raw, 0.0) * wi[h:h + 1, :]
        bits = pltpu.bitcast(sc, I32)
        key = bits ^ ((bits >> 31) & 0x7FFFFFFF)
        key = jnp.where(off + krow < chunk_end, key, INT_MIN)
        key_scr[i] = key
        hi_scr[i] = (key >> 16).astype(jnp.int16)
        lo_scr[i] = ((key & 0xFFFF) - 32768).astype(jnp.int16)
        return carry

    lax.fori_loop(0, nblk, score_block, 0)

    def count(hits):
        def body(i, acc):
            return acc + _fold_rows(hits(i, key_scr[i]), jnp.add)
        acc = lax.fori_loop(0, nblk, body, jnp.zeros((8, tq), F32))
        return jnp.sum(acc, axis=0, keepdims=True)

    i16 = jnp.int16

    def count16(src, pred):
        def body(i, acc):
            hit = jnp.where(pred(src[i]), i16(1), i16(0))
            parts = [hit[r:r + 16] for r in range(0, kblk, 16)]
            while len(parts) > 1:
                parts = [parts[r] + parts[r + 1] for r in range(0, len(parts), 2)]
            return acc + parts[0]
        acc = lax.fori_loop(0, nblk, body, jnp.zeros((16, tq), i16))
        return jnp.sum(acc.astype(F32), axis=0, keepdims=True)

    def half_search(src, above):
        def bit(i, c):
            t, n_t = c
            cand = t + jnp.left_shift(jnp.int32(1), 15 - i)
            c16 = cand.astype(i16)
            cnt = above + count16(src, lambda x: x >= c16)
            ok = cnt >= topk
            return jnp.where(ok, cand, t), jnp.where(ok, cnt, n_t)

        n_all = (nblk * kblk).astype(F32) + jnp.zeros((1, tq), F32)
        return lax.fori_loop(0, 16, bit, (jnp.full((1, tq), -32768, I32), n_all))

    thr_hi, _ = half_search(hi_scr, 0.0)
    thi16 = thr_hi.astype(i16)

    def mask_low(i, carry):
        lo_scr[i] = jnp.where(hi_scr[i] == thi16, lo_scr[i], i16(-32768))
        return carry

    lax.fori_loop(0, nblk, mask_low, 0)
    thr_lo, n_ge = half_search(lo_scr, count16(hi_scr, lambda x: x > thi16))
    thr = thr_hi * 65536 + (thr_lo + 32768)

    j_scr[...] = jnp.full((1, tq), 2 ** 30, I32)

    @pl.when(jnp.max(n_ge) > topk)
    def _():
        need = topk - count(lambda _, kb: jnp.where(kb > thr, 1.0, 0.0))

        def idx_bit(i, jmax):
            cand = jmax + jnp.left_shift(jnp.int32(1), idx_bits - 1 - i)
            cnt = count(lambda b, kb: jnp.where(
                kb == thr, jnp.where(b * kblk + krow < cand, 1.0, 0.0), 0.0))
            return jnp.where(cnt < need, cand, jmax)

        j_scr[...] = lax.fori_loop(0, idx_bits, idx_bit, jnp.zeros((1, tq), I32))

    jmax = j_scr[...]

    m_scr[...] = jnp.full(m_scr.shape, NEG_BIG, F32)
    acc_scr[...] = jnp.zeros(acc_scr.shape, F32)
    rep = ATT_HEADS // KV_HEADS
    qt = qt_ref[0]
    qg = [jnp.concatenate([qt[(g * rep + r) * HEAD_DIM:(g * rep + r + 1) * HEAD_DIM, :]
                           for r in range(rep)], axis=1) for g in range(KV_HEADS)]
    ones = jnp.ones((16, kblk), BF16)

    def score_phase(i, buf):
        off = pl.multiple_of(i * kblk, kblk)
        kb = key_scr[i]
        kpos = off + krow
        tie = jnp.where(kpos <= jmax, 0.0, NEG_BIG)
        bias = jnp.where(kb > thr, 0.0, jnp.where(kb == thr, tie, NEG_BIG))
        bias = jnp.where(kpos < chunk_end, bias, NEG_BIG)
        bias = jnp.concatenate([bias] * rep, axis=1)
        kk = kb_ref[0, pl.ds(off, kblk), :]
        for g in range(KV_HEADS):
            s = jnp.dot(kk[:, g * HEAD_DIM:(g + 1) * HEAD_DIM], qg[g],
                        preferred_element_type=F32) + bias
            s_scr[buf, g] = s
            m_blk = jnp.max(_fold_rows(s, jnp.maximum), axis=0, keepdims=True)
            m_old = m_scr[g]
            m_new = jnp.maximum(m_old, m_blk)
            a_scr[g] = jnp.exp2(m_old - m_new)
            m_scr[g] = m_new

    def value_phase(i, buf):
        off = pl.multiple_of(i * kblk, kblk)
        vv = vt_ref[0, :, pl.ds(off, kblk)]
        for g in range(KV_HEADS):
            p = jnp.exp2(s_scr[buf, g] - m_scr[g]).astype(BF16)
            vg = jnp.concatenate([vv[g * HEAD_DIM:(g + 1) * HEAD_DIM, :], ones], axis=0)
            acc_scr[g] = a_scr[g] * acc_scr[g] + jnp.dot(vg, p, preferred_element_type=F32)

    score_phase(0, 0)

    def attn_pair(k, carry):
        i = 2 * k
        value_phase(i, 0)
        score_phase(i + 1, 1)
        value_phase(i + 1, 1)
        score_phase(jnp.minimum(i + 2, nblk - 1), 0)
        return carry

    lax.fori_loop(0, nblk // 2, attn_pair, 0)

    @pl.when(nblk % 2 == 1)
    def _():
        value_phase(nblk - 1, 0)

    heads = []
    for g in range(KV_HEADS):
        acc = acc_scr[g]
        og = acc[:HEAD_DIM] / acc[HEAD_DIM:HEAD_DIM + 1]
        heads += [og[:, r * tq:(r + 1) * tq] for r in range(rep)]
    o_ref[0] = jnp.concatenate(heads, axis=0).T.astype(BF16)


def _dsa(qt, qit, wit, kb, vt, kib, *, tq, pos0, topk):
    b, _, t = qt.shape
    lp = kb.shape[1]
    kblk = DSA_KEY_BLOCK
    assert lp % kblk == 0 and t % tq == 0 and tq % LANES == 0
    kern = functools.partial(_dsa_kernel, tq=tq, kblk=kblk, pos0=pos0, topk=topk,
                             idx_bits=(lp - 1).bit_length())
    rep = ATT_HEADS // KV_HEADS
    qmap = lambda bi, j: (bi, 0, j)
    kmap = lambda bi, j: (bi, 0, 0)
    return pl.pallas_call(
        kern,
        grid=(b, t // tq),
        in_specs=[pl.BlockSpec((1, ATT_WIDTH, tq), qmap),
                  pl.BlockSpec((1, IDX_HEADS * IDX_DIM, tq), qmap),
                  pl.BlockSpec((1, 8, tq), qmap),
                  pl.BlockSpec((1, lp, KV_WIDTH), kmap),
                  pl.BlockSpec((1, KV_WIDTH, lp), kmap),
                  pl.BlockSpec((1, lp, IDX_DIM), kmap)],
        out_specs=pl.BlockSpec((1, tq, ATT_WIDTH), lambda bi, j: (bi, j, 0)),
        out_shape=jax.ShapeDtypeStruct((b, t, ATT_WIDTH), BF16),
        scratch_shapes=[pltpu.VMEM((lp // kblk, kblk, tq), I32),
                        pltpu.VMEM((lp // kblk, kblk, tq), jnp.int16),
                        pltpu.VMEM((lp // kblk, kblk, tq), jnp.int16),
                        pltpu.VMEM((1, tq), I32),
                        pltpu.VMEM((2, KV_HEADS, kblk, rep * tq), F32),
                        pltpu.VMEM((KV_HEADS, HEAD_DIM + 16, rep * tq), F32),
                        pltpu.VMEM((KV_HEADS, 1, rep * tq), F32),
                        pltpu.VMEM((KV_HEADS, 1, rep * tq), F32)],
        compiler_params=_params(("parallel", "arbitrary")),
        name="dsa",
    )(qt, qit, wit, kb, vt, kib)


def _gmlp_kernel(x_ref, wu_ref, wv_ref, g_ref, b_ref, ws_ref, bst_ref, o_ref, v_ref, *, tm, lc):
    xb = x_ref[...].astype(BF16)
    u = _gelu_tanh(jnp.dot(xb, wu_ref[...], preferred_element_type=F32))
    v = _layer_norm(_gelu_tanh(jnp.dot(xb, wv_ref[...], preferred_element_type=F32)),
                    g_ref[...], b_ref[...])
    v_ref[...] = v
    vb = v.astype(BF16)
    r = lax.broadcasted_iota(I32, (lc, lc), 0)
    c = lax.broadcasted_iota(I32, (lc, lc), 1)
    for g in range(SG_GROUPS):
        w = jnp.where(r >= c, ws_ref[g], 0.0).astype(BF16)
        bias = bst_ref[:, g:g + 1]
        cs = slice(g * SG_GW, (g + 1) * SG_GW)
        for n in range(tm // lc):
            rs = slice(n * lc, (n + 1) * lc)
            mixed = jnp.dot(w, vb[rs, cs], preferred_element_type=F32) + bias
            o_ref[rs, cs] = (u[rs, cs] * mixed).astype(BF16)


def _gmlp(x, wu, wv, ln_g, ln_b, ws, bst, *, tm, lc):
    n = x.shape[0]
    row = lambda i: (i, 0)
    full2 = lambda i: (0, 0)
    return pl.pallas_call(
        functools.partial(_gmlp_kernel, tm=tm, lc=lc),
        grid=(n // tm,),
        in_specs=[pl.BlockSpec((tm, D_MODEL), row),
                  pl.BlockSpec((D_MODEL, SG_WIDTH), full2),
                  pl.BlockSpec((D_MODEL, SG_WIDTH), full2),
                  pl.BlockSpec((1, SG_WIDTH), full2),
                  pl.BlockSpec((1, SG_WIDTH), full2),
                  pl.BlockSpec((SG_GROUPS, lc, lc), lambda i: (0, 0, 0)),
                  pl.BlockSpec((lc, SG_GROUPS), full2)],
        out_specs=[pl.BlockSpec((tm, SG_WIDTH), row), pl.BlockSpec((tm, SG_WIDTH), row)],
        out_shape=[jax.ShapeDtypeStruct((n, SG_WIDTH), BF16),
                   jax.ShapeDtypeStruct((n, SG_WIDTH), F32)],
        compiler_params=_params(("parallel",)),
        name="gmlp",
    )(x, wu, wv, ln_g, ln_b, ws, bst)


def _hgrn_kernel(x_ref, w_ref, loglb_ref, log1mlb_ref, omlb_ref, ng_ref, s0_ref, c_ref, sout_ref,
                 q_scr, k_scr, v_scr, lf_scr, o_scr, st_scr, *, tc):
    t = pl.program_id(1)

    @pl.when(t == 0)
    def _():
        for h in range(HG_HEADS):
            st_scr[h] = s0_ref[0, h].T

    xb = x_ref[0].astype(BF16)
    z = jnp.dot(xb, w_ref[...], preferred_element_type=F32)
    hq = z[:, 0:HG_WIDTH]
    hf = z[:, HG_WIDTH:2 * HG_WIDTH]
    hg = z[:, 3 * HG_WIDTH:4 * HG_WIDTH]
    q_scr[...] = hq * _sigmoid(hq) * HG_DK ** -0.5
    log_sig = jnp.minimum(hf, 0.0) - jnp.log1p(jnp.exp(-jnp.abs(hf)))
    y = log1mlb_ref[...] + log_sig
    a = loglb_ref[...]
    lf_scr[...] = jnp.maximum(a, y) + jnp.log1p(jnp.exp(-jnp.abs(a - y)))
    k_scr[...] = omlb_ref[...] * _sigmoid(-hf)
    v_scr[...] = z[:, 2 * HG_WIDTH:3 * HG_WIDTH]

    ch = min(tc, CHUNK)
    ri = lax.broadcasted_iota(I32, (tc, tc), 0)
    ci = lax.broadcasted_iota(I32, (tc, tc), 1)
    same = (ri // ch) == (ci // ch)
    causal = same & (ri >= ci)
    lf = lf_scr[...]
    bcum_all = jnp.dot(jnp.where(causal, 1.0, 0.0), lf, preferred_element_type=F32,
                       precision=lax.Precision.HIGHEST)
    factored = jnp.min(bcum_all) > -HG_FACTOR_RANGE

    @pl.when(factored)
    def _():
        btot = jnp.dot(jnp.where(same, 1.0, 0.0), lf, preferred_element_type=F32,
                       precision=lax.Precision.HIGHEST)
        qa = q_scr[...]
        ka = k_scr[...]
        qd = (qa * jnp.exp(bcum_all)).astype(BF16)
        ki = (ka * jnp.exp(-bcum_all)).astype(BF16)
        kd = (ka * jnp.exp(btot - bcum_all)).astype(BF16)
        va = v_scr[...].astype(BF16)
        for h in range(HG_HEADS):
            cs = slice(h * HG_DK, (h + 1) * HG_DK)
            att = lax.dot_general(qd[:, cs], ki[:, cs], _NT, preferred_element_type=F32)
            att = jnp.where(causal, att, 0.0).astype(BF16)
            o = jnp.dot(att, va[:, cs], preferred_element_type=F32)
            st = st_scr[h]
            for n in range(tc // ch):
                rs = slice(n * ch, (n + 1) * ch)
                o_scr[rs, cs] = o[rs] + lax.dot_general(qd[rs, cs], st.astype(BF16), _NT,
                                                        preferred_element_type=F32)
                upd = lax.dot_general(va[rs, cs], kd[rs, cs], _TN, preferred_element_type=F32)
                st = jnp.exp(btot[n * ch:n * ch + 1, cs]) * st + upd
            st_scr[h] = st

    nb = HG_BLOCK
    tri = jnp.where(lax.broadcasted_iota(I32, (nb, nb), 0) >= lax.broadcasted_iota(I32, (nb, nb), 1),
                    1.0, 0.0).astype(F32)
    trow = lax.broadcasted_iota(I32, (nb, 1), 0)

    def block(i, carry):
        r0 = pl.multiple_of(i * nb, nb)
        rows = pl.ds(r0, nb)
        for h in range(HG_HEADS):
            cs = slice(h * HG_DK, (h + 1) * HG_DK)
            bcum = jnp.dot(tri, lf_scr[rows, cs], preferred_element_type=F32,
                           precision=lax.Precision.HIGHEST)
            qb = q_scr[rows, cs]
            kb = k_scr[rows, cs]
            vb = v_scr[rows, cs]
            o = jnp.zeros((nb, HG_DV), F32)
            for s in range(nb):
                e = jnp.exp(jnp.where(trow >= s, bcum - bcum[s:s + 1, :], -jnp.inf))
                a_ts = jnp.sum(qb * kb[s:s + 1, :] * e, axis=1, keepdims=True)
                o = o + a_ts * vb[s:s + 1, :]
            st = st_scr[h]
            qd = (qb * jnp.exp(bcum)).astype(BF16)
            o = o + lax.dot_general(qd, st.astype(BF16), _NT, preferred_element_type=F32)
            blast = bcum[nb - 1:nb, :]
            kd = (kb * jnp.exp(blast - bcum)).astype(BF16)
            upd = lax.dot_general(vb.astype(BF16), kd, _TN, preferred_element_type=F32)
            st_scr[h] = jnp.exp(blast) * st + upd
            o_scr[rows, cs] = o
        return carry

    @pl.when(jnp.logical_not(factored))
    def _():
        lax.fori_loop(0, tc // nb, block, 0)

    for h in range(HG_HEADS):
        cs = slice(h * HG_DV, (h + 1) * HG_DV)
        o = o_scr[:, cs]
        o = o * lax.rsqrt(jnp.mean(o * o, axis=-1, keepdims=True) + LN_EPS)
        g = hg[:, cs]
        c_ref[0, :, cs] = (o * ng_ref[:, cs] * (g * _sigmoid(g))).astype(BF16)

    @pl.when(t == pl.num_programs(1) - 1)
    def _():
        for h in range(HG_HEADS):
            sout_ref[0, h] = st_scr[h].T


def _hgrn(x, w, loglb, log1mlb, omlb, ng, s0, *, tc):
    b, t, _ = x.shape
    xmap = lambda bi, j: (bi, j, 0)
    vec = pl.BlockSpec((1, HG_WIDTH), lambda bi, j: (0, 0))
    smap = lambda bi, j: (bi, 0, 0, 0)
    return pl.pallas_call(
        functools.partial(_hgrn_kernel, tc=tc),
        grid=(b, t // tc),
        in_specs=[pl.BlockSpec((1, tc, D_MODEL), xmap),
                  pl.BlockSpec((D_MODEL, 4 * HG_WIDTH), lambda bi, j: (0, 0)),
                  vec, vec, vec, vec,
                  pl.BlockSpec((1, HG_HEADS, HG_DK, HG_DV), smap)],
        out_specs=[pl.BlockSpec((1, tc, HG_WIDTH), xmap),
                   pl.BlockSpec((1, HG_HEADS, HG_DK, HG_DV), smap)],
        out_shape=[jax.ShapeDtypeStruct((b, t, HG_WIDTH), BF16),
                   jax.ShapeDtypeStruct((b, HG_HEADS, HG_DK, HG_DV), F32)],
        scratch_shapes=[pltpu.VMEM((tc, HG_WIDTH), F32)] * 5
                       + [pltpu.VMEM((HG_HEADS, HG_DV, HG_DK), F32)],
        compiler_params=_params(("parallel", "arbitrary")),
        name="hgrn",
    )(x, w, loglb, log1mlb, omlb, ng, s0)


def _merge_kernel(x_ref, a_ref, b_ref, c_ref, wg_ref, wa_ref, wb_ref, wc_ref, wo_ref,
                  g_ref, bt_ref, y_ref):
    x = x_ref[...]
    xb = x.astype(BF16)
    m = None
    for i, (br, w) in enumerate(((a_ref, wa_ref), (b_ref, wb_ref), (c_ref, wc_ref))):
        gate = _sigmoid(jnp.dot(xb, wg_ref[:, i * D_MODEL:(i + 1) * D_MODEL],
                                preferred_element_type=F32))
        term = gate * jnp.dot(br[...], w[...], preferred_element_type=F32)
        m = term if m is None else m + term
    mo = jnp.dot(m.astype(BF16), wo_ref[...], preferred_element_type=F32)
    y_ref[...] = _layer_norm(ALPHA * x + mo, g_ref[...], bt_ref[...])


def _merge(x, a, b, c, wg, wa, wb, wc, wo, ln_g, ln_b, *, tm):
    n = x.shape[0]
    row = lambda i: (i, 0)
    full = lambda i: (0, 0)
    br = pl.BlockSpec((tm, ATT_WIDTH), row)
    bw = pl.BlockSpec((ATT_WIDTH, D_MODEL), full)
    vec = pl.BlockSpec((1, D_MODEL), full)
    return pl.pallas_call(
        _merge_kernel,
        grid=(n // tm,),
        in_specs=[pl.BlockSpec((tm, D_MODEL), row), br, br, br,
                  pl.BlockSpec((D_MODEL, N_BRANCH * D_MODEL), full), bw, bw, bw,
                  pl.BlockSpec((D_MODEL, D_MODEL), full), vec, vec],
        out_specs=pl.BlockSpec((tm, D_MODEL), row),
        out_shape=jax.ShapeDtypeStruct((n, D_MODEL), F32),
        compiler_params=_params(("parallel",)),
        name="merge",
    )(x, a, b, c, wg, wa, wb, wc, wo, ln_g, ln_b)


def _route(logits):
    ex = jnp.exp(logits - jnp.max(logits, axis=0, keepdims=True))
    probs = ex / jnp.sum(ex, axis=0, keepdims=True)
    p = [[probs[g * EXP_PER_GROUP + k:g * EXP_PER_GROUP + k + 1, :] for k in range(EXP_PER_GROUP)]
         for g in range(N_GROUPS)]
    score = []
    for g in range(N_GROUPS):
        best = None
        for k1 in range(EXP_PER_GROUP):
            for k2 in range(k1 + 1, EXP_PER_GROUP):
                pair = p[g][k1] + p[g][k2]
                best = pair if best is None else jnp.maximum(best, pair)
        score.append(best)
    gsel = jnp.zeros(score[0].shape, I32)
    top = score[0]
    for g in range(1, N_GROUPS):
        better = score[g] > top
        top = jnp.where(better, score[g], top)
        gsel = jnp.where(better, g, gsel)
    val = []
    for k in range(EXP_PER_GROUP):
        v = p[0][k]
        for g in range(1, N_GROUPS):
            v = jnp.where(gsel == g, p[g][k], v)
        val.append(v)
    v1, i1 = val[0], jnp.zeros(gsel.shape, I32)
    for k in range(1, EXP_PER_GROUP):
        better = val[k] > v1
        v1 = jnp.where(better, val[k], v1)
        i1 = jnp.where(better, k, i1)
    v2, i2 = jnp.full(v1.shape, -1.0, F32), jnp.zeros(gsel.shape, I32)
    for k in range(EXP_PER_GROUP):
        better = (i1 != k) & (val[k] > v2)
        v2 = jnp.where(better, val[k], v2)
        i2 = jnp.where(better, k, i2)
    den = v1 + v2
    w1, w2 = v1 / den, v2 / den
    out = []
    for g in range(N_GROUPS):
        rows = [jnp.where(gsel == g, jnp.where(i1 == k, w1, jnp.where(i2 == k, w2, 0.0)), 0.0)
                for k in range(EXP_PER_GROUP)]
        out.append(jnp.concatenate(rows, axis=0))
    return out


def _moe_kernel(x_ref, wr_ref, br_ref, wg_ref, wu_ref, wd_ref, g_ref, bt_ref, y_ref,
                xb_scr, gate_scr, acc_scr, *, tm):
    grp = pl.program_id(1)

    @pl.when(grp == 0)
    def _():
        xb = x_ref[...].astype(BF16)
        xb_scr[...] = xb
        logits = lax.dot_general(wr_ref[...], xb, _NT, preferred_element_type=F32) + br_ref[...]
        gates = _route(logits)
        pad = jnp.zeros((LANES - EXP_PER_GROUP, tm), F32)
        for g in range(N_GROUPS):
            gate_scr[g] = jnp.concatenate([gates[g], pad], axis=0).T
        acc_scr[...] = jnp.zeros(acc_scr.shape, F32)

    xb = xb_scr[...]
    gate = gate_scr[grp]
    acc = acc_scr[...]
    for k in range(EXP_PER_GROUP):
        h = jnp.dot(xb, wg_ref[k], preferred_element_type=F32)
        up = jnp.dot(xb, wu_ref[k], preferred_element_type=F32)
        act = h * _sigmoid(h) * up * gate[:, k:k + 1]
        acc = acc + jnp.dot(act.astype(BF16), wd_ref[k], preferred_element_type=F32)
    acc_scr[...] = acc

    @pl.when(grp == N_GROUPS - 1)
    def _():
        y_ref[...] = _layer_norm(ALPHA * x_ref[...] + acc, g_ref[...], bt_ref[...])


def _moe(x, wr_t, br, wg, wu, wd, ln_g, ln_b, *, tm):
    n = x.shape[0]
    row = lambda i, g: (i, 0)
    full = lambda i, g: (0, 0)
    vec = pl.BlockSpec((1, D_MODEL), full)
    return pl.pallas_call(
        functools.partial(_moe_kernel, tm=tm),
        grid=(n // tm, N_GROUPS),
        in_specs=[pl.BlockSpec((tm, D_MODEL), row),
                  pl.BlockSpec((N_EXPERTS, D_MODEL), full),
                  pl.BlockSpec((N_EXPERTS, 1), full),
                  pl.BlockSpec((EXP_PER_GROUP, D_MODEL, D_FF), lambda i, g: (g, 0, 0)),
                  pl.BlockSpec((EXP_PER_GROUP, D_MODEL, D_FF), lambda i, g: (g, 0, 0)),
                  pl.BlockSpec((EXP_PER_GROUP, D_FF, D_MODEL), lambda i, g: (g, 0, 0)),
                  vec, vec],
        out_specs=pl.BlockSpec((tm, D_MODEL), row),
        out_shape=jax.ShapeDtypeStruct((n, D_MODEL), F32),
        scratch_shapes=[pltpu.VMEM((tm, D_MODEL), BF16),
                        pltpu.VMEM((N_GROUPS, tm, LANES), F32),
                        pltpu.VMEM((tm, D_MODEL), F32)],
        compiler_params=_params(("parallel", "arbitrary")),
        name="moe",
    )(x, wr_t, br, wg, wu, wd, ln_g, ln_b)


WPREP_ROWS = 128


def _split_w_in_kernel(w_ref, wn_ref, wt_ref, wu_ref, wv_ref, whg_ref, wg_ref):
    w = w_ref[0]
    offs = [0]
    for n in SPLIT_SIZES:
        offs.append(offs[-1] + n)
    seg = lambda i, j=None: w[:, offs[i]:offs[i + 1 if j is None else j]]
    zeros = lambda c: jnp.zeros((w.shape[0], c), w.dtype)
    wn_ref[0] = jnp.concatenate([seg(1), seg(2), seg(4), zeros(LANES - IDX_DIM)], axis=1).astype(BF16)
    wt = jnp.concatenate([seg(0), seg(3), seg(2), seg(5), zeros(ATT_T_PAD - ATT_T_ROWS + 8 - IDX_HEADS)],
                         axis=1)
    for c in range(ATT_T_PAD // LANES):
        rows = min(LANES, ATT_T_ROWS - c * LANES)
        wt_ref[0, c * LANES:c * LANES + rows, :] = wt[:, c * LANES:(c + 1) * LANES].T[:rows].astype(BF16)
    wu_ref[0] = seg(6).astype(BF16)
    wv_ref[0] = seg(7).astype(BF16)
    whg_ref[0] = seg(8, 12).astype(BF16)
    wg_ref[0] = seg(12).astype(BF16)


def _split_w_in(w_in):
    depth, d, ncol = w_in.shape
    tr = WPREP_ROWS
    cols = (ATT_N_COLS, None, SG_WIDTH, SG_WIDTH, 4 * HG_WIDTH, N_BRANCH * D_MODEL)
    shapes = [(depth, ATT_T_ROWS, d) if c is None else (depth, d, c) for c in cols]
    specs = [pl.BlockSpec((1, ATT_T_ROWS, tr), lambda l, i: (l, 0, i)) if c is None
             else pl.BlockSpec((1, tr, c), lambda l, i: (l, i, 0)) for c in cols]
    return pl.pallas_call(
        _split_w_in_kernel,
        grid=(depth, d // tr),
        in_specs=[pl.BlockSpec((1, tr, ncol), lambda l, i: (l, i, 0))],
        out_specs=specs,
        out_shape=[jax.ShapeDtypeStruct(sh, BF16) for sh in shapes],
        compiler_params=_params(("parallel", "parallel")),
        name="split_w_in",
    )(w_in)


def _row_tile(n, want):
    while n % want:
        want //= 2
    return want


def _layer(x, bsz, t, lw, *, pos0, topk, cache, s0):
    n = bsz * t
    lc = min(t, SG_LEN)
    k, v, ki, kb, kib, qt, qit, vt, wit = _attn_proj(x, lw["w_att_n"], lw["w_att_t"], bsz, t,
                                                     _row_tile(t, 512))
    kb3 = kb.reshape(bsz, t, KV_WIDTH)
    kib3 = kib.reshape(bsz, t, IDX_DIM)
    if cache is not None:
        ck, cvt, cki = cache
        kb3 = jnp.concatenate([ck, kb3], axis=1)
        kib3 = jnp.concatenate([cki, kib3], axis=1)
        vt = jnp.concatenate([cvt, vt], axis=2)
    ltot = kb3.shape[1]
    lp = -(-ltot // DSA_KEY_BLOCK) * DSA_KEY_BLOCK
    tq = LANES
    tp = -(-t // tq) * tq
    padq = lambda a: jnp.pad(a, ((0, 0), (0, 0), (0, tp - t)))
    a = _dsa(padq(qt), padq(qit), padq(wit), jnp.pad(kb3, ((0, 0), (0, lp - ltot), (0, 0))),
             jnp.pad(vt, ((0, 0), (0, 0), (0, lp - ltot))),
             jnp.pad(kib3, ((0, 0), (0, lp - ltot), (0, 0))), tq=tq, pos0=pos0, topk=topk)[:, :t]
    b, v_gm = _gmlp(x, lw["w_u"], lw["w_v"], lw["ln_sg_g"], lw["ln_sg_b"],
                    lw["w_sg"][:, :lc, :lc], lw["b_sg"][:, :lc].T, tm=_row_tile(n, 256), lc=lc)
    c, s_new = _hgrn(x.reshape(bsz, t, D_MODEL), lw["w_hg"], lw["log_lb"], lw["log1m_lb"],
                     lw["om_lb"], lw["hg_norm_g"], s0, tc=min(t, 256))
    x1 = _merge(x, a.reshape(n, ATT_WIDTH), b, c.reshape(n, HG_WIDTH), lw["w_gates"],
                lw["w_branch_a"], lw["w_branch_b"], lw["w_branch_c"], lw["w_out"],
                lw["ln1_g"], lw["ln1_b"], tm=_row_tile(n, 256))
    x2 = _moe(x1, lw["w_router_t"], lw["b_router"], lw["w_exp_gate"], lw["w_exp_up"],
              lw["w_exp_down"], lw["ln2_g"], lw["ln2_b"], tm=_row_tile(n, 512))
    return x2, k, v, ki, s_new, v_gm


def kernel(x_prompt, x_sample, cache_k, cache_v, cache_kidx, state_hgrn, w_in, w_sg, b_sg, ln_sg_g, ln_sg_b, hg_lb_logits, hg_norm_g, w_branch_a, w_branch_b, w_branch_c, w_out, ln1_g, ln1_b, w_router, b_router, w_exp_gate, w_exp_up, w_exp_down, ln2_g, ln2_b):
    bp, sp, _ = x_prompt.shape
    bs, ss, _ = x_sample.shape
    past = cache_k.shape[2]
    topk_p = min(TOPK_MAX, sp // 4)
    topk_s = min(TOPK_MAX, (past + ss) // 4)

    lb_all = jnp.cumsum(jax.nn.softmax(hg_lb_logits.astype(F32), axis=0), axis=0)
    lb_all = lb_all - lb_all[0:1]
    vec = lambda a: a.reshape(1, -1)

    xp = x_prompt.reshape(bp * sp, D_MODEL)
    xs = x_sample.reshape(bs * ss, D_MODEL)
    s0_p = jnp.zeros((bp, HG_HEADS, HG_DK, HG_DV), F32)
    outs_p, outs_s = [], []
    w_att_n, w_att_t, w_u, w_v, w_hg, w_gates = _split_w_in(w_in)
    for l in range(DEPTH):
        lw = dict(
            w_att_n=w_att_n[l], w_att_t=w_att_t[l], w_u=w_u[l], w_v=w_v[l], w_hg=w_hg[l],
            w_gates=w_gates[l],
            ln_sg_g=vec(ln_sg_g[l]), ln_sg_b=vec(ln_sg_b[l]), w_sg=w_sg[l], b_sg=b_sg[l],
            log_lb=vec(jnp.log(lb_all[l])), log1m_lb=vec(jnp.log1p(-lb_all[l])),
            om_lb=vec(1.0 - lb_all[l]), hg_norm_g=vec(hg_norm_g[l].astype(F32)),
            w_branch_a=w_branch_a[l].astype(BF16), w_branch_b=w_branch_b[l].astype(BF16),
            w_branch_c=w_branch_c[l].astype(BF16), w_out=w_out[l].astype(BF16),
            ln1_g=vec(ln1_g[l]), ln1_b=vec(ln1_b[l]),
            w_router_t=w_router.T.astype(BF16), b_router=b_router.astype(F32).reshape(-1, 1),
            w_exp_gate=w_exp_gate[l].astype(BF16), w_exp_up=w_exp_up[l].astype(BF16),
            w_exp_down=w_exp_down[l].astype(BF16),
            ln2_g=vec(ln2_g[l]), ln2_b=vec(ln2_b[l]))
        xp, k, v, ki, s_new, _ = _layer(xp, bp, sp, lw, pos0=0, topk=topk_p, cache=None, s0=s0_p)
        outs_p.append((k.reshape(bp, sp, KV_HEADS, HEAD_DIM), v.reshape(bp, sp, KV_HEADS, HEAD_DIM),
                       ki.reshape(bp, sp, IDX_DIM), s_new))
        cache = (cache_k[l].reshape(bs, past, KV_WIDTH).astype(BF16),
                 jnp.swapaxes(cache_v[l].reshape(bs, past, KV_WIDTH), 1, 2).astype(BF16),
                 cache_kidx[l].astype(BF16))
        xs, k, v, ki, s_new, v_gm = _layer(xs, bs, ss, lw, pos0=past, topk=topk_s, cache=cache,
                                           s0=state_hgrn[l].astype(F32))
        outs_s.append((k.reshape(bs, ss, KV_HEADS, HEAD_DIM), v.reshape(bs, ss, KV_HEADS, HEAD_DIM),
                       ki.reshape(bs, ss, IDX_DIM), s_new, v_gm.reshape(bs, ss, SG_WIDTH)))

    stack = lambda rows, i: jnp.stack([r[i] for r in rows])
    return (xp.reshape(bp, sp, D_MODEL), xs.reshape(bs, ss, D_MODEL),
            stack(outs_p, 0), stack(outs_p, 1), stack(outs_p, 2), stack(outs_p, 3),
            stack(outs_s, 0), stack(outs_s, 1), stack(outs_s, 2), stack(outs_s, 3),
            stack(outs_s, 4))
```

```python
import functools

import jax
import jax.numpy as jnp
from jax import lax
from jax.experimental import pallas as pl
from jax.experimental.pallas import tpu as pltpu

F32 = jnp.float32
BF16 = jnp.bfloat16
I32 = jnp.int32

D_MODEL = 1024
DEPTH = 4
CHUNK = 64
ATT_HEADS = 8
KV_HEADS = 2
HEAD_DIM = 64
ATT_WIDTH = ATT_HEADS * HEAD_DIM
KV_WIDTH = KV_HEADS * HEAD_DIM
IDX_HEADS = 4
IDX_DIM = 64
TOPK_MAX = 256
SG_LEN = 128
SG_GROUPS = 4
SG_WIDTH = 512
SG_GW = SG_WIDTH // SG_GROUPS
HG_HEADS = 4
HG_DK = 128
HG_DV = 128
HG_WIDTH = HG_HEADS * HG_DV
HG_BLOCK = 16
HG_FACTOR_RANGE = 80.0
N_BRANCH = 3
N_EXPERTS = 16
N_GROUPS = 4
EXP_PER_GROUP = 4
D_FF = 256
ALPHA = (2 * DEPTH) ** 0.25
LN_EPS = 1e-5

SPLIT_SIZES = (ATT_WIDTH, KV_WIDTH, KV_WIDTH, IDX_HEADS * IDX_DIM, IDX_DIM, IDX_HEADS,
               SG_WIDTH, SG_WIDTH, HG_HEADS * HG_DK, HG_HEADS * HG_DK, HG_WIDTH, HG_WIDTH,
               N_BRANCH * D_MODEL)

LANES = 128
INT_MIN = -2 ** 31
NEG_BIG = -1e30
LOG2_E = 1.4426950408889634
DSA_KEY_BLOCK = 512
VMEM_LIMIT = 56 * 1024 * 1024

_NT = (((1,), (1,)), ((), ()))
_TN = (((0,), (0,)), ((), ()))


def _params(sem):
    return pltpu.CompilerParams(dimension_semantics=sem, vmem_limit_bytes=VMEM_LIMIT)


def _layer_norm(x, g, b):
    mu = jnp.mean(x, axis=-1, keepdims=True)
    d = x - mu
    var = jnp.mean(d * d, axis=-1, keepdims=True)
    return d * lax.rsqrt(var + LN_EPS) * g + b


def _gelu_tanh(x):
    return 0.5 * x * (1.0 + jnp.tanh(0.7978845608028654 * (x + 0.044715 * (x * x * x))))


def _sigmoid(x):
    return 1.0 / (1.0 + jnp.exp(-x))


def _fold_rows(x, op):
    parts = [x[i:i + 8] for i in range(0, x.shape[0], 8)]
    while len(parts) > 1:
        parts = [op(parts[i], parts[i + 1]) for i in range(0, len(parts), 2)]
    return parts[0]


ATT_T_ROWS = ATT_WIDTH + IDX_HEADS * IDX_DIM + KV_WIDTH + 8
ATT_T_PAD = -(-ATT_T_ROWS // LANES) * LANES
ATT_N_COLS = 3 * LANES


def _attn_proj_kernel(x_ref, wn_ref, wt_ref, k_ref, v_ref, ki_ref, kb_ref, kib_ref,
                      qt_ref, qit_ref, vt_ref, wit_ref):
    xb = x_ref[...].astype(BF16)
    z = jnp.dot(xb, wn_ref[...], preferred_element_type=F32)
    k = z[:, 0:KV_WIDTH]
    ki = z[:, 2 * KV_WIDTH:2 * KV_WIDTH + IDX_DIM]
    k_ref[...] = k
    v_ref[...] = z[:, KV_WIDTH:2 * KV_WIDTH]
    ki_ref[...] = ki
    kb_ref[...] = k.astype(BF16)
    kib_ref[...] = ki.astype(BF16)
    zt = lax.dot_general(wt_ref[...], xb, _NT, preferred_element_type=F32)
    o = 0
    qt_ref[0] = (zt[o:o + ATT_WIDTH] * (HEAD_DIM ** -0.5 * LOG2_E)).astype(BF16)
    o += ATT_WIDTH
    qit_ref[0] = (zt[o:o + IDX_HEADS * IDX_DIM] * IDX_DIM ** -0.5).astype(BF16)
    o += IDX_HEADS * IDX_DIM
    vt_ref[0] = zt[o:o + KV_WIDTH].astype(BF16)
    o += KV_WIDTH
    wit_ref[0] = zt[o:o + 8] * IDX_HEADS ** -0.5


def _attn_proj(x, wn, wt, bsz, t, tm):
    n = bsz * t
    per = t // tm
    row = lambda i: (i, 0)
    tmap = lambda i: (i // per, 0, i % per)
    nat = ((KV_WIDTH, F32), (KV_WIDTH, F32), (IDX_DIM, F32), (KV_WIDTH, BF16), (IDX_DIM, BF16))
    tr = ((ATT_WIDTH, BF16), (IDX_HEADS * IDX_DIM, BF16), (KV_WIDTH, BF16), (8, F32))
    return pl.pallas_call(
        _attn_proj_kernel,
        grid=(n // tm,),
        in_specs=[pl.BlockSpec((tm, D_MODEL), row),
                  pl.BlockSpec((D_MODEL, ATT_N_COLS), lambda i: (0, 0)),
                  pl.BlockSpec((ATT_T_ROWS, D_MODEL), lambda i: (0, 0))],
        out_specs=[pl.BlockSpec((tm, c), row) for c, _ in nat]
                  + [pl.BlockSpec((1, c, tm), tmap) for c, _ in tr],
        out_shape=[jax.ShapeDtypeStruct((n, c), d) for c, d in nat]
                  + [jax.ShapeDtypeStruct((bsz, c, t), d) for c, d in tr],
        compiler_params=_params(("parallel",)),
        name="attn_proj",
    )(x, wn, wt)


def _dsa_kernel(qt_ref, qit_ref, wit_ref, kb_ref, vt_ref, kib_ref, o_ref,
                key_scr, hi_scr, lo_scr, j_scr, s_scr, acc_scr, m_scr, a_scr,
                *, tq, kblk, pos0, topk, idx_bits):
    j = pl.program_id(1)
    qpos0 = pos0 + j * tq
    qcol = lax.broadcasted_iota(I32, (1, tq), 1)
    chunk_end = ((qpos0 + qcol) // CHUNK + 1) * CHUNK
    lvis = ((qpos0 + tq - 1) // CHUNK + 1) * CHUNK
    nblk = (lvis + kblk - 1) // kblk
    krow = lax.broadcasted_iota(I32, (kblk, tq), 0)

    wi = wit_ref[0]
    qit = qit_ref[0]

    def score_block(i, carry):
        off = pl.multiple_of(i * kblk, kblk)
        kib = kib_ref[0, pl.ds(off, kblk), :]
        sc = jnp.zeros((kblk, tq), F32)
        for h in range(IDX_HEADS):
            raw = jnp.dot(kib, qit[h * IDX_DIM:(h + 1) * IDX_DIM, :], preferred_element_type=F32)
            sc = sc + jnp.maximum(raw, 0.0) * wi[h:h + 1, :]
        bits = pltpu.bitcast(sc, I32)
        key = bits ^ ((bits >> 31) & 0x7FFFFFFF)
        key = jnp.where(off + krow < chunk_end, key, INT_MIN)
        key_scr[i] = key
        hi_scr[i] = (key >> 16).astype(jnp.int16)
        lo_scr[i] = ((key & 0xFFFF) - 32768).astype(jnp.int16)
        return carry

    lax.fori_loop(0, nblk, score_block, 0)

    def count(hits):
        def body(i, acc):
            return acc + _fold_rows(hits(i, key_scr[i]), jnp.add)
        acc = lax.fori_loop(0, nblk, body, jnp.zeros((8, tq), F32))
        return jnp.sum(acc, axis=0, keepdims=True)

    i16 = jnp.int16

    def count16(src, pred):
        def body(i, acc):
            hit = jnp.where(pred(src[i]), i16(1), i16(0))
            parts = [hit[r:r + 16] for r in range(0, kblk, 16)]
            while len(parts) > 1:
                parts = [parts[r] + parts[r + 1] for r in range(0, len(parts), 2)]
            return acc + parts[0]
        acc = lax.fori_loop(0, nblk, body, jnp.zeros((16, tq), i16))
        return jnp.sum(acc.astype(F32), axis=0, keepdims=True)

    def half_search(src, above):
        def bit(i, c):
            t, n_t = c
            cand = t + jnp.left_shift(jnp.int32(1), 15 - i)
            c16 = cand.astype(i16)
            cnt = above + count16(src, lambda x: x >= c16)
            ok = cnt >= topk
            return jnp.where(ok, cand, t), jnp.where(ok, cnt, n_t)

        n_all = (nblk * kblk).astype(F32) + jnp.zeros((1, tq), F32)
        return lax.fori_loop(0, 16, bit, (jnp.full((1, tq), -32768, I32), n_all))

    def key_bit(i, thr):
        cand = thr + jnp.left_shift(jnp.int32(1), 31 - i)
        cnt = count(lambda _, kb: jnp.where(kb >= cand, 1.0, 0.0))
        return jnp.where(cnt >= topk, cand, thr)

    thr = lax.fori_loop(0, 32, key_bit, jnp.full((1, tq), INT_MIN, I32))
    n_ge = count(lambda _, kb: jnp.where(kb >= thr, 1.0, 0.0))

    j_scr[...] = jnp.full((1, tq), 2 ** 30, I32)

    @pl.when(jnp.max(n_ge) > topk)
    def _():
        need = topk - count(lambda _, kb: jnp.where(kb > thr, 1.0, 0.0))

        def idx_bit(i, jmax):
            cand = jmax + jnp.left_shift(jnp.int32(1), idx_bits - 1 - i)
            cnt = count(lambda b, kb: jnp.where(
                kb == thr, jnp.where(b * kblk + krow < cand, 1.0, 0.0), 0.0))
            return jnp.where(cnt < need, cand, jmax)

        j_scr[...] = lax.fori_loop(0, idx_bits, idx_bit, jnp.zeros((1, tq), I32))

    jmax = j_scr[...]

    m_scr[...] = jnp.full(m_scr.shape, NEG_BIG, F32)
    acc_scr[...] = jnp.zeros(acc_scr.shape, F32)
    rep = ATT_HEADS // KV_HEADS
    qt = qt_ref[0]
    qg = [jnp.concatenate([qt[(g * rep + r) * HEAD_DIM:(g * rep + r + 1) * HEAD_DIM, :]
                           for r in range(rep)], axis=1) for g in range(KV_HEADS)]
    ones = jnp.ones((16, kblk), BF16)

    def score_phase(i, buf):
        off = pl.multiple_of(i * kblk, kblk)
        kb = key_scr[i]
        kpos = off + krow
        tie = jnp.where(kpos <= jmax, 0.0, NEG_BIG)
        bias = jnp.where(kb > thr, 0.0, jnp.where(kb == thr, tie, NEG_BIG))
        bias = jnp.where(kpos < chunk_end, bias, NEG_BIG)
        bias = jnp.concatenate([bias] * rep, axis=1)
        kk = kb_ref[0, pl.ds(off, kblk), :]
        for g in range(KV_HEADS):
            s = jnp.dot(kk[:, g * HEAD_DIM:(g + 1) * HEAD_DIM], qg[g],
                        preferred_element_type=F32) + bias
            s_scr[buf, g] = s
            m_blk = jnp.max(_fold_rows(s, jnp.maximum), axis=0, keepdims=True)
            m_old = m_scr[g]
            m_new = jnp.maximum(m_old, m_blk)
            a_scr[g] = jnp.exp2(m_old - m_new)
            m_scr[g] = m_new

    def value_phase(i, buf):
        off = pl.multiple_of(i * kblk, kblk)
        vv = vt_ref[0, :, pl.ds(off, kblk)]
        for g in range(KV_HEADS):
            p = jnp.exp2(s_scr[buf, g] - m_scr[g]).astype(BF16)
            vg = jnp.concatenate([vv[g * HEAD_DIM:(g + 1) * HEAD_DIM, :], ones], axis=0)
            acc_scr[g] = a_scr[g] * acc_scr[g] + jnp.dot(vg, p, preferred_element_type=F32)

    score_phase(0, 0)

    def attn_pair(k, carry):
        i = 2 * k
        value_phase(i, 0)
        score_phase(i + 1, 1)
        value_phase(i + 1, 1)
        score_phase(jnp.minimum(i + 2, nblk - 1), 0)
        return carry

    lax.fori_loop(0, nblk // 2, attn_pair, 0)

    @pl.when(nblk % 2 == 1)
    def _():
        value_phase(nblk - 1, 0)

    heads = []
    for g in range(KV_HEADS):
        acc = acc_scr[g]
        og = acc[:HEAD_DIM] / acc[HEAD_DIM:HEAD_DIM + 1]
        heads += [og[:, r * tq:(r + 1) * tq] for r in range(rep)]
    o_ref[0] = jnp.concatenate(heads, axis=0).T.astype(BF16)


def _dsa(qt, qit, wit, kb, vt, kib, *, tq, pos0, topk):
    b, _, t = qt.shape
    lp = kb.shape[1]
    kblk = DSA_KEY_BLOCK
    assert lp % kblk == 0 and t % tq == 0 and tq % LANES == 0
    kern = functools.partial(_dsa_kernel, tq=tq, kblk=kblk, pos0=pos0, topk=topk,
                             idx_bits=(lp - 1).bit_length())
    rep = ATT_HEADS // KV_HEADS
    qmap = lambda bi, j: (bi, 0, j)
    kmap = lambda bi, j: (bi, 0, 0)
    return pl.pallas_call(
        kern,
        grid=(b, t // tq),
        in_specs=[pl.BlockSpec((1, ATT_WIDTH, tq), qmap),
                  pl.BlockSpec((1, IDX_HEADS * IDX_DIM, tq), qmap),
                  pl.BlockSpec((1, 8, tq), qmap),
                  pl.BlockSpec((1, lp, KV_WIDTH), kmap),
                  pl.BlockSpec((1, KV_WIDTH, lp), kmap),
                  pl.BlockSpec((1, lp, IDX_DIM), kmap)],
        out_specs=pl.BlockSpec((1, tq, ATT_WIDTH), lambda bi, j: (bi, j, 0)),
        out_shape=jax.ShapeDtypeStruct((b, t, ATT_WIDTH), BF16),
        scratch_shapes=[pltpu.VMEM((lp // kblk, kblk, tq), I32),
                        pltpu.VMEM((lp // kblk, kblk, tq), jnp.int16),
                        pltpu.VMEM((lp // kblk, kblk, tq), jnp.int16),
                        pltpu.VMEM((1, tq), I32),
                        pltpu.VMEM((2, KV_HEADS, kblk, rep * tq), F32),
                        pltpu.VMEM((KV_HEADS, HEAD_DIM + 16, rep * tq), F32),
                        pltpu.VMEM((KV_HEADS, 1, rep * tq), F32),
                        pltpu.VMEM((KV_HEADS, 1, rep * tq), F32)],
        compiler_params=_params(("parallel", "arbitrary")),
        name="dsa",
    )(qt, qit, wit, kb, vt, kib)


def _gmlp_kernel(x_ref, wu_ref, wv_ref, g_ref, b_ref, ws_ref, bst_ref, o_ref, v_ref, *, tm, lc):
    xb = x_ref[...].astype(BF16)
    u = _gelu_tanh(jnp.dot(xb, wu_ref[...], preferred_element_type=F32))
    v = _layer_norm(_gelu_tanh(jnp.dot(xb, wv_ref[...], preferred_element_type=F32)),
                    g_ref[...], b_ref[...])
    v_ref[...] = v
    vb = v.astype(BF16)
    r = lax.broadcasted_iota(I32, (lc, lc), 0)
    c = lax.broadcasted_iota(I32, (lc, lc), 1)
    for g in range(SG_GROUPS):
        w = jnp.where(r >= c, ws_ref[g], 0.0).astype(BF16)
        bias = bst_ref[:, g:g + 1]
        cs = slice(g * SG_GW, (g + 1) * SG_GW)
        for n in range(tm // lc):
            rs = slice(n * lc, (n + 1) * lc)
            mixed = jnp.dot(w, vb[rs, cs], preferred_element_type=F32) + bias
            o_ref[rs, cs] = (u[rs, cs] * mixed).astype(BF16)


def _gmlp(x, wu, wv, ln_g, ln_b, ws, bst, *, tm, lc):
    n = x.shape[0]
    row = lambda i: (i, 0)
    full2 = lambda i: (0, 0)
    return pl.pallas_call(
        functools.partial(_gmlp_kernel, tm=tm, lc=lc),
        grid=(n // tm,),
        in_specs=[pl.BlockSpec((tm, D_MODEL), row),
                  pl.BlockSpec((D_MODEL, SG_WIDTH), full2),
                  pl.BlockSpec((D_MODEL, SG_WIDTH), full2),
                  pl.BlockSpec((1, SG_WIDTH), full2),
                  pl.BlockSpec((1, SG_WIDTH), full2),
                  pl.BlockSpec((SG_GROUPS, lc, lc), lambda i: (0, 0, 0)),
                  pl.BlockSpec((lc, SG_GROUPS), full2)],
        out_specs=[pl.BlockSpec((tm, SG_WIDTH), row), pl.BlockSpec((tm, SG_WIDTH), row)],
        out_shape=[jax.ShapeDtypeStruct((n, SG_WIDTH), BF16),
                   jax.ShapeDtypeStruct((n, SG_WIDTH), F32)],
        compiler_params=_params(("parallel",)),
        name="gmlp",
    )(x, wu, wv, ln_g, ln_b, ws, bst)


def _hgrn_kernel(x_ref, w_ref, loglb_ref, log1mlb_ref, omlb_ref, ng_ref, s0_ref, c_ref, sout_ref,
                 q_scr, k_scr, v_scr, lf_scr, o_scr, st_scr, *, tc):
    t = pl.program_id(1)

    @pl.when(t == 0)
    def _():
        for h in range(HG_HEADS):
            st_scr[h] = s0_ref[0, h].T

    xb = x_ref[0].astype(BF16)
    z = jnp.dot(xb, w_ref[...], preferred_element_type=F32)
    hq = z[:, 0:HG_WIDTH]
    hf = z[:, HG_WIDTH:2 * HG_WIDTH]
    hg = z[:, 3 * HG_WIDTH:4 * HG_WIDTH]
    q_scr[...] = hq * _sigmoid(hq) * HG_DK ** -0.5
    log_sig = jnp.minimum(hf, 0.0) - jnp.log1p(jnp.exp(-jnp.abs(hf)))
    y = log1mlb_ref[...] + log_sig
    a = loglb_ref[...]
    lf_scr[...] = jnp.maximum(a, y) + jnp.log1p(jnp.exp(-jnp.abs(a - y)))
    k_scr[...] = omlb_ref[...] * _sigmoid(-hf)
    v_scr[...] = z[:, 2 * HG_WIDTH:3 * HG_WIDTH]

    ch = min(tc, CHUNK)
    ri = lax.broadcasted_iota(I32, (tc, tc), 0)
    ci = lax.broadcasted_iota(I32, (tc, tc), 1)
    same = (ri // ch) == (ci // ch)
    causal = same & (ri >= ci)
    lf = lf_scr[...]
    bcum_all = jnp.dot(jnp.where(causal, 1.0, 0.0), lf, preferred_element_type=F32,
                       precision=lax.Precision.HIGHEST)
    factored = jnp.min(bcum_all) > -HG_FACTOR_RANGE

    @pl.when(factored)
    def _():
        btot = jnp.dot(jnp.where(same, 1.0, 0.0), lf, preferred_element_type=F32,
                       precision=lax.Precision.HIGHEST)
        qa = q_scr[...]
        ka = k_scr[...]
        qd = (qa * jnp.exp(bcum_all)).astype(BF16)
        ki = (ka * jnp.exp(-bcum_all)).astype(BF16)
        kd = (ka * jnp.exp(btot - bcum_all)).astype(BF16)
        va = v_scr[...].astype(BF16)
        for h in range(HG_HEADS):
            cs = slice(h * HG_DK, (h + 1) * HG_DK)
            att = lax.dot_general(qd[:, cs], ki[:, cs], _NT, preferred_element_type=F32)
            att = jnp.where(causal, att, 0.0).astype(BF16)
            o = jnp.dot(att, va[:, cs], preferred_element_type=F32)
            st = st_scr[h]
            for n in range(tc // ch):
                rs = slice(n * ch, (n + 1) * ch)
                o_scr[rs, cs] = o[rs] + lax.dot_general(qd[rs, cs], st.astype(BF16), _NT,
                                                        preferred_element_type=F32)
                upd = lax.dot_general(va[rs, cs], kd[rs, cs], _TN, preferred_element_type=F32)
                st = jnp.exp(btot[n * ch:n * ch + 1, cs]) * st + upd
            st_scr[h] = st

    nb = HG_BLOCK
    tri = jnp.where(lax.broadcasted_iota(I32, (nb, nb), 0) >= lax.broadcasted_iota(I32, (nb, nb), 1),
                    1.0, 0.0).astype(F32)
    trow = lax.broadcasted_iota(I32, (nb, 1), 0)

    def block(i, carry):
        r0 = pl.multiple_of(i * nb, nb)
        rows = pl.ds(r0, nb)
        for h in range(HG_HEADS):
            cs = slice(h * HG_DK, (h + 1) * HG_DK)
            bcum = jnp.dot(tri, lf_scr[rows, cs], preferred_element_type=F32,
                           precision=lax.Precision.HIGHEST)
            qb = q_scr[rows, cs]
            kb = k_scr[rows, cs]
            vb = v_scr[rows, cs]
            o = jnp.zeros((nb, HG_DV), F32)
            for s in range(nb):
                e = jnp.exp(jnp.where(trow >= s, bcum - bcum[s:s + 1, :], -jnp.inf))
                a_ts = jnp.sum(qb * kb[s:s + 1, :] * e, axis=1, keepdims=True)
                o = o + a_ts * vb[s:s + 1, :]
            st = st_scr[h]
            qd = (qb * jnp.exp(bcum)).astype(BF16)
            o = o + lax.dot_general(qd, st.astype(BF16), _NT, preferred_element_type=F32)
            blast = bcum[nb - 1:nb, :]
            kd = (kb * jnp.exp(blast - bcum)).astype(BF16)
            upd = lax.dot_general(vb.astype(BF16), kd, _TN, preferred_element_type=F32)
            st_scr[h] = jnp.exp(blast) * st + upd
            o_scr[rows, cs] = o
        return carry

    @pl.when(jnp.logical_not(factored))
    def _():
        lax.fori_loop(0, tc // nb, block, 0)

    for h in range(HG_HEADS):
        cs = slice(h * HG_DV, (h + 1) * HG_DV)
        o = o_scr[:, cs]
        o = o * lax.rsqrt(jnp.mean(o * o, axis=-1, keepdims=True) + LN_EPS)
        g = hg[:, cs]
        c_ref[0, :, cs] = (o * ng_ref[:, cs] * (g * _sigmoid(g))).astype(BF16)

    @pl.when(t == pl.num_programs(1) - 1)
    def _():
        for h in range(HG_HEADS):
            sout_ref[0, h] = st_scr[h].T


def _hgrn(x, w, loglb, log1mlb, omlb, ng, s0, *, tc):
    b, t, _ = x.shape
    xmap = lambda bi, j: (bi, j, 0)
    vec = pl.BlockSpec((1, HG_WIDTH), lambda bi, j: (0, 0))
    smap = lambda bi, j: (bi, 0, 0, 0)
    return pl.pallas_call(
        functools.partial(_hgrn_kernel, tc=tc),
        grid=(b, t // tc),
        in_specs=[pl.BlockSpec((1, tc, D_MODEL), xmap),
                  pl.BlockSpec((D_MODEL, 4 * HG_WIDTH), lambda bi, j: (0, 0)),
                  vec, vec, vec, vec,
                  pl.BlockSpec((1, HG_HEADS, HG_DK, HG_DV), smap)],
        out_specs=[pl.BlockSpec((1, tc, HG_WIDTH), xmap),
                   pl.BlockSpec((1, HG_HEADS, HG_DK, HG_DV), smap)],
        out_shape=[jax.ShapeDtypeStruct((b, t, HG_WIDTH), BF16),
                   jax.ShapeDtypeStruct((b, HG_HEADS, HG_DK, HG_DV), F32)],
        scratch_shapes=[pltpu.VMEM((tc, HG_WIDTH), F32)] * 5
                       + [pltpu.VMEM((HG_HEADS, HG_DV, HG_DK), F32)],
        compiler_params=_params(("parallel", "arbitrary")),
        name="hgrn",
    )(x, w, loglb, log1mlb, omlb, ng, s0)


def _merge_kernel(x_ref, a_ref, b_ref, c_ref, wg_ref, wa_ref, wb_ref, wc_ref, wo_ref,
                  g_ref, bt_ref, y_ref):
    x = x_ref[...]
    xb = x.astype(BF16)
    m = None
    for i, (br, w) in enumerate(((a_ref, wa_ref), (b_ref, wb_ref), (c_ref, wc_ref))):
        gate = _sigmoid(jnp.dot(xb, wg_ref[:, i * D_MODEL:(i + 1) * D_MODEL],
                                preferred_element_type=F32))
        term = gate * jnp.dot(br[...], w[...], preferred_element_type=F32)
        m = term if m is None else m + term
    mo = jnp.dot(m.astype(BF16), wo_ref[...], preferred_element_type=F32)
    y_ref[...] = _layer_norm(ALPHA * x + mo, g_ref[...], bt_ref[...])


def _merge(x, a, b, c, wg, wa, wb, wc, wo, ln_g, ln_b, *, tm):
    n = x.shape[0]
    row = lambda i: (i, 0)
    full = lambda i: (0, 0)
    br = pl.BlockSpec((tm, ATT_WIDTH), row)
    bw = pl.BlockSpec((ATT_WIDTH, D_MODEL), full)
    vec = pl.BlockSpec((1, D_MODEL), full)
    return pl.pallas_call(
        _merge_kernel,
        grid=(n // tm,),
        in_specs=[pl.BlockSpec((tm, D_MODEL), row), br, br, br,
                  pl.BlockSpec((D_MODEL, N_BRANCH * D_MODEL), full), bw, bw, bw,
                  pl.BlockSpec((D_MODEL, D_MODEL), full), vec, vec],
        out_specs=pl.BlockSpec((tm, D_MODEL), row),
        out_shape=jax.ShapeDtypeStruct((n, D_MODEL), F32),
        compiler_params=_params(("parallel",)),
        name="merge",
    )(x, a, b, c, wg, wa, wb, wc, wo, ln_g, ln_b)


def _route(logits):
    ex = jnp.exp(logits - jnp.max(logits, axis=0, keepdims=True))
    probs = ex / jnp.sum(ex, axis=0, keepdims=True)
    p = [[probs[g * EXP_PER_GROUP + k:g * EXP_PER_GROUP + k + 1, :] for k in range(EXP_PER_GROUP)]
         for g in range(N_GROUPS)]
    score = []
    for g in range(N_GROUPS):
        best = None
        for k1 in range(EXP_PER_GROUP):
            for k2 in range(k1 + 1, EXP_PER_GROUP):
                pair = p[g][k1] + p[g][k2]
                best = pair if best is None else jnp.maximum(best, pair)
        score.append(best)
    gsel = jnp.zeros(score[0].shape, I32)
    top = score[0]
    for g in range(1, N_GROUPS):
        better = score[g] > top
        top = jnp.where(better, score[g], top)
        gsel = jnp.where(better, g, gsel)
    val = []
    for k in range(EXP_PER_GROUP):
        v = p[0][k]
        for g in range(1, N_GROUPS):
            v = jnp.where(gsel == g, p[g][k], v)
        val.append(v)
    v1, i1 = val[0], jnp.zeros(gsel.shape, I32)
    for k in range(1, EXP_PER_GROUP):
        better = val[k] > v1
        v1 = jnp.where(better, val[k], v1)
        i1 = jnp.where(better, k, i1)
    v2, i2 = jnp.full(v1.shape, -1.0, F32), jnp.zeros(gsel.shape, I32)
    for k in range(EXP_PER_GROUP):
        better = (i1 != k) & (val[k] > v2)
        v2 = jnp.where(better, val[k], v2)
        i2 = jnp.where(better, k, i2)
    den = v1 + v2
    w1, w2 = v1 / den, v2 / den
    out = []
    for g in range(N_GROUPS):
        rows = [jnp.where(gsel == g, jnp.where(i1 == k, w1, jnp.where(i2 == k, w2, 0.0)), 0.0)
                for k in range(EXP_PER_GROUP)]
        out.append(jnp.concatenate(rows, axis=0))
    return out


def _moe_kernel(x_ref, wr_ref, br_ref, wg_ref, wu_ref, wd_ref, g_ref, bt_ref, y_ref,
                xb_scr, gate_scr, acc_scr, *, tm):
    grp = pl.program_id(1)

    @pl.when(grp == 0)
    def _():
        xb = x_ref[...].astype(BF16)
        xb_scr[...] = xb
        logits = lax.dot_general(wr_ref[...], xb, _NT, preferred_element_type=F32) + br_ref[...]
        gates = _route(logits)
        pad = jnp.zeros((LANES - EXP_PER_GROUP, tm), F32)
        for g in range(N_GROUPS):
            gate_scr[g] = jnp.concatenate([gates[g], pad], axis=0).T
        acc_scr[...] = jnp.zeros(acc_scr.shape, F32)

    xb = xb_scr[...]
    gate = gate_scr[grp]
    acc = acc_scr[...]
    for k in range(EXP_PER_GROUP):
        h = jnp.dot(xb, wg_ref[k], preferred_element_type=F32)
        up = jnp.dot(xb, wu_ref[k], preferred_element_type=F32)
        act = h * _sigmoid(h) * up * gate[:, k:k + 1]
        acc = acc + jnp.dot(act.astype(BF16), wd_ref[k], preferred_element_type=F32)
    acc_scr[...] = acc

    @pl.when(grp == N_GROUPS - 1)
    def _():
        y_ref[...] = _layer_norm(ALPHA * x_ref[...] + acc, g_ref[...], bt_ref[...])


def _moe(x, wr_t, br, wg, wu, wd, ln_g, ln_b, *, tm):
    n = x.shape[0]
    row = lambda i, g: (i, 0)
    full = lambda i, g: (0, 0)
    vec = pl.BlockSpec((1, D_MODEL), full)
    return pl.pallas_call(
        functools.partial(_moe_kernel, tm=tm),
        grid=(n // tm, N_GROUPS),
        in_specs=[pl.BlockSpec((tm, D_MODEL), row),
                  pl.BlockSpec((N_EXPERTS, D_MODEL), full),
                  pl.BlockSpec((N_EXPERTS, 1), full),
                  pl.BlockSpec((EXP_PER_GROUP, D_MODEL, D_FF), lambda i, g: (g, 0, 0)),
                  pl.BlockSpec((EXP_PER_GROUP, D_MODEL, D_FF), lambda i, g: (g, 0, 0)),
                  pl.BlockSpec((EXP_PER_GROUP, D_FF, D_MODEL), lambda i, g: (g, 0, 0)),
                  vec, vec],
        out_specs=pl.BlockSpec((tm, D_MODEL), row),
        out_shape=jax.ShapeDtypeStruct((n, D_MODEL), F32),
        scratch_shapes=[pltpu.VMEM((tm, D_MODEL), BF16),
                        pltpu.VMEM((N_GROUPS, tm, LANES), F32),
                        pltpu.VMEM((tm, D_MODEL), F32)],
        compiler_params=_params(("parallel", "arbitrary")),
        name="moe",
    )(x, wr_t, br, wg, wu, wd, ln_g, ln_b)


WPREP_ROWS = 128


def _split_w_in_kernel(w_ref, wn_ref, wt_ref, wu_ref, wv_ref, whg_ref, wg_ref):
    w = w_ref[0]
    offs = [0]
    for n in SPLIT_SIZES:
        offs.append(offs[-1] + n)
    seg = lambda i, j=None: w[:, offs[i]:offs[i + 1 if j is None else j]]
    zeros = lambda c: jnp.zeros((w.shape[0], c), w.dtype)
    wn_ref[0] = jnp.concatenate([seg(1), seg(2), seg(4), zeros(LANES - IDX_DIM)], axis=1).astype(BF16)
    wt = jnp.concatenate([seg(0), seg(3), seg(2), seg(5), zeros(ATT_T_PAD - ATT_T_ROWS + 8 - IDX_HEADS)],
                         axis=1)
    for c in range(ATT_T_PAD // LANES):
        rows = min(LANES, ATT_T_ROWS - c * LANES)
        wt_ref[0, c * LANES:c * LANES + rows, :] = wt[:, c * LANES:(c + 1) * LANES].T[:rows].astype(BF16)
    wu_ref[0] = seg(6).astype(BF16)
    wv_ref[0] = seg(7).astype(BF16)
    whg_ref[0] = seg(8, 12).astype(BF16)
    wg_ref[0] = seg(12).astype(BF16)


def _split_w_in(w_in):
    depth, d, ncol = w_in.shape
    tr = WPREP_ROWS
    cols = (ATT_N_COLS, None, SG_WIDTH, SG_WIDTH, 4 * HG_WIDTH, N_BRANCH * D_MODEL)
    shapes = [(depth, ATT_T_ROWS, d) if c is None else (depth, d, c) for c in cols]
    specs = [pl.BlockSpec((1, ATT_T_ROWS, tr), lambda l, i: (l, 0, i)) if c is None
             else pl.BlockSpec((1, tr, c), lambda l, i: (l, i, 0)) for c in cols]
    return pl.pallas_call(
        _split_w_in_kernel,
        grid=(depth, d // tr),
        in_specs=[pl.BlockSpec((1, tr, ncol), lambda l, i: (l, i, 0))],
        out_specs=specs,
        out_shape=[jax.ShapeDtypeStruct(sh, BF16) for sh in shapes],
        compiler_params=_params(("parallel", "parallel")),
        name="split_w_in",
    )(w_in)


def _row_tile(n, want):
    while n % want:
        want //= 2
    return want


def _layer(x, bsz, t, lw, *, pos0, topk, cache, s0):
    n = bsz * t
    lc = min(t, SG_LEN)
    k, v, ki, kb, kib, qt, qit, vt, wit = _attn_proj(x, lw["w_att_n"], lw["w_att_t"], bsz, t,
                                                     _row_tile(t, 512))
    kb3 = kb.reshape(bsz, t, KV_WIDTH)
    kib3 = kib.reshape(bsz, t, IDX_DIM)
    if cache is not None:
        ck, cvt, cki = cache
        kb3 = jnp.concatenate([ck, kb3], axis=1)
        kib3 = jnp.concatenate([cki, kib3], axis=1)
        vt = jnp.concatenate([cvt, vt], axis=2)
    ltot = kb3.shape[1]
    lp = -(-ltot // DSA_KEY_BLOCK) * DSA_KEY_BLOCK
    tq = LANES
    tp = -(-t // tq) * tq
    padq = lambda a: jnp.pad(a, ((0, 0), (0, 0), (0, tp - t)))
    a = _dsa(padq(qt), padq(qit), padq(wit), jnp.pad(kb3, ((0, 0), (0, lp - ltot), (0, 0))),
             jnp.pad(vt, ((0, 0), (0, 0), (0, lp - ltot))),
             jnp.pad(kib3, ((0, 0), (0, lp - ltot), (0, 0))), tq=tq, pos0=pos0, topk=topk)[:, :t]
    b, v_gm = _gmlp(x, lw["w_u"], lw["w_v"], lw["ln_sg_g"], lw["ln_sg_b"],
                    lw["w_sg"][:, :lc, :lc], lw["b_sg"][:, :lc].T, tm=_row_tile(n, 256), lc=lc)
    c, s_new = _hgrn(x.reshape(bsz, t, D_MODEL), lw["w_hg"], lw["log_lb"], lw["log1m_lb"],
                     lw["om_lb"], lw["hg_norm_g"], s0, tc=min(t, 256))
    x1 = _merge(x, a.reshape(n, ATT_WIDTH), b, c.reshape(n, HG_WIDTH), lw["w_gates"],
                lw["w_branch_a"], lw["w_branch_b"], lw["w_branch_c"], lw["w_out"],
                lw["ln1_g"], lw["ln1_b"], tm=_row_tile(n, 256))
    x2 = _moe(x1, lw["w_router_t"], lw["b_router"], lw["w_exp_gate"], lw["w_exp_up"],
              lw["w_exp_down"], lw["ln2_g"], lw["ln2_b"], tm=_row_tile(n, 512))
    return x2, k, v, ki, s_new, v_gm


def kernel(x_prompt, x_sample, cache_k, cache_v, cache_kidx, state_hgrn, w_in, w_sg, b_sg, ln_sg_g, ln_sg_b, hg_lb_logits, hg_norm_g, w_branch_a, w_branch_b, w_branch_c, w_out, ln1_g, ln1_b, w_router, b_router, w_exp_gate, w_exp_up, w_exp_down, ln2_g, ln2_b):
    bp, sp, _ = x_prompt.shape
    bs, ss, _ = x_sample.shape
    past = cache_k.shape[2]
    topk_p = min(TOPK_MAX, sp // 4)
    topk_s = min(TOPK_MAX, (past + ss) // 4)

    lb_all = jnp.cumsum(jax.nn.softmax(hg_lb_logits.astype(F32), axis=0), axis=0)
    lb_all = lb_all - lb_all[0:1]
    vec = lambda a: a.reshape(1, -1)

    xp = x_prompt.reshape(bp * sp, D_MODEL)
    xs = x_sample.reshape(bs * ss, D_MODEL)
    s0_p = jnp.zeros((bp, HG_HEADS, HG_DK, HG_DV), F32)
    outs_p, outs_s = [], []
    w_att_n, w_att_t, w_u, w_v, w_hg, w_gates = _split_w_in(w_in)
    for l in range(DEPTH):
        lw = dict(
            w_att_n=w_att_n[l], w_att_t=w_att_t[l], w_u=w_u[l], w_v=w_v[l], w_hg=w_hg[l],
            w_gates=w_gates[l],
            ln_sg_g=vec(ln_sg_g[l]), ln_sg_b=vec(ln_sg_b[l]), w_sg=w_sg[l], b_sg=b_sg[l],
            log_lb=vec(jnp.log(lb_all[l])), log1m_lb=vec(jnp.log1p(-lb_all[l])),
            om_lb=vec(1.0 - lb_all[l]), hg_norm_g=vec(hg_norm_g[l].astype(F32)),
            w_branch_a=w_branch_a[l].astype(BF16), w_branch_b=w_branch_b[l].astype(BF16),
            w_branch_c=w_branch_c[l].astype(BF16), w_out=w_out[l].astype(BF16),
            ln1_g=vec(ln1_g[l]), ln1_b=vec(ln1_b[l]),
            w_router_t=w_router.T.astype(BF16), b_router=b_router.astype(F32).reshape(-1, 1),
            w_exp_gate=w_exp_gate[l].astype(BF16), w_exp_up=w_exp_up[l].astype(BF16),
            w_exp_down=w_exp_down[l].astype(BF16),
            ln2_g=vec(ln2_g[l]), ln2_b=vec(ln2_b[l]))
        xp, k, v, ki, s_new, _ = _layer(xp, bp, sp, lw, pos0=0, topk=topk_p, cache=None, s0=s0_p)
        outs_p.append((k.reshape(bp, sp, KV_HEADS, HEAD_DIM), v.reshape(bp, sp, KV_HEADS, HEAD_DIM),
                       ki.reshape(bp, sp, IDX_DIM), s_new))
        cache = (cache_k[l].reshape(bs, past, KV_WIDTH).astype(BF16),
                 jnp.swapaxes(cache_v[l].reshape(bs, past, KV_WIDTH), 1, 2).astype(BF16),
                 cache_kidx[l].astype(BF16))
        xs, k, v, ki, s_new, v_gm = _layer(xs, bs, ss, lw, pos0=past, topk=topk_s, cache=cache,
                                           s0=state_hgrn[l].astype(F32))
        outs_s.append((k.reshape(bs, ss, KV_HEADS, HEAD_DIM), v.reshape(bs, ss, KV_HEADS, HEAD_DIM),
                       ki.reshape(bs, ss, IDX_DIM), s_new, v_gm.reshape(bs, ss, SG_WIDTH)))

    stack = lambda rows, i: jnp.stack([r[i] for r in rows])
    return (xp.reshape(bp, sp, D_MODEL), xs.reshape(bs, ss, D_MODEL),
            stack(outs_p, 0), stack(outs_p, 1), stack(outs_p, 2), stack(outs_p, 3),
            stack(outs_s, 0), stack(outs_s, 1), stack(outs_s, 2), stack(outs_s, 3),
            stack(outs_s, 4))
```

```python
import functools

import jax
import jax.numpy as jnp
from jax import lax
from jax.experimental import pallas as pl
from jax.experimental.pallas import tpu as pltpu

F32 = jnp.float32
BF16 = jnp.bfloat16
I32 = jnp.int32

D_MODEL = 1024
DEPTH = 4
CHUNK = 64
ATT_HEADS = 8
KV_HEADS = 2
HEAD_DIM = 64
ATT_WIDTH = ATT_HEADS * HEAD_DIM
KV_WIDTH = KV_HEADS * HEAD_DIM
IDX_HEADS = 4
IDX_DIM = 64
TOPK_MAX = 256
SG_LEN = 128
SG_GROUPS = 4
SG_WIDTH = 512
SG_GW = SG_WIDTH // SG_GROUPS
HG_HEADS = 4
HG_DK = 128
HG_DV = 128
HG_WIDTH = HG_HEADS * HG_DV
HG_BLOCK = 16
HG_FACTOR_RANGE = 80.0
N_BRANCH = 3
N_EXPERTS = 16
N_GROUPS = 4
EXP_PER_GROUP = 4
D_FF = 256
ALPHA = (2 * DEPTH) ** 0.25
LN_EPS = 1e-5

SPLIT_SIZES = (ATT_WIDTH, KV_WIDTH, KV_WIDTH, IDX_HEADS * IDX_DIM, IDX_DIM, IDX_HEADS,
               SG_WIDTH, SG_WIDTH, HG_HEADS * HG_DK, HG_HEADS * HG_DK, HG_WIDTH, HG_WIDTH,
               N_BRANCH * D_MODEL)

LANES = 128
INT_MIN = -2 ** 31
NEG_BIG = -1e30
LOG2_E = 1.4426950408889634
DSA_KEY_BLOCK = 512
VMEM_LIMIT = 56 * 1024 * 1024

_NT = (((1,), (1,)), ((), ()))
_TN = (((0,), (0,)), ((), ()))


def _params(sem):
    return pltpu.CompilerParams(dimension_semantics=sem, vmem_limit_bytes=VMEM_LIMIT)


def _layer_norm(x, g, b):
    mu = jnp.mean(x, axis=-1, keepdims=True)
    d = x - mu
    var = jnp.mean(d * d, axis=-1, keepdims=True)
    return d * lax.rsqrt(var + LN_EPS) * g + b


def _gelu_tanh(x):
    return 0.5 * x * (1.0 + jnp.tanh(0.7978845608028654 * (x + 0.044715 * (x * x * x))))


def _sigmoid(x):
    return 1.0 / (1.0 + jnp.exp(-x))


def _fold_rows(x, op):
    parts = [x[i:i + 8] for i in range(0, x.shape[0], 8)]
    while len(parts) > 1:
        parts = [op(parts[i], parts[i + 1]) for i in range(0, len(parts), 2)]
    return parts[0]


ATT_T_ROWS = ATT_WIDTH + IDX_HEADS * IDX_DIM + KV_WIDTH + 8
ATT_T_PAD = -(-ATT_T_ROWS // LANES) * LANES
ATT_N_COLS = 3 * LANES


def _attn_proj_kernel(x_ref, wn_ref, wt_ref, k_ref, v_ref, ki_ref, kb_ref, kib_ref,
                      qt_ref, qit_ref, vt_ref, wit_ref):
    xb = x_ref[...].astype(BF16)
    z = jnp.dot(xb, wn_ref[...], preferred_element_type=F32)
    k = z[:, 0:KV_WIDTH]
    ki = z[:, 2 * KV_WIDTH:2 * KV_WIDTH + IDX_DIM]
    k_ref[...] = k
    v_ref[...] = z[:, KV_WIDTH:2 * KV_WIDTH]
    ki_ref[...] = ki
    kb_ref[...] = k.astype(BF16)
    kib_ref[...] = ki.astype(BF16)
    zt = lax.dot_general(wt_ref[...], xb, _NT, preferred_element_type=F32)
    o = 0
    qt_ref[0] = (zt[o:o + ATT_WIDTH] * (HEAD_DIM ** -0.5 * LOG2_E)).astype(BF16)
    o += ATT_WIDTH
    qit_ref[0] = (zt[o:o + IDX_HEADS * IDX_DIM] * IDX_DIM ** -0.5).astype(BF16)
    o += IDX_HEADS * IDX_DIM
    vt_ref[0] = zt[o:o + KV_WIDTH].astype(BF16)
    o += KV_WIDTH
    wit_ref[0] = zt[o:o + 8] * IDX_HEADS ** -0.5


def _attn_proj(x, wn, wt, bsz, t, tm):
    n = bsz * t
    per = t // tm
    row = lambda i: (i, 0)
    tmap = lambda i: (i // per, 0, i % per)
    nat = ((KV_WIDTH, F32), (KV_WIDTH, F32), (IDX_DIM, F32), (KV_WIDTH, BF16), (IDX_DIM, BF16))
    tr = ((ATT_WIDTH, BF16), (IDX_HEADS * IDX_DIM, BF16), (KV_WIDTH, BF16), (8, F32))
    return pl.pallas_call(
        _attn_proj_kernel,
        grid=(n // tm,),
        in_specs=[pl.BlockSpec((tm, D_MODEL), row),
                  pl.BlockSpec((D_MODEL, ATT_N_COLS), lambda i: (0, 0)),
                  pl.BlockSpec((ATT_T_ROWS, D_MODEL), lambda i: (0, 0))],
        out_specs=[pl.BlockSpec((tm, c), row) for c, _ in nat]
                  + [pl.BlockSpec((1, c, tm), tmap) for c, _ in tr],
        out_shape=[jax.ShapeDtypeStruct((n, c), d) for c, d in nat]
                  + [jax.ShapeDtypeStruct((bsz, c, t), d) for c, d in tr],
        compiler_params=_params(("parallel",)),
        name="attn_proj",
    )(x, wn, wt)


def _dsa_kernel(qt_ref, qit_ref, wit_ref, kb_ref, vt_ref, kib_ref, o_ref,
                key_scr, bias_scr, s_scr, acc_scr, m_scr, a_scr,
                *, tq, kblk, pos0, topk):
    j = pl.program_id(1)
    qpos0 = pos0 + j * tq
    qcol = lax.broadcasted_iota(I32, (1, tq), 1)
    chunk_end = ((qpos0 + qcol) // CHUNK + 1) * CHUNK
    lvis = ((qpos0 + tq - 1) // CHUNK + 1) * CHUNK
    nblk = (lvis + kblk - 1) // kblk
    krow = lax.broadcasted_iota(I32, (kblk, tq), 0)

    wi = wit_ref[0]
    qit = qit_ref[0]

    def score_block(i, carry):
        off = pl.multiple_of(i * kblk, kblk)
        kib = kib_ref[0, pl.ds(off, kblk), :]
        sc = jnp.zeros((kblk, tq), F32)
        for h in range(IDX_HEADS):
            raw = jnp.dot(kib, qit[h * IDX_DIM:(h + 1) * IDX_DIM, :], preferred_element_type=F32)
            sc = sc + jnp.maximum(raw, 0.0) * wi[h:h + 1, :]
        bits = pltpu.bitcast(sc, I32)
        key = bits ^ ((bits >> 31) & 0x7FFFFFFF)
        key_scr[i] = jnp.where(off + krow < chunk_end, key, INT_MIN)
        return carry

    lax.fori_loop(0, nblk, score_block, 0)

    def count(hits):
        def body(i, acc):
            return acc + _fold_rows(hits(i, key_scr[i]), jnp.add)
        acc = lax.fori_loop(0, nblk, body, jnp.zeros((8, tq), F32))
        return jnp.sum(acc, axis=0, keepdims=True)

    def key_bit(i, thr):
        cand = thr + jnp.left_shift(jnp.int32(1), 31 - i)
        cnt = count(lambda _, kb: jnp.where(kb >= cand, 1.0, 0.0))
        return jnp.where(cnt >= topk, cand, thr)

    thr = lax.fori_loop(0, 32, key_bit, jnp.full((1, tq), INT_MIN, I32))

    need = topk - count(lambda _, kb: jnp.where(kb > thr, 1.0, 0.0))
    tri = jnp.where(lax.broadcasted_iota(I32, (kblk, kblk), 0) >= lax.broadcasted_iota(I32, (kblk, kblk), 1),
                    1.0, 0.0).astype(BF16)

    def select_block(i, taken):
        off = pl.multiple_of(i * kblk, kblk)
        kb = key_scr[i]
        tied = jnp.where(kb == thr, 1.0, 0.0)
        rank = taken + jnp.dot(tri, tied.astype(BF16), preferred_element_type=F32)
        tie = jnp.where(rank <= need, 0.0, NEG_BIG)
        bias = jnp.where(kb > thr, 0.0, jnp.where(kb == thr, tie, NEG_BIG))
        bias_scr[i] = jnp.where(off + krow < chunk_end, bias, NEG_BIG)
        return rank[kblk - 1:kblk, :]

    lax.fori_loop(0, nblk, select_block, jnp.zeros((1, tq), F32))

    m_scr[...] = jnp.full(m_scr.shape, NEG_BIG, F32)
    acc_scr[...] = jnp.zeros(acc_scr.shape, F32)
    rep = ATT_HEADS // KV_HEADS
    qt = qt_ref[0]
    qg = [jnp.concatenate([qt[(g * rep + r) * HEAD_DIM:(g * rep + r + 1) * HEAD_DIM, :]
                           for r in range(rep)], axis=1) for g in range(KV_HEADS)]
    ones = jnp.ones((16, kblk), BF16)

    def score_phase(i, buf):
        off = pl.multiple_of(i * kblk, kblk)
        bias = jnp.concatenate([bias_scr[i]] * rep, axis=1)
        kk = kb_ref[0, pl.ds(off, kblk), :]
        for g in range(KV_HEADS):
            s = jnp.dot(kk[:, g * HEAD_DIM:(g + 1) * HEAD_DIM], qg[g],
                        preferred_element_type=F32) + bias
            s_scr[buf, g] = s
            m_blk = jnp.max(_fold_rows(s, jnp.maximum), axis=0, keepdims=True)
            m_old = m_scr[g]
            m_new = jnp.maximum(m_old, m_blk)
            a_scr[g] = jnp.exp2(m_old - m_new)
            m_scr[g] = m_new

    def value_phase(i, buf):
        off = pl.multiple_of(i * kblk, kblk)
        vv = vt_ref[0, :, pl.ds(off, kblk)]
        for g in range(KV_HEADS):
            p = jnp.exp2(s_scr[buf, g] - m_scr[g]).astype(BF16)
            vg = jnp.concatenate([vv[g * HEAD_DIM:(g + 1) * HEAD_DIM, :], ones], axis=0)
            acc_scr[g] = a_scr[g] * acc_scr[g] + jnp.dot(vg, p, preferred_element_type=F32)

    score_phase(0, 0)

    def attn_pair(k, carry):
        i = 2 * k
        value_phase(i, 0)
        score_phase(i + 1, 1)
        value_phase(i + 1, 1)
        score_phase(jnp.minimum(i + 2, nblk - 1), 0)
        return carry

    lax.fori_loop(0, nblk // 2, attn_pair, 0)

    @pl.when(nblk % 2 == 1)
    def _():
        value_phase(nblk - 1, 0)

    heads = []
    for g in range(KV_HEADS):
        acc = acc_scr[g]
        og = acc[:HEAD_DIM] / acc[HEAD_DIM:HEAD_DIM + 1]
        heads += [og[:, r * tq:(r + 1) * tq] for r in range(rep)]
    o_ref[0] = jnp.concatenate(heads, axis=0).T.astype(BF16)


def _dsa(qt, qit, wit, kb, vt, kib, *, tq, pos0, topk):
    b, _, t = qt.shape
    lp = kb.shape[1]
    kblk = DSA_KEY_BLOCK
    assert lp % kblk == 0 and t % tq == 0 and tq % LANES == 0
    kern = functools.partial(_dsa_kernel, tq=tq, kblk=kblk, pos0=pos0, topk=topk)
    rep = ATT_HEADS // KV_HEADS
    qmap = lambda bi, j: (bi, 0, j)
    kmap = lambda bi, j: (bi, 0, 0)
    return pl.pallas_call(
        kern,
        grid=(b, t // tq),
        in_specs=[pl.BlockSpec((1, ATT_WIDTH, tq), qmap),
                  pl.BlockSpec((1, IDX_HEADS * IDX_DIM, tq), qmap),
                  pl.BlockSpec((1, 8, tq), qmap),
                  pl.BlockSpec((1, lp, KV_WIDTH), kmap),
                  pl.BlockSpec((1, KV_WIDTH, lp), kmap),
                  pl.BlockSpec((1, lp, IDX_DIM), kmap)],
        out_specs=pl.BlockSpec((1, tq, ATT_WIDTH), lambda bi, j: (bi, j, 0)),
        out_shape=jax.ShapeDtypeStruct((b, t, ATT_WIDTH), BF16),
        scratch_shapes=[pltpu.VMEM((lp // kblk, kblk, tq), I32),
                        pltpu.VMEM((lp // kblk, kblk, tq), F32),
                        pltpu.VMEM((2, KV_HEADS, kblk, rep * tq), F32),
                        pltpu.VMEM((KV_HEADS, HEAD_DIM + 16, rep * tq), F32),
                        pltpu.VMEM((KV_HEADS, 1, rep * tq), F32),
                        pltpu.VMEM((KV_HEADS, 1, rep * tq), F32)],
        compiler_params=_params(("parallel", "arbitrary")),
        name="dsa",
    )(qt, qit, wit, kb, vt, kib)


def _gmlp_kernel(x_ref, wu_ref, wv_ref, g_ref, b_ref, ws_ref, bst_ref, o_ref, v_ref, *, tm, lc):
    xb = x_ref[...].astype(BF16)
    u = _gelu_tanh(jnp.dot(xb, wu_ref[...], preferred_element_type=F32))
    v = _layer_norm(_gelu_tanh(jnp.dot(xb, wv_ref[...], preferred_element_type=F32)),
                    g_ref[...], b_ref[...])
    v_ref[...] = v
    vb = v.astype(BF16)
    r = lax.broadcasted_iota(I32, (lc, lc), 0)
    c = lax.broadcasted_iota(I32, (lc, lc), 1)
    for g in range(SG_GROUPS):
        w = jnp.where(r >= c, ws_ref[g], 0.0).astype(BF16)
        bias = bst_ref[:, g:g + 1]
        cs = slice(g * SG_GW, (g + 1) * SG_GW)
        for n in range(tm // lc):
            rs = slice(n * lc, (n + 1) * lc)
            mixed = jnp.dot(w, vb[rs, cs], preferred_element_type=F32) + bias
            o_ref[rs, cs] = (u[rs, cs] * mixed).astype(BF16)


def _gmlp(x, wu, wv, ln_g, ln_b, ws, bst, *, tm, lc):
    n = x.shape[0]
    row = lambda i: (i, 0)
    full2 = lambda i: (0, 0)
    return pl.pallas_call(
        functools.partial(_gmlp_kernel, tm=tm, lc=lc),
        grid=(n // tm,),
        in_specs=[pl.BlockSpec((tm, D_MODEL), row),
                  pl.BlockSpec((D_MODEL, SG_WIDTH), full2),
                  pl.BlockSpec((D_MODEL, SG_WIDTH), full2),
                  pl.BlockSpec((1, SG_WIDTH), full2),
                  pl.BlockSpec((1, SG_WIDTH), full2),
                  pl.BlockSpec((SG_GROUPS, lc, lc), lambda i: (0, 0, 0)),
                  pl.BlockSpec((lc, SG_GROUPS), full2)],
        out_specs=[pl.BlockSpec((tm, SG_WIDTH), row), pl.BlockSpec((tm, SG_WIDTH), row)],
        out_shape=[jax.ShapeDtypeStruct((n, SG_WIDTH), BF16),
                   jax.ShapeDtypeStruct((n, SG_WIDTH), F32)],
        compiler_params=_params(("parallel",)),
        name="gmlp",
    )(x, wu, wv, ln_g, ln_b, ws, bst)


def _hgrn_kernel(x_ref, w_ref, loglb_ref, log1mlb_ref, omlb_ref, ng_ref, s0_ref, c_ref, sout_ref,
                 q_scr, k_scr, v_scr, lf_scr, o_scr, st_scr, *, tc):
    t = pl.program_id(1)

    @pl.when(t == 0)
    def _():
        for h in range(HG_HEADS):
            st_scr[h] = s0_ref[0, h].T

    xb = x_ref[0].astype(BF16)
    z = jnp.dot(xb, w_ref[...], preferred_element_type=F32)
    hq = z[:, 0:HG_WIDTH]
    hf = z[:, HG_WIDTH:2 * HG_WIDTH]
    hg = z[:, 3 * HG_WIDTH:4 * HG_WIDTH]
    q_scr[...] = hq * _sigmoid(hq) * HG_DK ** -0.5
    log_sig = jnp.minimum(hf, 0.0) - jnp.log1p(jnp.exp(-jnp.abs(hf)))
    y = log1mlb_ref[...] + log_sig
    a = loglb_ref[...]
    lf_scr[...] = jnp.maximum(a, y) + jnp.log1p(jnp.exp(-jnp.abs(a - y)))
    k_scr[...] = omlb_ref[...] * _sigmoid(-hf)
    v_scr[...] = z[:, 2 * HG_WIDTH:3 * HG_WIDTH]

    ch = min(tc, CHUNK)
    ri = lax.broadcasted_iota(I32, (tc, tc), 0)
    ci = lax.broadcasted_iota(I32, (tc, tc), 1)
    same = (ri // ch) == (ci // ch)
    causal = same & (ri >= ci)
    lf = lf_scr[...]
    bcum_all = jnp.dot(jnp.where(causal, 1.0, 0.0), lf, preferred_element_type=F32,
                       precision=lax.Precision.HIGHEST)
    factored = jnp.min(bcum_all) > -HG_FACTOR_RANGE

    @pl.when(factored)
    def _():
        btot = jnp.dot(jnp.where(same, 1.0, 0.0), lf, preferred_element_type=F32,
                       precision=lax.Precision.HIGHEST)
        qa = q_scr[...]
        ka = k_scr[...]
        qd = (qa * jnp.exp(bcum_all)).astype(BF16)
        ki = (ka * jnp.exp(-bcum_all)).astype(BF16)
        kd = (ka * jnp.exp(btot - bcum_all)).astype(BF16)
        va = v_scr[...].astype(BF16)
        for h in range(HG_HEADS):
            cs = slice(h * HG_DK, (h + 1) * HG_DK)
            att = lax.dot_general(qd[:, cs], ki[:, cs], _NT, preferred_element_type=F32)
            att = jnp.where(causal, att, 0.0).astype(BF16)
            o = jnp.dot(att, va[:, cs], preferred_element_type=F32)
            st = st_scr[h]
            for n in range(tc // ch):
                rs = slice(n * ch, (n + 1) * ch)
                o_scr[rs, cs] = o[rs] + lax.dot_general(qd[rs, cs], st.astype(BF16), _NT,
                                                        preferred_element_type=F32)
                upd = lax.dot_general(va[rs, cs], kd[rs, cs], _TN, preferred_element_type=F32)
                st = jnp.exp(btot[n * ch:n * ch + 1, cs]) * st + upd
            st_scr[h] = st

    nb = HG_BLOCK
    tri = jnp.where(lax.broadcasted_iota(I32, (nb, nb), 0) >= lax.broadcasted_iota(I32, (nb, nb), 1),
                    1.0, 0.0).astype(F32)
    trow = lax.broadcasted_iota(I32, (nb, 1), 0)

    def block(i, carry):
        r0 = pl.multiple_of(i * nb, nb)
        rows = pl.ds(r0, nb)
        for h in range(HG_HEADS):
            cs = slice(h * HG_DK, (h + 1) * HG_DK)
            bcum = jnp.dot(tri, lf_scr[rows, cs], preferred_element_type=F32,
                           precision=lax.Precision.HIGHEST)
            qb = q_scr[rows, cs]
            kb = k_scr[rows, cs]
            vb = v_scr[rows, cs]
            o = jnp.zeros((nb, HG_DV), F32)
            for s in range(nb):
                e = jnp.exp(jnp.where(trow >= s, bcum - bcum[s:s + 1, :], -jnp.inf))
                a_ts = jnp.sum(qb * kb[s:s + 1, :] * e, axis=1, keepdims=True)
                o = o + a_ts * vb[s:s + 1, :]
            st = st_scr[h]
            qd = (qb * jnp.exp(bcum)).astype(BF16)
            o = o + lax.dot_general(qd, st.astype(BF16), _NT, preferred_element_type=F32)
            blast = bcum[nb - 1:nb, :]
            kd = (kb * jnp.exp(blast - bcum)).astype(BF16)
            upd = lax.dot_general(vb.astype(BF16), kd, _TN, preferred_element_type=F32)
            st_scr[h] = jnp.exp(blast) * st + upd
            o_scr[rows, cs] = o
        return carry

    @pl.when(jnp.logical_not(factored))
    def _():
        lax.fori_loop(0, tc // nb, block, 0)

    for h in range(HG_HEADS):
        cs = slice(h * HG_DV, (h + 1) * HG_DV)
        o = o_scr[:, cs]
        o = o * lax.rsqrt(jnp.mean(o * o, axis=-1, keepdims=True) + LN_EPS)
        g = hg[:, cs]
        c_ref[0, :, cs] = (o * ng_ref[:, cs] * (g * _sigmoid(g))).astype(BF16)

    @pl.when(t == pl.num_programs(1) - 1)
    def _():
        for h in range(HG_HEADS):
            sout_ref[0, h] = st_scr[h].T


def _hgrn(x, w, loglb, log1mlb, omlb, ng, s0, *, tc):
    b, t, _ = x.shape
    xmap = lambda bi, j: (bi, j, 0)
    vec = pl.BlockSpec((1, HG_WIDTH), lambda bi, j: (0, 0))
    smap = lambda bi, j: (bi, 0, 0, 0)
    return pl.pallas_call(
        functools.partial(_hgrn_kernel, tc=tc),
        grid=(b, t // tc),
        in_specs=[pl.BlockSpec((1, tc, D_MODEL), xmap),
                  pl.BlockSpec((D_MODEL, 4 * HG_WIDTH), lambda bi, j: (0, 0)),
                  vec, vec, vec, vec,
                  pl.BlockSpec((1, HG_HEADS, HG_DK, HG_DV), smap)],
        out_specs=[pl.BlockSpec((1, tc, HG_WIDTH), xmap),
                   pl.BlockSpec((1, HG_HEADS, HG_DK, HG_DV), smap)],
        out_shape=[jax.ShapeDtypeStruct((b, t, HG_WIDTH), BF16),
                   jax.ShapeDtypeStruct((b, HG_HEADS, HG_DK, HG_DV), F32)],
        scratch_shapes=[pltpu.VMEM((tc, HG_WIDTH), F32)] * 5
                       + [pltpu.VMEM((HG_HEADS, HG_DV, HG_DK), F32)],
        compiler_params=_params(("parallel", "arbitrary")),
        name="hgrn",
    )(x, w, loglb, log1mlb, omlb, ng, s0)


def _merge_kernel(x_ref, a_ref, b_ref, c_ref, wg_ref, wa_ref, wb_ref, wc_ref, wo_ref,
                  g_ref, bt_ref, y_ref):
    x = x_ref[...]
    xb = x.astype(BF16)
    m = None
    for i, (br, w) in enumerate(((a_ref, wa_ref), (b_ref, wb_ref), (c_ref, wc_ref))):
        gate = _sigmoid(jnp.dot(xb, wg_ref[:, i * D_MODEL:(i + 1) * D_MODEL],
                                preferred_element_type=F32))
        term = gate * jnp.dot(br[...], w[...], preferred_element_type=F32)
        m = term if m is None else m + term
    mo = jnp.dot(m.astype(BF16), wo_ref[...], preferred_element_type=F32)
    y_ref[...] = _layer_norm(ALPHA * x + mo, g_ref[...], bt_ref[...])


def _merge(x, a, b, c, wg, wa, wb, wc, wo, ln_g, ln_b, *, tm):
    n = x.shape[0]
    row = lambda i: (i, 0)
    full = lambda i: (0, 0)
    br = pl.BlockSpec((tm, ATT_WIDTH), row)
    bw = pl.BlockSpec((ATT_WIDTH, D_MODEL), full)
    vec = pl.BlockSpec((1, D_MODEL), full)
    return pl.pallas_call(
        _merge_kernel,
        grid=(n // tm,),
        in_specs=[pl.BlockSpec((tm, D_MODEL), row), br, br, br,
                  pl.BlockSpec((D_MODEL, N_BRANCH * D_MODEL), full), bw, bw, bw,
                  pl.BlockSpec((D_MODEL, D_MODEL), full), vec, vec],
        out_specs=pl.BlockSpec((tm, D_MODEL), row),
        out_shape=jax.ShapeDtypeStruct((n, D_MODEL), F32),
        compiler_params=_params(("parallel",)),
        name="merge",
    )(x, a, b, c, wg, wa, wb, wc, wo, ln_g, ln_b)


def _route(logits):
    ex = jnp.exp(logits - jnp.max(logits, axis=0, keepdims=True))
    probs = ex / jnp.sum(ex, axis=0, keepdims=True)
    p = [[probs[g * EXP_PER_GROUP + k:g * EXP_PER_GROUP + k + 1, :] for k in range(EXP_PER_GROUP)]
         for g in range(N_GROUPS)]
    score = []
    for g in range(N_GROUPS):
        best = None
        for k1 in range(EXP_PER_GROUP):
            for k2 in range(k1 + 1, EXP_PER_GROUP):
                pair = p[g][k1] + p[g][k2]
                best = pair if best is None else jnp.maximum(best, pair)
        score.append(best)
    gsel = jnp.zeros(score[0].shape, I32)
    top = score[0]
    for g in range(1, N_GROUPS):
        better = score[g] > top
        top = jnp.where(better, score[g], top)
        gsel = jnp.where(better, g, gsel)
    val = []
    for k in range(EXP_PER_GROUP):
        v = p[0][k]
        for g in range(1, N_GROUPS):
            v = jnp.where(gsel == g, p[g][k], v)
        val.append(v)
    v1, i1 = val[0], jnp.zeros(gsel.shape, I32)
    for k in range(1, EXP_PER_GROUP):
        better = val[k] > v1
        v1 = jnp.where(better, val[k], v1)
        i1 = jnp.where(better, k, i1)
    v2, i2 = jnp.full(v1.shape, -1.0, F32), jnp.zeros(gsel.shape, I32)
    for k in range(EXP_PER_GROUP):
        better = (i1 != k) & (val[k] > v2)
        v2 = jnp.where(better, val[k], v2)
        i2 = jnp.where(better, k, i2)
    den = v1 + v2
    w1, w2 = v1 / den, v2 / den
    out = []
    for g in range(N_GROUPS):
        rows = [jnp.where(gsel == g, jnp.where(i1 == k, w1, jnp.where(i2 == k, w2, 0.0)), 0.0)
                for k in range(EXP_PER_GROUP)]
        out.append(jnp.concatenate(rows, axis=0))
    return out


def _moe_kernel(x_ref, wr_ref, br_ref, wg_ref, wu_ref, wd_ref, g_ref, bt_ref, y_ref,
                xb_scr, gate_scr, acc_scr, *, tm):
    grp = pl.program_id(1)

    @pl.when(grp == 0)
    def _():
        xb = x_ref[...].astype(BF16)
        xb_scr[...] = xb
        logits = lax.dot_general(wr_ref[...], xb, _NT, preferred_element_type=F32) + br_ref[...]
        gates = _route(logits)
        pad = jnp.zeros((LANES - EXP_PER_GROUP, tm), F32)
        for g in range(N_GROUPS):
            gate_scr[g] = jnp.concatenate([gates[g], pad], axis=0).T
        acc_scr[...] = jnp.zeros(acc_scr.shape, F32)

    xb = xb_scr[...]
    gate = gate_scr[grp]
    acc = acc_scr[...]
    for k in range(EXP_PER_GROUP):
        h = jnp.dot(xb, wg_ref[k], preferred_element_type=F32)
        up = jnp.dot(xb, wu_ref[k], preferred_element_type=F32)
        act = h * _sigmoid(h) * up * gate[:, k:k + 1]
        acc = acc + jnp.dot(act.astype(BF16), wd_ref[k], preferred_element_type=F32)
    acc_scr[...] = acc

    @pl.when(grp == N_GROUPS - 1)
    def _():
        y_ref[...] = _layer_norm(ALPHA * x_ref[...] + acc, g_ref[...], bt_ref[...])


def _moe(x, wr_t, br, wg, wu, wd, ln_g, ln_b, *, tm):
    n = x.shape[0]
    row = lambda i, g: (i, 0)
    full = lambda i, g: (0, 0)
    vec = pl.BlockSpec((1, D_MODEL), full)
    return pl.pallas_call(
        functools.partial(_moe_kernel, tm=tm),
        grid=(n // tm, N_GROUPS),
        in_specs=[pl.BlockSpec((tm, D_MODEL), row),
                  pl.BlockSpec((N_EXPERTS, D_MODEL), full),
                  pl.BlockSpec((N_EXPERTS, 1), full),
                  pl.BlockSpec((EXP_PER_GROUP, D_MODEL, D_FF), lambda i, g: (g, 0, 0)),
                  pl.BlockSpec((EXP_PER_GROUP, D_MODEL, D_FF), lambda i, g: (g, 0, 0)),
                  pl.BlockSpec((EXP_PER_GROUP, D_FF, D_MODEL), lambda i, g: (g, 0, 0)),
                  vec, vec],
        out_specs=pl.BlockSpec((tm, D_MODEL), row),
        out_shape=jax.ShapeDtypeStruct((n, D_MODEL), F32),
        scratch_shapes=[pltpu.VMEM((tm, D_MODEL), BF16),
                        pltpu.VMEM((N_GROUPS, tm, LANES), F32),
                        pltpu.VMEM((tm, D_MODEL), F32)],
        compiler_params=_params(("parallel", "arbitrary")),
        name="moe",
    )(x, wr_t, br, wg, wu, wd, ln_g, ln_b)


WPREP_ROWS = 128


def _split_w_in_kernel(w_ref, wn_ref, wt_ref, wu_ref, wv_ref, whg_ref, wg_ref):
    w = w_ref[0]
    offs = [0]
    for n in SPLIT_SIZES:
        offs.append(offs[-1] + n)
    seg = lambda i, j=None: w[:, offs[i]:offs[i + 1 if j is None else j]]
    zeros = lambda c: jnp.zeros((w.shape[0], c), w.dtype)
    wn_ref[0] = jnp.concatenate([seg(1), seg(2), seg(4), zeros(LANES - IDX_DIM)], axis=1).astype(BF16)
    wt = jnp.concatenate([seg(0), seg(3), seg(2), seg(5), zeros(ATT_T_PAD - ATT_T_ROWS + 8 - IDX_HEADS)],
                         axis=1)
    for c in range(ATT_T_PAD // LANES):
        rows = min(LANES, ATT_T_ROWS - c * LANES)
        wt_ref[0, c * LANES:c * LANES + rows, :] = wt[:, c * LANES:(c + 1) * LANES].T[:rows].astype(BF16)
    wu_ref[0] = seg(6).astype(BF16)
    wv_ref[0] = seg(7).astype(BF16)
    whg_ref[0] = seg(8, 12).astype(BF16)
    wg_ref[0] = seg(12).astype(BF16)


def _split_w_in(w_in):
    depth, d, ncol = w_in.shape
    tr = WPREP_ROWS
    cols = (ATT_N_COLS, None, SG_WIDTH, SG_WIDTH, 4 * HG_WIDTH, N_BRANCH * D_MODEL)
    shapes = [(depth, ATT_T_ROWS, d) if c is None else (depth, d, c) for c in cols]
    specs = [pl.BlockSpec((1, ATT_T_ROWS, tr), lambda l, i: (l, 0, i)) if c is None
             else pl.BlockSpec((1, tr, c), lambda l, i: (l, i, 0)) for c in cols]
    return pl.pallas_call(
        _split_w_in_kernel,
        grid=(depth, d // tr),
        in_specs=[pl.BlockSpec((1, tr, ncol), lambda l, i: (l, i, 0))],
        out_specs=specs,
        out_shape=[jax.ShapeDtypeStruct(sh, BF16) for sh in shapes],
        compiler_params=_params(("parallel", "parallel")),
        name="split_w_in",
    )(w_in)


def _row_tile(n, want):
    while n % want:
        want //= 2
    return want


def _layer(x, bsz, t, lw, *, pos0, topk, cache, s0):
    n = bsz * t
    lc = min(t, SG_LEN)
    k, v, ki, kb, kib, qt, qit, vt, wit = _attn_proj(x, lw["w_att_n"], lw["w_att_t"], bsz, t,
                                                     _row_tile(t, 512))
    kb3 = kb.reshape(bsz, t, KV_WIDTH)
    kib3 = kib.reshape(bsz, t, IDX_DIM)
    if cache is not None:
        ck, cvt, cki = cache
        kb3 = jnp.concatenate([ck, kb3], axis=1)
        kib3 = jnp.concatenate([cki, kib3], axis=1)
        vt = jnp.concatenate([cvt, vt], axis=2)
    ltot = kb3.shape[1]
    lp = -(-ltot // DSA_KEY_BLOCK) * DSA_KEY_BLOCK
    tq = LANES
    tp = -(-t // tq) * tq
    padq = lambda a: jnp.pad(a, ((0, 0), (0, 0), (0, tp - t)))
    a = _dsa(padq(qt), padq(qit), padq(wit), jnp.pad(kb3, ((0, 0), (0, lp - ltot), (0, 0))),
             jnp.pad(vt, ((0, 0), (0, 0), (0, lp - ltot))),
             jnp.pad(kib3, ((0, 0), (0, lp - ltot), (0, 0))), tq=tq, pos0=pos0, topk=topk)[:, :t]
    b, v_gm = _gmlp(x, lw["w_u"], lw["w_v"], lw["ln_sg_g"], lw["ln_sg_b"],
                    lw["w_sg"][:, :lc, :lc], lw["b_sg"][:, :lc].T, tm=_row_tile(n, 256), lc=lc)
    c, s_new = _hgrn(x.reshape(bsz, t, D_MODEL), lw["w_hg"], lw["log_lb"], lw["log1m_lb"],
                     lw["om_lb"], lw["hg_norm_g"], s0, tc=min(t, 256))
    x1 = _merge(x, a.reshape(n, ATT_WIDTH), b, c.reshape(n, HG_WIDTH), lw["w_gates"],
                lw["w_branch_a"], lw["w_branch_b"], lw["w_branch_c"], lw["w_out"],
                lw["ln1_g"], lw["ln1_b"], tm=_row_tile(n, 256))
    x2 = _moe(x1, lw["w_router_t"], lw["b_router"], lw["w_exp_gate"], lw["w_exp_up"],
              lw["w_exp_down"], lw["ln2_g"], lw["ln2_b"], tm=_row_tile(n, 512))
    return x2, k, v, ki, s_new, v_gm


def kernel(x_prompt, x_sample, cache_k, cache_v, cache_kidx, state_hgrn, w_in, w_sg, b_sg, ln_sg_g, ln_sg_b, hg_lb_logits, hg_norm_g, w_branch_a, w_branch_b, w_branch_c, w_out, ln1_g, ln1_b, w_router, b_router, w_exp_gate, w_exp_up, w_exp_down, ln2_g, ln2_b):
    bp, sp, _ = x_prompt.shape
    bs, ss, _ = x_sample.shape
    past = cache_k.shape[2]
    topk_p = min(TOPK_MAX, sp // 4)
    topk_s = min(TOPK_MAX, (past + ss) // 4)

    lb_all = jnp.cumsum(jax.nn.softmax(hg_lb_logits.astype(F32), axis=0), axis=0)
    lb_all = lb_all - lb_all[0:1]
    vec = lambda a: a.reshape(1, -1)

    xp = x_prompt.reshape(bp * sp, D_MODEL)
    xs = x_sample.reshape(bs * ss, D_MODEL)
    s0_p = jnp.zeros((bp, HG_HEADS, HG_DK, HG_DV), F32)
    outs_p, outs_s = [], []
    w_att_n, w_att_t, w_u, w_v, w_hg, w_gates = _split_w_in(w_in)
    for l in range(DEPTH):
        lw = dict(
            w_att_n=w_att_n[l], w_att_t=w_att_t[l], w_u=w_u[l], w_v=w_v[l], w_hg=w_hg[l],
            w_gates=w_gates[l],
            ln_sg_g=vec(ln_sg_g[l]), ln_sg_b=vec(ln_sg_b[l]), w_sg=w_sg[l], b_sg=b_sg[l],
            log_lb=vec(jnp.log(lb_all[l])), log1m_lb=vec(jnp.log1p(-lb_all[l])),
            om_lb=vec(1.0 - lb_all[l]), hg_norm_g=vec(hg_norm_g[l].astype(F32)),
            w_branch_a=w_branch_a[l].astype(BF16), w_branch_b=w_branch_b[l].astype(BF16),
            w_branch_c=w_branch_c[l].astype(BF16), w_out=w_out[l].astype(BF16),
            ln1_g=vec(ln1_g[l]), ln1_b=vec(ln1_b[l]),
            w_router_t=w_router.T.astype(BF16), b_router=b_router.astype(F32).reshape(-1, 1),
            w_exp_gate=w_exp_gate[l].astype(BF16), w_exp_up=w_exp_up[l].astype(BF16),
            w_exp_down=w_exp_down[l].astype(BF16),
            ln2_g=vec(ln2_g[l]), ln2_b=vec(ln2_b[l]))
        xp, k, v, ki, s_new, _ = _layer(xp, bp, sp, lw, pos0=0, topk=topk_p, cache=None, s0=s0_p)
        outs_p.append((k.reshape(bp, sp, KV_HEADS, HEAD_DIM), v.reshape(bp, sp, KV_HEADS, HEAD_DIM),
                       ki.reshape(bp, sp, IDX_DIM), s_new))
        cache = (cache_k[l].reshape(bs, past, KV_WIDTH).astype(BF16),
                 jnp.swapaxes(cache_v[l].reshape(bs, past, KV_WIDTH), 1, 2).astype(BF16),
                 cache_kidx[l].astype(BF16))
        xs, k, v, ki, s_new, v_gm = _layer(xs, bs, ss, lw, pos0=past, topk=topk_s, cache=cache,
                                           s0=state_hgrn[l].astype(F32))
        outs_s.append((k.reshape(bs, ss, KV_HEADS, HEAD_DIM), v.reshape(bs, ss, KV_HEADS, HEAD_DIM),
                       ki.reshape(bs, ss, IDX_DIM), s_new, v_gm.reshape(bs, ss, SG_WIDTH)))

    stack = lambda rows, i: jnp.stack([r[i] for r in rows])
    return (xp.reshape(bp, sp, D_MODEL), xs.reshape(bs, ss, D_MODEL),
            stack(outs_p, 0), stack(outs_p, 1), stack(outs_p, 2), stack(outs_p, 3),
            stack(outs_s, 0), stack(outs_s, 1), stack(outs_s, 2), stack(outs_s, 3),
            stack(outs_s, 4))
```

```python
import functools

import jax
import jax.numpy as jnp
from jax import lax
from jax.experimental import pallas as pl
from jax.experimental.pallas import tpu as pltpu

F32 = jnp.float32
BF16 = jnp.bfloat16
I32 = jnp.int32

D_MODEL = 1024
DEPTH = 4
CHUNK = 64
ATT_HEADS = 8
KV_HEADS = 2
HEAD_DIM = 64
ATT_WIDTH = ATT_HEADS * HEAD_DIM
KV_WIDTH = KV_HEADS * HEAD_DIM
IDX_HEADS = 4
IDX_DIM = 64
TOPK_MAX = 256
SG_LEN = 128
SG_GROUPS = 4
SG_WIDTH = 512
SG_GW = SG_WIDTH // SG_GROUPS
HG_HEADS = 4
HG_DK = 128
HG_DV = 128
HG_WIDTH = HG_HEADS * HG_DV
HG_BLOCK = 16
HG_FACTOR_RANGE = 80.0
N_BRANCH = 3
N_EXPERTS = 16
N_GROUPS = 4
EXP_PER_GROUP = 4
D_FF = 256
ALPHA = (2 * DEPTH) ** 0.25
LN_EPS = 1e-5

SPLIT_SIZES = (ATT_WIDTH, KV_WIDTH, KV_WIDTH, IDX_HEADS * IDX_DIM, IDX_DIM, IDX_HEADS,
               SG_WIDTH, SG_WIDTH, HG_HEADS * HG_DK, HG_HEADS * HG_DK, HG_WIDTH, HG_WIDTH,
               N_BRANCH * D_MODEL)

LANES = 128
INT_MIN = -2 ** 31
NEG_BIG = -1e30
LOG2_E = 1.4426950408889634
DSA_KEY_BLOCK = 512
VMEM_LIMIT = 56 * 1024 * 1024

_NT = (((1,), (1,)), ((), ()))
_TN = (((0,), (0,)), ((), ()))


def _params(sem):
    return pltpu.CompilerParams(dimension_semantics=sem, vmem_limit_bytes=VMEM_LIMIT)


def _layer_norm(x, g, b):
    mu = jnp.mean(x, axis=-1, keepdims=True)
    d = x - mu
    var = jnp.mean(d * d, axis=-1, keepdims=True)
    return d * lax.rsqrt(var + LN_EPS) * g + b


def _gelu_tanh(x):
    return 0.5 * x * (1.0 + jnp.tanh(0.7978845608028654 * (x + 0.044715 * (x * x * x))))


def _sigmoid(x):
    return 1.0 / (1.0 + jnp.exp(-x))


def _fold_rows(x, op):
    parts = [x[i:i + 8] for i in range(0, x.shape[0], 8)]
    while len(parts) > 1:
        parts = [op(parts[i], parts[i + 1]) for i in range(0, len(parts), 2)]
    return parts[0]


ATT_T_ROWS = ATT_WIDTH + IDX_HEADS * IDX_DIM + KV_WIDTH + 8
ATT_T_PAD = -(-ATT_T_ROWS // LANES) * LANES
ATT_N_COLS = 3 * LANES


def _attn_proj_kernel(x_ref, wn_ref, wt_ref, k_ref, v_ref, ki_ref, kb_ref, kib_ref,
                      qt_ref, qit_ref, vt_ref, wit_ref):
    xb = x_ref[...].astype(BF16)
    z = jnp.dot(xb, wn_ref[...], preferred_element_type=F32)
    k = z[:, 0:KV_WIDTH]
    ki = z[:, 2 * KV_WIDTH:2 * KV_WIDTH + IDX_DIM]
    k_ref[...] = k
    v_ref[...] = z[:, KV_WIDTH:2 * KV_WIDTH]
    ki_ref[...] = ki
    kb_ref[...] = k.astype(BF16)
    kib_ref[...] = ki.astype(BF16)
    zt = lax.dot_general(wt_ref[...], xb, _NT, preferred_element_type=F32)
    o = 0
    qt_ref[0] = (zt[o:o + ATT_WIDTH] * (HEAD_DIM ** -0.5 * LOG2_E)).astype(BF16)
    o += ATT_WIDTH
    qit_ref[0] = (zt[o:o + IDX_HEADS * IDX_DIM] * IDX_DIM ** -0.5).astype(BF16)
    o += IDX_HEADS * IDX_DIM
    vt_ref[0] = zt[o:o + KV_WIDTH].astype(BF16)
    o += KV_WIDTH
    wit_ref[0] = zt[o:o + 8] * IDX_HEADS ** -0.5


def _attn_proj(x, wn, wt, bsz, t, tm):
    n = bsz * t
    per = t // tm
    row = lambda i: (i, 0)
    tmap = lambda i: (i // per, 0, i % per)
    nat = ((KV_WIDTH, F32), (KV_WIDTH, F32), (IDX_DIM, F32), (KV_WIDTH, BF16), (IDX_DIM, BF16))
    tr = ((ATT_WIDTH, BF16), (IDX_HEADS * IDX_DIM, BF16), (KV_WIDTH, BF16), (8, F32))
    return pl.pallas_call(
        _attn_proj_kernel,
        grid=(n // tm,),
        in_specs=[pl.BlockSpec((tm, D_MODEL), row),
                  pl.BlockSpec((D_MODEL, ATT_N_COLS), lambda i: (0, 0)),
                  pl.BlockSpec((ATT_T_ROWS, D_MODEL), lambda i: (0, 0))],
        out_specs=[pl.BlockSpec((tm, c), row) for c, _ in nat]
                  + [pl.BlockSpec((1, c, tm), tmap) for c, _ in tr],
        out_shape=[jax.ShapeDtypeStruct((n, c), d) for c, d in nat]
                  + [jax.ShapeDtypeStruct((bsz, c, t), d) for c, d in tr],
        compiler_params=_params(("parallel",)),
        name="attn_proj",
    )(x, wn, wt)


def _dsa_kernel(qt_ref, qit_ref, wit_ref, kb_ref, vt_ref, kib_ref, o_ref,
                key_scr, bias_scr, s_scr, acc_scr, m_scr, a_scr,
                *, tq, kblk, pos0, topk):
    j = pl.program_id(1)
    qpos0 = pos0 + j * tq
    qcol = lax.broadcasted_iota(I32, (1, tq), 1)
    chunk_end = ((qpos0 + qcol) // CHUNK + 1) * CHUNK
    lvis = ((qpos0 + tq - 1) // CHUNK + 1) * CHUNK
    nblk = (lvis + kblk - 1) // kblk
    krow = lax.broadcasted_iota(I32, (kblk, tq), 0)

    wi = wit_ref[0]
    qit = qit_ref[0]

    def score_block(i, carry):
        off = pl.multiple_of(i * kblk, kblk)
        kib = kib_ref[0, pl.ds(off, kblk), :]
        sc = jnp.zeros((kblk, tq), F32)
        for h in range(IDX_HEADS):
            raw = jnp.dot(kib, qit[h * IDX_DIM:(h + 1) * IDX_DIM, :], preferred_element_type=F32)
            sc = sc + jnp.maximum(raw, 0.0) * wi[h:h + 1, :]
        bits = pltpu.bitcast(sc, I32)
        key = bits ^ ((bits >> 31) & 0x7FFFFFFF)
        key_scr[i] = jnp.where(off + krow < chunk_end, key, INT_MIN)
        return carry

    lax.fori_loop(0, nblk, score_block, 0)

    def count(hits):
        def body(i, acc):
            return acc + _fold_rows(hits(i, key_scr[i]), jnp.add)
        acc = lax.fori_loop(0, nblk, body, jnp.zeros((8, tq), F32))
        return jnp.sum(acc, axis=0, keepdims=True)

    def key_bit(i, thr):
        cand = thr + jnp.left_shift(jnp.int32(1), 31 - i)
        cnt = count(lambda _, kb: jnp.where(kb >= cand, 1.0, 0.0))
        return jnp.where(cnt >= topk, cand, thr)

    thr = lax.fori_loop(0, 32, key_bit, jnp.full((1, tq), INT_MIN, I32))

    need = topk - count(lambda _, kb: jnp.where(kb > thr, 1.0, 0.0))
    tri = jnp.where(lax.broadcasted_iota(I32, (kblk, kblk), 0) >= lax.broadcasted_iota(I32, (kblk, kblk), 1),
                    1.0, 0.0).astype(BF16)

    def select_block(i, taken):
        off = pl.multiple_of(i * kblk, kblk)
        kb = key_scr[i]
        tied = jnp.where(kb == thr, 1.0, 0.0)
        rank = taken + jnp.dot(tri, tied.astype(BF16), preferred_element_type=F32)
        tie = jnp.where(rank <= need, 0.0, NEG_BIG)
        bias = jnp.where(kb > thr, 0.0, jnp.where(kb == thr, tie, NEG_BIG))
        bias_scr[i] = jnp.where(off + krow < chunk_end, bias, NEG_BIG)
        return rank[kblk - 1:kblk, :]

    lax.fori_loop(0, nblk, select_block, jnp.zeros((1, tq), F32))

    m_scr[...] = jnp.full(m_scr.shape, NEG_BIG, F32)
    acc_scr[...] = jnp.zeros(acc_scr.shape, F32)
    rep = ATT_HEADS // KV_HEADS
    qt = qt_ref[0]
    qg = [jnp.concatenate([qt[(g * rep + r) * HEAD_DIM:(g * rep + r + 1) * HEAD_DIM, :]
                           for r in range(rep)], axis=1) for g in range(KV_HEADS)]
    ones = jnp.ones((16, kblk), BF16)

    def score_phase(i, buf):
        off = pl.multiple_of(i * kblk, kblk)
        bias = jnp.concatenate([bias_scr[i]] * rep, axis=1)
        kk = kb_ref[0, pl.ds(off, kblk), :]
        for g in range(KV_HEADS):
            s = jnp.dot(kk[:, g * HEAD_DIM:(g + 1) * HEAD_DIM], qg[g],
                        preferred_element_type=F32) + bias
            s_scr[buf, g] = s
            m_blk = jnp.max(_fold_rows(s, jnp.maximum), axis=0, keepdims=True)
            m_old = m_scr[g]
            m_new = jnp.maximum(m_old, m_blk)
            a_scr[g] = jnp.exp2(m_old - m_new)
            m_scr[g] = m_new

    def value_phase(i, buf):
        off = pl.multiple_of(i * kblk, kblk)
        vv = vt_ref[0, :, pl.ds(off, kblk)]
        for g in range(KV_HEADS):
            p = jnp.exp2(s_scr[buf, g] - m_scr[g]).astype(BF16)
            vg = jnp.concatenate([vv[g * HEAD_DIM:(g + 1) * HEAD_DIM, :], ones], axis=0)
            acc_scr[g] = a_scr[g] * acc_scr[g] + jnp.dot(vg, p, preferred_element_type=F32)

    score_phase(0, 0)

    def attn_pair(k, carry):
        i = 2 * k
        value_phase(i, 0)
        score_phase(i + 1, 1)
        value_phase(i + 1, 1)
        score_phase(jnp.minimum(i + 2, nblk - 1), 0)
        return carry

    lax.fori_loop(0, nblk // 2, attn_pair, 0)

    @pl.when(nblk % 2 == 1)
    def _():
        value_phase(nblk - 1, 0)

    heads = []
    for g in range(KV_HEADS):
        acc = acc_scr[g]
        og = acc[:HEAD_DIM] / acc[HEAD_DIM:HEAD_DIM + 1]
        heads += [og[:, r * tq:(r + 1) * tq] for r in range(rep)]
    o_ref[0] = jnp.concatenate(heads, axis=0).T.astype(BF16)


def _dsa(qt, qit, wit, kb, vt, kib, *, tq, pos0, topk):
    b, _, t = qt.shape
    lp = kb.shape[1]
    kblk = DSA_KEY_BLOCK
    assert lp % kblk == 0 and t % tq == 0 and tq % LANES == 0
    kern = functools.partial(_dsa_kernel, tq=tq, kblk=kblk, pos0=pos0, topk=topk)
    rep = ATT_HEADS // KV_HEADS
    qmap = lambda bi, j: (bi, 0, j)
    kmap = lambda bi, j: (bi, 0, 0)
    return pl.pallas_call(
        kern,
        grid=(b, t // tq),
        in_specs=[pl.BlockSpec((1, ATT_WIDTH, tq), qmap),
                  pl.BlockSpec((1, IDX_HEADS * IDX_DIM, tq), qmap),
                  pl.BlockSpec((1, 8, tq), qmap),
                  pl.BlockSpec((1, lp, KV_WIDTH), kmap),
                  pl.BlockSpec((1, KV_WIDTH, lp), kmap),
                  pl.BlockSpec((1, lp, IDX_DIM), kmap)],
        out_specs=pl.BlockSpec((1, tq, ATT_WIDTH), lambda bi, j: (bi, j, 0)),
        out_shape=jax.ShapeDtypeStruct((b, t, ATT_WIDTH), BF16),
        scratch_shapes=[pltpu.VMEM((lp // kblk, kblk, tq), I32),
                        pltpu.VMEM((lp // kblk, kblk, tq), F32),
                        pltpu.VMEM((2, KV_HEADS, kblk, rep * tq), F32),
                        pltpu.VMEM((KV_HEADS, HEAD_DIM + 16, rep * tq), F32),
                        pltpu.VMEM((KV_HEADS, 1, rep * tq), F32),
                        pltpu.VMEM((KV_HEADS, 1, rep * tq), F32)],
        compiler_params=_params(("parallel", "arbitrary")),
        name="dsa",
    )(qt, qit, wit, kb, vt, kib)


def _gmlp_kernel(x_ref, wu_ref, wv_ref, g_ref, b_ref, ws_ref, bst_ref, o_ref, v_ref, *, tm, lc):
    xb = x_ref[...].astype(BF16)
    u = _gelu_tanh(jnp.dot(xb, wu_ref[...], preferred_element_type=F32))
    v = _layer_norm(_gelu_tanh(jnp.dot(xb, wv_ref[...], preferred_element_type=F32)),
                    g_ref[...], b_ref[...])
    v_ref[...] = v
    vb = v.astype(BF16)
    r = lax.broadcasted_iota(I32, (lc, lc), 0)
    c = lax.broadcasted_iota(I32, (lc, lc), 1)
    for g in range(SG_GROUPS):
        w = jnp.where(r >= c, ws_ref[g], 0.0).astype(BF16)
        bias = bst_ref[:, g:g + 1]
        cs = slice(g * SG_GW, (g + 1) * SG_GW)
        for n in range(tm // lc):
            rs = slice(n * lc, (n + 1) * lc)
            mixed = jnp.dot(w, vb[rs, cs], preferred_element_type=F32) + bias
            o_ref[rs, cs] = (u[rs, cs] * mixed).astype(BF16)


def _gmlp(x, wu, wv, ln_g, ln_b, ws, bst, *, tm, lc):
    n = x.shape[0]
    row = lambda i: (i, 0)
    full2 = lambda i: (0, 0)
    return pl.pallas_call(
        functools.partial(_gmlp_kernel, tm=tm, lc=lc),
        grid=(n // tm,),
        in_specs=[pl.BlockSpec((tm, D_MODEL), row),
                  pl.BlockSpec((D_MODEL, SG_WIDTH), full2),
                  pl.BlockSpec((D_MODEL, SG_WIDTH), full2),
                  pl.BlockSpec((1, SG_WIDTH), full2),
                  pl.BlockSpec((1, SG_WIDTH), full2),
                  pl.BlockSpec((SG_GROUPS, lc, lc), lambda i: (0, 0, 0)),
                  pl.BlockSpec((lc, SG_GROUPS), full2)],
        out_specs=[pl.BlockSpec((tm, SG_WIDTH), row), pl.BlockSpec((tm, SG_WIDTH), row)],
        out_shape=[jax.ShapeDtypeStruct((n, SG_WIDTH), BF16),
                   jax.ShapeDtypeStruct((n, SG_WIDTH), F32)],
        compiler_params=_params(("parallel",)),
        name="gmlp",
    )(x, wu, wv, ln_g, ln_b, ws, bst)


def _hgrn_kernel(x_ref, w_ref, loglb_ref, log1mlb_ref, omlb_ref, ng_ref, s0_ref, c_ref, sout_ref,
                 q_scr, k_scr, v_scr, lf_scr, o_scr, st_scr, *, tc):
    t = pl.program_id(1)

    @pl.when(t == 0)
    def _():
        for h in range(HG_HEADS):
            st_scr[h] = s0_ref[0, h].T

    xb = x_ref[0].astype(BF16)
    z = jnp.dot(xb, w_ref[...], preferred_element_type=F32)
    hq = z[:, 0:HG_WIDTH]
    hf = z[:, HG_WIDTH:2 * HG_WIDTH]
    hg = z[:, 3 * HG_WIDTH:4 * HG_WIDTH]
    q_scr[...] = hq * _sigmoid(hq) * HG_DK ** -0.5
    log_sig = jnp.minimum(hf, 0.0) - jnp.log1p(jnp.exp(-jnp.abs(hf)))
    y = log1mlb_ref[...] + log_sig
    a = loglb_ref[...]
    lf_scr[...] = jnp.maximum(a, y) + jnp.log1p(jnp.exp(-jnp.abs(a - y)))
    k_scr[...] = omlb_ref[...] * _sigmoid(-hf)
    v_scr[...] = z[:, 2 * HG_WIDTH:3 * HG_WIDTH]

    ch = min(tc, CHUNK)
    ri = lax.broadcasted_iota(I32, (tc, tc), 0)
    ci = lax.broadcasted_iota(I32, (tc, tc), 1)
    same = (ri // ch) == (ci // ch)
    causal = same & (ri >= ci)
    lf = lf_scr[...]
    lf_hi = lf.astype(BF16)
    lf_r = lf - lf_hi.astype(F32)
    lf_mid = lf_r.astype(BF16)
    lf_lo = (lf_r - lf_mid.astype(F32)).astype(BF16)
    cmask = jnp.where(causal, 1.0, 0.0).astype(BF16)
    bcum_all = (jnp.dot(cmask, lf_hi, preferred_element_type=F32)
                + jnp.dot(cmask, lf_mid, preferred_element_type=F32)
                + jnp.dot(cmask, lf_lo, preferred_element_type=F32))
    factored = jnp.min(bcum_all) > -HG_FACTOR_RANGE

    @pl.when(factored)
    def _():
        btot = jnp.concatenate(
            [jnp.broadcast_to(bcum_all[(n + 1) * ch - 1:(n + 1) * ch, :], (ch, HG_WIDTH))
             for n in range(tc // ch)], axis=0)
        qa = q_scr[...]
        ka = k_scr[...]
        qd = (qa * jnp.exp(bcum_all)).astype(BF16)
        ki = (ka * jnp.exp(-bcum_all)).astype(BF16)
        kd = (ka * jnp.exp(btot - bcum_all)).astype(BF16)
        va = v_scr[...].astype(BF16)
        for h in range(HG_HEADS):
            cs = slice(h * HG_DK, (h + 1) * HG_DK)
            att = lax.dot_general(qd[:, cs], ki[:, cs], _NT, preferred_element_type=F32)
            att = jnp.where(causal, att, 0.0).astype(BF16)
            o = jnp.dot(att, va[:, cs], preferred_element_type=F32)
            st = st_scr[h]
            for n in range(tc // ch):
                rs = slice(n * ch, (n + 1) * ch)
                o_scr[rs, cs] = o[rs] + lax.dot_general(qd[rs, cs], st.astype(BF16), _NT,
                                                        preferred_element_type=F32)
                upd = lax.dot_general(va[rs, cs], kd[rs, cs], _TN, preferred_element_type=F32)
                st = jnp.exp(btot[n * ch:n * ch + 1, cs]) * st + upd
            st_scr[h] = st

    nb = HG_BLOCK
    tri = jnp.where(lax.broadcasted_iota(I32, (nb, nb), 0) >= lax.broadcasted_iota(I32, (nb, nb), 1),
                    1.0, 0.0).astype(F32)
    trow = lax.broadcasted_iota(I32, (nb, 1), 0)

    def block(i, carry):
        r0 = pl.multiple_of(i * nb, nb)
        rows = pl.ds(r0, nb)
        for h in range(HG_HEADS):
            cs = slice(h * HG_DK, (h + 1) * HG_DK)
            bcum = jnp.dot(tri, lf_scr[rows, cs], preferred_element_type=F32,
                           precision=lax.Precision.HIGHEST)
            qb = q_scr[rows, cs]
            kb = k_scr[rows, cs]
            vb = v_scr[rows, cs]
            o = jnp.zeros((nb, HG_DV), F32)
            for s in range(nb):
                e = jnp.exp(jnp.where(trow >= s, bcum - bcum[s:s + 1, :], -jnp.inf))
                a_ts = jnp.sum(qb * kb[s:s + 1, :] * e, axis=1, keepdims=True)
                o = o + a_ts * vb[s:s + 1, :]
            st = st_scr[h]
            qd = (qb * jnp.exp(bcum)).astype(BF16)
            o = o + lax.dot_general(qd, st.astype(BF16), _NT, preferred_element_type=F32)
            blast = bcum[nb - 1:nb, :]
            kd = (kb * jnp.exp(blast - bcum)).astype(BF16)
            upd = lax.dot_general(vb.astype(BF16), kd, _TN, preferred_element_type=F32)
            st_scr[h] = jnp.exp(blast) * st + upd
            o_scr[rows, cs] = o
        return carry

    @pl.when(jnp.logical_not(factored))
    def _():
        lax.fori_loop(0, tc // nb, block, 0)

    for h in range(HG_HEADS):
        cs = slice(h * HG_DV, (h + 1) * HG_DV)
        o = o_scr[:, cs]
        o = o * lax.rsqrt(jnp.mean(o * o, axis=-1, keepdims=True) + LN_EPS)
        g = hg[:, cs]
        c_ref[0, :, cs] = (o * ng_ref[:, cs] * (g * _sigmoid(g))).astype(BF16)

    @pl.when(t == pl.num_programs(1) - 1)
    def _():
        for h in range(HG_HEADS):
            sout_ref[0, h] = st_scr[h].T


def _hgrn(x, w, loglb, log1mlb, omlb, ng, s0, *, tc):
    b, t, _ = x.shape
    xmap = lambda bi, j: (bi, j, 0)
    vec = pl.BlockSpec((1, HG_WIDTH), lambda bi, j: (0, 0))
    smap = lambda bi, j: (bi, 0, 0, 0)
    return pl.pallas_call(
        functools.partial(_hgrn_kernel, tc=tc),
        grid=(b, t // tc),
        in_specs=[pl.BlockSpec((1, tc, D_MODEL), xmap),
                  pl.BlockSpec((D_MODEL, 4 * HG_WIDTH), lambda bi, j: (0, 0)),
                  vec, vec, vec, vec,
                  pl.BlockSpec((1, HG_HEADS, HG_DK, HG_DV), smap)],
        out_specs=[pl.BlockSpec((1, tc, HG_WIDTH), xmap),
                   pl.BlockSpec((1, HG_HEADS, HG_DK, HG_DV), smap)],
        out_shape=[jax.ShapeDtypeStruct((b, t, HG_WIDTH), BF16),
                   jax.ShapeDtypeStruct((b, HG_HEADS, HG_DK, HG_DV), F32)],
        scratch_shapes=[pltpu.VMEM((tc, HG_WIDTH), F32)] * 5
                       + [pltpu.VMEM((HG_HEADS, HG_DV, HG_DK), F32)],
        compiler_params=_params(("parallel", "arbitrary")),
        name="hgrn",
    )(x, w, loglb, log1mlb, omlb, ng, s0)


def _merge_kernel(x_ref, a_ref, b_ref, c_ref, wg_ref, wa_ref, wb_ref, wc_ref, wo_ref,
                  g_ref, bt_ref, y_ref):
    x = x_ref[...]
    xb = x.astype(BF16)
    m = None
    for i, (br, w) in enumerate(((a_ref, wa_ref), (b_ref, wb_ref), (c_ref, wc_ref))):
        gate = _sigmoid(jnp.dot(xb, wg_ref[:, i * D_MODEL:(i + 1) * D_MODEL],
                                preferred_element_type=F32))
        term = gate * jnp.dot(br[...], w[...], preferred_element_type=F32)
        m = term if m is None else m + term
    mo = jnp.dot(m.astype(BF16), wo_ref[...], preferred_element_type=F32)
    y_ref[...] = _layer_norm(ALPHA * x + mo, g_ref[...], bt_ref[...])


def _merge(x, a, b, c, wg, wa, wb, wc, wo, ln_g, ln_b, *, tm):
    n = x.shape[0]
    row = lambda i: (i, 0)
    full = lambda i: (0, 0)
    br = pl.BlockSpec((tm, ATT_WIDTH), row)
    bw = pl.BlockSpec((ATT_WIDTH, D_MODEL), full)
    vec = pl.BlockSpec((1, D_MODEL), full)
    return pl.pallas_call(
        _merge_kernel,
        grid=(n // tm,),
        in_specs=[pl.BlockSpec((tm, D_MODEL), row), br, br, br,
                  pl.BlockSpec((D_MODEL, N_BRANCH * D_MODEL), full), bw, bw, bw,
                  pl.BlockSpec((D_MODEL, D_MODEL), full), vec, vec],
        out_specs=pl.BlockSpec((tm, D_MODEL), row),
        out_shape=jax.ShapeDtypeStruct((n, D_MODEL), F32),
        compiler_params=_params(("parallel",)),
        name="merge",
    )(x, a, b, c, wg, wa, wb, wc, wo, ln_g, ln_b)


def _route(logits):
    ex = jnp.exp(logits - jnp.max(logits, axis=0, keepdims=True))
    probs = ex / jnp.sum(ex, axis=0, keepdims=True)
    p = [[probs[g * EXP_PER_GROUP + k:g * EXP_PER_GROUP + k + 1, :] for k in range(EXP_PER_GROUP)]
         for g in range(N_GROUPS)]
    score = []
    for g in range(N_GROUPS):
        best = None
        for k1 in range(EXP_PER_GROUP):
            for k2 in range(k1 + 1, EXP_PER_GROUP):
                pair = p[g][k1] + p[g][k2]
                best = pair if best is None else jnp.maximum(best, pair)
        score.append(best)
    gsel = jnp.zeros(score[0].shape, I32)
    top = score[0]
    for g in range(1, N_GROUPS):
        better = score[g] > top
        top = jnp.where(better, score[g], top)
        gsel = jnp.where(better, g, gsel)
    val = []
    for k in range(EXP_PER_GROUP):
        v = p[0][k]
        for g in range(1, N_GROUPS):
            v = jnp.where(gsel == g, p[g][k], v)
        val.append(v)
    v1, i1 = val[0], jnp.zeros(gsel.shape, I32)
    for k in range(1, EXP_PER_GROUP):
        better = val[k] > v1
        v1 = jnp.where(better, val[k], v1)
        i1 = jnp.where(better, k, i1)
    v2, i2 = jnp.full(v1.shape, -1.0, F32), jnp.zeros(gsel.shape, I32)
    for k in range(EXP_PER_GROUP):
        better = (i1 != k) & (val[k] > v2)
        v2 = jnp.where(better, val[k], v2)
        i2 = jnp.where(better, k, i2)
    den = v1 + v2
    w1, w2 = v1 / den, v2 / den
    out = []
    for g in range(N_GROUPS):
        rows = [jnp.where(gsel == g, jnp.where(i1 == k, w1, jnp.where(i2 == k, w2, 0.0)), 0.0)
                for k in range(EXP_PER_GROUP)]
        out.append(jnp.concatenate(rows, axis=0))
    return out


def _moe_kernel(x_ref, wr_ref, br_ref, wg_ref, wu_ref, wd_ref, g_ref, bt_ref, y_ref,
                xb_scr, gate_scr, acc_scr, *, tm):
    grp = pl.program_id(1)

    @pl.when(grp == 0)
    def _():
        xb = x_ref[...].astype(BF16)
        xb_scr[...] = xb
        logits = lax.dot_general(wr_ref[...], xb, _NT, preferred_element_type=F32) + br_ref[...]
        gates = _route(logits)
        pad = jnp.zeros((LANES - EXP_PER_GROUP, tm), F32)
        for g in range(N_GROUPS):
            gate_scr[g] = jnp.concatenate([gates[g], pad], axis=0).T
        acc_scr[...] = jnp.zeros(acc_scr.shape, F32)

    xb = xb_scr[...]
    gate = gate_scr[grp]
    acc = acc_scr[...]
    for k in range(EXP_PER_GROUP):
        h = jnp.dot(xb, wg_ref[k], preferred_element_type=F32)
        up = jnp.dot(xb, wu_ref[k], preferred_element_type=F32)
        act = h * _sigmoid(h) * up * gate[:, k:k + 1]
        acc = acc + jnp.dot(act.astype(BF16), wd_ref[k], preferred_element_type=F32)
    acc_scr[...] = acc

    @pl.when(grp == N_GROUPS - 1)
    def _():
        y_ref[...] = _layer_norm(ALPHA * x_ref[...] + acc, g_ref[...], bt_ref[...])


def _moe(x, wr_t, br, wg, wu, wd, ln_g, ln_b, *, tm):
    n = x.shape[0]
    row = lambda i, g: (i, 0)
    full = lambda i, g: (0, 0)
    vec = pl.BlockSpec((1, D_MODEL), full)
    return pl.pallas_call(
        functools.partial(_moe_kernel, tm=tm),
        grid=(n // tm, N_GROUPS),
        in_specs=[pl.BlockSpec((tm, D_MODEL), row),
                  pl.BlockSpec((N_EXPERTS, D_MODEL), full),
                  pl.BlockSpec((N_EXPERTS, 1), full),
                  pl.BlockSpec((EXP_PER_GROUP, D_MODEL, D_FF), lambda i, g: (g, 0, 0)),
                  pl.BlockSpec((EXP_PER_GROUP, D_MODEL, D_FF), lambda i, g: (g, 0, 0)),
                  pl.BlockSpec((EXP_PER_GROUP, D_FF, D_MODEL), lambda i, g: (g, 0, 0)),
                  vec, vec],
        out_specs=pl.BlockSpec((tm, D_MODEL), row),
        out_shape=jax.ShapeDtypeStruct((n, D_MODEL), F32),
        scratch_shapes=[pltpu.VMEM((tm, D_MODEL), BF16),
                        pltpu.VMEM((N_GROUPS, tm, LANES), F32),
                        pltpu.VMEM((tm, D_MODEL), F32)],
        compiler_params=_params(("parallel", "arbitrary")),
        name="moe",
    )(x, wr_t, br, wg, wu, wd, ln_g, ln_b)


WPREP_ROWS = 128


def _split_w_in_kernel(w_ref, wn_ref, wt_ref, wu_ref, wv_ref, whg_ref, wg_ref):
    w = w_ref[0]
    offs = [0]
    for n in SPLIT_SIZES:
        offs.append(offs[-1] + n)
    seg = lambda i, j=None: w[:, offs[i]:offs[i + 1 if j is None else j]]
    zeros = lambda c: jnp.zeros((w.shape[0], c), w.dtype)
    wn_ref[0] = jnp.concatenate([seg(1), seg(2), seg(4), zeros(LANES - IDX_DIM)], axis=1).astype(BF16)
    wt = jnp.concatenate([seg(0), seg(3), seg(2), seg(5), zeros(ATT_T_PAD - ATT_T_ROWS + 8 - IDX_HEADS)],
                         axis=1)
    for c in range(ATT_T_PAD // LANES):
        rows = min(LANES, ATT_T_ROWS - c * LANES)
        wt_ref[0, c * LANES:c * LANES + rows, :] = wt[:, c * LANES:(c + 1) * LANES].T[:rows].astype(BF16)
    wu_ref[0] = seg(6).astype(BF16)
    wv_ref[0] = seg(7).astype(BF16)
    whg_ref[0] = seg(8, 12).astype(BF16)
    wg_ref[0] = seg(12).astype(BF16)


def _split_w_in(w_in):
    depth, d, ncol = w_in.shape
    tr = WPREP_ROWS
    cols = (ATT_N_COLS, None, SG_WIDTH, SG_WIDTH, 4 * HG_WIDTH, N_BRANCH * D_MODEL)
    shapes = [(depth, ATT_T_ROWS, d) if c is None else (depth, d, c) for c in cols]
    specs = [pl.BlockSpec((1, ATT_T_ROWS, tr), lambda l, i: (l, 0, i)) if c is None
             else pl.BlockSpec((1, tr, c), lambda l, i: (l, i, 0)) for c in cols]
    return pl.pallas_call(
        _split_w_in_kernel,
        grid=(depth, d // tr),
        in_specs=[pl.BlockSpec((1, tr, ncol), lambda l, i: (l, i, 0))],
        out_specs=specs,
        out_shape=[jax.ShapeDtypeStruct(sh, BF16) for sh in shapes],
        compiler_params=_params(("parallel", "parallel")),
        name="split_w_in",
    )(w_in)


def _row_tile(n, want):
    while n % want:
        want //= 2
    return want


def _layer(x, bsz, t, lw, *, pos0, topk, cache, s0):
    n = bsz * t
    lc = min(t, SG_LEN)
    k, v, ki, kb, kib, qt, qit, vt, wit = _attn_proj(x, lw["w_att_n"], lw["w_att_t"], bsz, t,
                                                     _row_tile(t, 512))
    kb3 = kb.reshape(bsz, t, KV_WIDTH)
    kib3 = kib.reshape(bsz, t, IDX_DIM)
    if cache is not None:
        ck, cvt, cki = cache
        kb3 = jnp.concatenate([ck, kb3], axis=1)
        kib3 = jnp.concatenate([cki, kib3], axis=1)
        vt = jnp.concatenate([cvt, vt], axis=2)
    ltot = kb3.shape[1]
    lp = -(-ltot // DSA_KEY_BLOCK) * DSA_KEY_BLOCK
    tq = LANES
    tp = -(-t // tq) * tq
    padq = lambda a: jnp.pad(a, ((0, 0), (0, 0), (0, tp - t)))
    a = _dsa(padq(qt), padq(qit), padq(wit), jnp.pad(kb3, ((0, 0), (0, lp - ltot), (0, 0))),
             jnp.pad(vt, ((0, 0), (0, 0), (0, lp - ltot))),
             jnp.pad(kib3, ((0, 0), (0, lp - ltot), (0, 0))), tq=tq, pos0=pos0, topk=topk)[:, :t]
    b, v_gm = _gmlp(x, lw["w_u"], lw["w_v"], lw["ln_sg_g"], lw["ln_sg_b"],
                    lw["w_sg"][:, :lc, :lc], lw["b_sg"][:, :lc].T, tm=_row_tile(n, 256), lc=lc)
    c, s_new = _hgrn(x.reshape(bsz, t, D_MODEL), lw["w_hg"], lw["log_lb"], lw["log1m_lb"],
                     lw["om_lb"], lw["hg_norm_g"], s0, tc=min(t, 256))
    x1 = _merge(x, a.reshape(n, ATT_WIDTH), b, c.reshape(n, HG_WIDTH), lw["w_gates"],
                lw["w_branch_a"], lw["w_branch_b"], lw["w_branch_c"], lw["w_out"],
                lw["ln1_g"], lw["ln1_b"], tm=_row_tile(n, 512))
    x2 = _moe(x1, lw["w_router_t"], lw["b_router"], lw["w_exp_gate"], lw["w_exp_up"],
              lw["w_exp_down"], lw["ln2_g"], lw["ln2_b"], tm=_row_tile(n, 1024))
    return x2, k, v, ki, s_new, v_gm


def kernel(x_prompt, x_sample, cache_k, cache_v, cache_kidx, state_hgrn, w_in, w_sg, b_sg, ln_sg_g, ln_sg_b, hg_lb_logits, hg_norm_g, w_branch_a, w_branch_b, w_branch_c, w_out, ln1_g, ln1_b, w_router, b_router, w_exp_gate, w_exp_up, w_exp_down, ln2_g, ln2_b):
    bp, sp, _ = x_prompt.shape
    bs, ss, _ = x_sample.shape
    past = cache_k.shape[2]
    topk_p = min(TOPK_MAX, sp // 4)
    topk_s = min(TOPK_MAX, (past + ss) // 4)

    lb_all = jnp.cumsum(jax.nn.softmax(hg_lb_logits.astype(F32), axis=0), axis=0)
    lb_all = lb_all - lb_all[0:1]
    vec = lambda a: a.reshape(1, -1)

    xp = x_prompt.reshape(bp * sp, D_MODEL)
    xs = x_sample.reshape(bs * ss, D_MODEL)
    s0_p = jnp.zeros((bp, HG_HEADS, HG_DK, HG_DV), F32)
    outs_p, outs_s = [], []
    w_att_n, w_att_t, w_u, w_v, w_hg, w_gates = _split_w_in(w_in)
    for l in range(DEPTH):
        lw = dict(
            w_att_n=w_att_n[l], w_att_t=w_att_t[l], w_u=w_u[l], w_v=w_v[l], w_hg=w_hg[l],
            w_gates=w_gates[l],
            ln_sg_g=vec(ln_sg_g[l]), ln_sg_b=vec(ln_sg_b[l]), w_sg=w_sg[l], b_sg=b_sg[l],
            log_lb=vec(jnp.log(lb_all[l])), log1m_lb=vec(jnp.log1p(-lb_all[l])),
            om_lb=vec(1.0 - lb_all[l]), hg_norm_g=vec(hg_norm_g[l].astype(F32)),
            w_branch_a=w_branch_a[l].astype(BF16), w_branch_b=w_branch_b[l].astype(BF16),
            w_branch_c=w_branch_c[l].astype(BF16), w_out=w_out[l].astype(BF16),
            ln1_g=vec(ln1_g[l]), ln1_b=vec(ln1_b[l]),
            w_router_t=w_router.T.astype(BF16), b_router=b_router.astype(F32).reshape(-1, 1),
            w_exp_gate=w_exp_gate[l].astype(BF16), w_exp_up=w_exp_up[l].astype(BF16),
            w_exp_down=w_exp_down[l].astype(BF16),
            ln2_g=vec(ln2_g[l]), ln2_b=vec(ln2_b[l]))
        xp, k, v, ki, s_new, _ = _layer(xp, bp, sp, lw, pos0=0, topk=topk_p, cache=None, s0=s0_p)
        outs_p.append((k.reshape(bp, sp, KV_HEADS, HEAD_DIM), v.reshape(bp, sp, KV_HEADS, HEAD_DIM),
                       ki.reshape(bp, sp, IDX_DIM), s_new))
        cache = (cache_k[l].reshape(bs, past, KV_WIDTH).astype(BF16),
                 jnp.swapaxes(cache_v[l].reshape(bs, past, KV_WIDTH), 1, 2).astype(BF16),
                 cache_kidx[l].astype(BF16))
        xs, k, v, ki, s_new, v_gm = _layer(xs, bs, ss, lw, pos0=past, topk=topk_s, cache=cache,
                                           s0=state_hgrn[l].astype(F32))
        outs_s.append((k.reshape(bs, ss, KV_HEADS, HEAD_DIM), v.reshape(bs, ss, KV_HEADS, HEAD_DIM),
                       ki.reshape(bs, ss, IDX_DIM), s_new, v_gm.reshape(bs, ss, SG_WIDTH)))

    stack = lambda rows, i: jnp.stack([r[i] for r in rows])
    return (xp.reshape(bp, sp, D_MODEL), xs.reshape(bs, ss, D_MODEL),
            stack(outs_p, 0), stack(outs_p, 1), stack(outs_p, 2), stack(outs_p, 3),
            stack(outs_s, 0), stack(outs_s, 1), stack(outs_s, 2), stack(outs_s, 3),
            stack(outs_s, 4))
```

```python
import functools

import jax
import jax.numpy as jnp
from jax import lax
from jax.experimental import pallas as pl
from jax.experimental.pallas import tpu as pltpu

F32 = jnp.float32
BF16 = jnp.bfloat16
I32 = jnp.int32

D_MODEL = 1024
DEPTH = 4
CHUNK = 64
ATT_HEADS = 8
KV_HEADS = 2
HEAD_DIM = 64
ATT_WIDTH = ATT_HEADS * HEAD_DIM
KV_WIDTH = KV_HEADS * HEAD_DIM
IDX_HEADS = 4
IDX_DIM = 64
TOPK_MAX = 256
SG_LEN = 128
SG_GROUPS = 4
SG_WIDTH = 512
SG_GW = SG_WIDTH // SG_GROUPS
HG_HEADS = 4
HG_DK = 128
HG_DV = 128
HG_WIDTH = HG_HEADS * HG_DV
HG_BLOCK = 16
HG_FACTOR_RANGE = 80.0
N_BRANCH = 3
N_EXPERTS = 16
N_GROUPS = 4
EXP_PER_GROUP = 4
D_FF = 256
ALPHA = (2 * DEPTH) ** 0.25
LN_EPS = 1e-5

SPLIT_SIZES = (ATT_WIDTH, KV_WIDTH, KV_WIDTH, IDX_HEADS * IDX_DIM, IDX_DIM, IDX_HEADS,
               SG_WIDTH, SG_WIDTH, HG_HEADS * HG_DK, HG_HEADS * HG_DK, HG_WIDTH, HG_WIDTH,
               N_BRANCH * D_MODEL)

LANES = 128
INT_MIN = -2 ** 31
NEG_BIG = -1e30
LOG2_E = 1.4426950408889634
DSA_KEY_BLOCK = 512
VMEM_LIMIT = 56 * 1024 * 1024

_NT = (((1,), (1,)), ((), ()))
_TN = (((0,), (0,)), ((), ()))


def _params(sem):
    return pltpu.CompilerParams(dimension_semantics=sem, vmem_limit_bytes=VMEM_LIMIT)


def _layer_norm(x, g, b):
    mu = jnp.mean(x, axis=-1, keepdims=True)
    d = x - mu
    var = jnp.mean(d * d, axis=-1, keepdims=True)
    return d * lax.rsqrt(var + LN_EPS) * g + b


def _gelu_tanh(x):
    return 0.5 * x * (1.0 + jnp.tanh(0.7978845608028654 * (x + 0.044715 * (x * x * x))))


def _sigmoid(x):
    return 1.0 / (1.0 + jnp.exp(-x))


def _fold_rows(x, op):
    parts = [x[i:i + 8] for i in range(0, x.shape[0], 8)]
    while len(parts) > 1:
        parts = [op(parts[i], parts[i + 1]) for i in range(0, len(parts), 2)]
    return parts[0]


ATT_T_ROWS = ATT_WIDTH + IDX_HEADS * IDX_DIM + KV_WIDTH + 8
ATT_T_PAD = -(-ATT_T_ROWS // LANES) * LANES
ATT_N_COLS = 3 * LANES


def _attn_proj_kernel(x_ref, wn_ref, wt_ref, k_ref, v_ref, ki_ref, kb_ref, kib_ref,
                      qt_ref, qit_ref, vt_ref, wit_ref):
    xb = x_ref[...].astype(BF16)
    z = jnp.dot(xb, wn_ref[...], preferred_element_type=F32)
    k = z[:, 0:KV_WIDTH]
    ki = z[:, 2 * KV_WIDTH:2 * KV_WIDTH + IDX_DIM]
    k_ref[...] = k
    v_ref[...] = z[:, KV_WIDTH:2 * KV_WIDTH]
    ki_ref[...] = ki
    kb_ref[...] = k.astype(BF16)
    kib_ref[...] = ki.astype(BF16)
    zt = lax.dot_general(wt_ref[...], xb, _NT, preferred_element_type=F32)
    o = 0
    qt_ref[0] = (zt[o:o + ATT_WIDTH] * (HEAD_DIM ** -0.5 * LOG2_E)).astype(BF16)
    o += ATT_WIDTH
    qit_ref[0] = (zt[o:o + IDX_HEADS * IDX_DIM] * IDX_DIM ** -0.5).astype(BF16)
    o += IDX_HEADS * IDX_DIM
    vt_ref[0] = zt[o:o + KV_WIDTH].astype(BF16)
    o += KV_WIDTH
    wit_ref[0] = zt[o:o + 8] * IDX_HEADS ** -0.5


def _attn_proj(x, wn, wt, bsz, t, tm):
    n = bsz * t
    per = t // tm
    row = lambda i: (i, 0)
    tmap = lambda i: (i // per, 0, i % per)
    nat = ((KV_WIDTH, F32), (KV_WIDTH, F32), (IDX_DIM, F32), (KV_WIDTH, BF16), (IDX_DIM, BF16))
    tr = ((ATT_WIDTH, BF16), (IDX_HEADS * IDX_DIM, BF16), (KV_WIDTH, BF16), (8, F32))
    return pl.pallas_call(
        _attn_proj_kernel,
        grid=(n // tm,),
        in_specs=[pl.BlockSpec((tm, D_MODEL), row),
                  pl.BlockSpec((D_MODEL, ATT_N_COLS), lambda i: (0, 0)),
                  pl.BlockSpec((ATT_T_ROWS, D_MODEL), lambda i: (0, 0))],
        out_specs=[pl.BlockSpec((tm, c), row) for c, _ in nat]
                  + [pl.BlockSpec((1, c, tm), tmap) for c, _ in tr],
        out_shape=[jax.ShapeDtypeStruct((n, c), d) for c, d in nat]
                  + [jax.ShapeDtypeStruct((bsz, c, t), d) for c, d in tr],
        compiler_params=_params(("parallel",)),
        name="attn_proj",
    )(x, wn, wt)


def _dsa_kernel(qt_ref, qit_ref, wit_ref, kb_ref, vt_ref, kib_ref, o_ref,
                key_scr, sort_scr, bias_scr, s_scr, acc_scr, m_scr, a_scr,
                *, tq, kblk, pos0, topk):
    j = pl.program_id(1)
    qpos0 = pos0 + j * tq
    qcol = lax.broadcasted_iota(I32, (1, tq), 1)
    chunk_end = ((qpos0 + qcol) // CHUNK + 1) * CHUNK
    lvis = ((qpos0 + tq - 1) // CHUNK + 1) * CHUNK
    nblk = (lvis + kblk - 1) // kblk
    krow = lax.broadcasted_iota(I32, (kblk, tq), 0)

    wi = wit_ref[0]
    qit = qit_ref[0]

    def score_block(i, carry):
        off = pl.multiple_of(i * kblk, kblk)
        kib = kib_ref[0, pl.ds(off, kblk), :]
        sc = jnp.zeros((kblk, tq), F32)
        for h in range(IDX_HEADS):
            raw = jnp.dot(kib, qit[h * IDX_DIM:(h + 1) * IDX_DIM, :], preferred_element_type=F32)
            sc = sc + jnp.maximum(raw, 0.0) * wi[h:h + 1, :]
        bits = pltpu.bitcast(sc, I32)
        key = bits ^ ((bits >> 31) & 0x7FFFFFFF)
        key = jnp.where(off + krow < chunk_end, key, INT_MIN)
        key_scr[i] = key
        for g in range(kblk // 32):
            a, b, c, d = (key[g * 32 + 8 * r:g * 32 + 8 * r + 8] for r in range(4))
            a, b = jnp.maximum(a, b), jnp.minimum(a, b)
            c, d = jnp.maximum(c, d), jnp.minimum(c, d)
            a, c = jnp.maximum(a, c), jnp.minimum(a, c)
            b, d = jnp.maximum(b, d), jnp.minimum(b, d)
            b, c = jnp.maximum(b, c), jnp.minimum(b, c)
            sort_scr[i, g * 32:(g + 1) * 32] = jnp.concatenate([a, b, c, d], axis=0)
        return carry

    lax.fori_loop(0, nblk, score_block, 0)

    def count(hits):
        def body(i, acc):
            return acc + _fold_rows(hits(i, key_scr[i]), jnp.add)
        acc = lax.fori_loop(0, nblk, body, jnp.zeros((8, tq), F32))
        return jnp.sum(acc, axis=0, keepdims=True)

    def key_bit(i, thr):
        cand = thr + jnp.left_shift(jnp.int32(1), 31 - i)

        def body(b, accs):
            ks = sort_scr[b]
            accs = list(accs)
            for g in range(kblk // 32):
                s0, s1, s2, s3 = (ks[g * 32 + 8 * r:g * 32 + 8 * r + 8] for r in range(4))
                n = jnp.where(s3 >= cand, 4.0, jnp.where(s2 >= cand, 3.0, jnp.where(
                    s1 >= cand, 2.0, jnp.where(s0 >= cand, 1.0, 0.0))))
                accs[g % 4] = accs[g % 4] + n
            return tuple(accs)

        accs = lax.fori_loop(0, nblk, body, tuple(jnp.zeros((8, tq), F32) for _ in range(4)))
        cnt = jnp.sum((accs[0] + accs[1]) + (accs[2] + accs[3]), axis=0, keepdims=True)
        return jnp.where(cnt >= topk, cand, thr)

    thr = lax.fori_loop(0, 32, key_bit, jnp.full((1, tq), INT_MIN, I32))

    need = topk - count(lambda _, kb: jnp.where(kb > thr, 1.0, 0.0))
    tri = jnp.where(lax.broadcasted_iota(I32, (kblk, kblk), 0) >= lax.broadcasted_iota(I32, (kblk, kblk), 1),
                    1.0, 0.0).astype(BF16)

    def select_block(i, taken):
        off = pl.multiple_of(i * kblk, kblk)
        kb = key_scr[i]
        tied = jnp.where(kb == thr, 1.0, 0.0)
        rank = taken + jnp.dot(tri, tied.astype(BF16), preferred_element_type=F32)
        tie = jnp.where(rank <= need, 0.0, NEG_BIG)
        bias = jnp.where(kb > thr, 0.0, jnp.where(kb == thr, tie, NEG_BIG))
        bias_scr[i] = jnp.where(off + krow < chunk_end, bias, NEG_BIG)
        return rank[kblk - 1:kblk, :]

    lax.fori_loop(0, nblk, select_block, jnp.zeros((1, tq), F32))

    m_scr[...] = jnp.full(m_scr.shape, NEG_BIG, F32)
    acc_scr[...] = jnp.zeros(acc_scr.shape, F32)
    rep = ATT_HEADS // KV_HEADS
    qt = qt_ref[0]
    qg = [jnp.concatenate([qt[(g * rep + r) * HEAD_DIM:(g * rep + r + 1) * HEAD_DIM, :]
                           for r in range(rep)], axis=1) for g in range(KV_HEADS)]
    ones = jnp.ones((16, kblk), BF16)

    def score_phase(i, buf):
        off = pl.multiple_of(i * kblk, kblk)
        bias = jnp.concatenate([bias_scr[i]] * rep, axis=1)
        kk = kb_ref[0, pl.ds(off, kblk), :]
        for g in range(KV_HEADS):
            s = jnp.dot(kk[:, g * HEAD_DIM:(g + 1) * HEAD_DIM], qg[g],
                        preferred_element_type=F32) + bias
            s_scr[buf, g] = s
            m_blk = jnp.max(_fold_rows(s, jnp.maximum), axis=0, keepdims=True)
            m_old = m_scr[g]
            m_new = jnp.maximum(m_old, m_blk)
            a_scr[g] = jnp.exp2(m_old - m_new)
            m_scr[g] = m_new

    def value_phase(i, buf):
        off = pl.multiple_of(i * kblk, kblk)
        vv = vt_ref[0, :, pl.ds(off, kblk)]
        for g in range(KV_HEADS):
            p = jnp.exp2(s_scr[buf, g] - m_scr[g]).astype(BF16)
            vg = jnp.concatenate([vv[g * HEAD_DIM:(g + 1) * HEAD_DIM, :], ones], axis=0)
            acc_scr[g] = a_scr[g] * acc_scr[g] + jnp.dot(vg, p, preferred_element_type=F32)

    score_phase(0, 0)

    def attn_pair(k, carry):
        i = 2 * k
        value_phase(i, 0)
        score_phase(i + 1, 1)
        value_phase(i + 1, 1)
        score_phase(jnp.minimum(i + 2, nblk - 1), 0)
        return carry

    lax.fori_loop(0, nblk // 2, attn_pair, 0)

    @pl.when(nblk % 2 == 1)
    def _():
        value_phase(nblk - 1, 0)

    heads = []
    for g in range(KV_HEADS):
        acc = acc_scr[g]
        og = acc[:HEAD_DIM] / acc[HEAD_DIM:HEAD_DIM + 1]
        heads += [og[:, r * tq:(r + 1) * tq] for r in range(rep)]
    o_ref[0] = jnp.concatenate(heads, axis=0).T.astype(BF16)


def _dsa(qt, qit, wit, kb, vt, kib, *, tq, pos0, topk):
    b, _, t = qt.shape
    lp = kb.shape[1]
    kblk = DSA_KEY_BLOCK
    assert lp % kblk == 0 and t % tq == 0 and tq % LANES == 0
    kern = functools.partial(_dsa_kernel, tq=tq, kblk=kblk, pos0=pos0, topk=topk)
    rep = ATT_HEADS // KV_HEADS
    qmap = lambda bi, j: (bi, 0, j)
    kmap = lambda bi, j: (bi, 0, 0)
    return pl.pallas_call(
        kern,
        grid=(b, t // tq),
        in_specs=[pl.BlockSpec((1, ATT_WIDTH, tq), qmap),
                  pl.BlockSpec((1, IDX_HEADS * IDX_DIM, tq), qmap),
                  pl.BlockSpec((1, 8, tq), qmap),
                  pl.BlockSpec((1, lp, KV_WIDTH), kmap),
                  pl.BlockSpec((1, KV_WIDTH, lp), kmap),
                  pl.BlockSpec((1, lp, IDX_DIM), kmap)],
        out_specs=pl.BlockSpec((1, tq, ATT_WIDTH), lambda bi, j: (bi, j, 0)),
        out_shape=jax.ShapeDtypeStruct((b, t, ATT_WIDTH), BF16),
        scratch_shapes=[pltpu.VMEM((lp // kblk, kblk, tq), I32),
                        pltpu.VMEM((lp // kblk, kblk, tq), I32),
                        pltpu.VMEM((lp // kblk, kblk, tq), F32),
                        pltpu.VMEM((2, KV_HEADS, kblk, rep * tq), F32),
                        pltpu.VMEM((KV_HEADS, HEAD_DIM + 16, rep * tq), F32),
                        pltpu.VMEM((KV_HEADS, 1, rep * tq), F32),
                        pltpu.VMEM((KV_HEADS, 1, rep * tq), F32)],
        compiler_params=_params(("parallel", "arbitrary")),
        name="dsa",
    )(qt, qit, wit, kb, vt, kib)


def _gmlp_kernel(x_ref, wu_ref, wv_ref, g_ref, b_ref, ws_ref, bst_ref, o_ref, v_ref, *, tm, lc):
    xb = x_ref[...].astype(BF16)
    u = _gelu_tanh(jnp.dot(xb, wu_ref[...], preferred_element_type=F32))
    v = _layer_norm(_gelu_tanh(jnp.dot(xb, wv_ref[...], preferred_element_type=F32)),
                    g_ref[...], b_ref[...])
    v_ref[...] = v
    vb = v.astype(BF16)
    r = lax.broadcasted_iota(I32, (lc, lc), 0)
    c = lax.broadcasted_iota(I32, (lc, lc), 1)
    for g in range(SG_GROUPS):
        w = jnp.where(r >= c, ws_ref[g], 0.0).astype(BF16)
        bias = bst_ref[:, g:g + 1]
        cs = slice(g * SG_GW, (g + 1) * SG_GW)
        for n in range(tm // lc):
            rs = slice(n * lc, (n + 1) * lc)
            mixed = jnp.dot(w, vb[rs, cs], preferred_element_type=F32) + bias
            o_ref[rs, cs] = (u[rs, cs] * mixed).astype(BF16)


def _gmlp(x, wu, wv, ln_g, ln_b, ws, bst, *, tm, lc):
    n = x.shape[0]
    row = lambda i: (i, 0)
    full2 = lambda i: (0, 0)
    return pl.pallas_call(
        functools.partial(_gmlp_kernel, tm=tm, lc=lc),
        grid=(n // tm,),
        in_specs=[pl.BlockSpec((tm, D_MODEL), row),
                  pl.BlockSpec((D_MODEL, SG_WIDTH), full2),
                  pl.BlockSpec((D_MODEL, SG_WIDTH), full2),
                  pl.BlockSpec((1, SG_WIDTH), full2),
                  pl.BlockSpec((1, SG_WIDTH), full2),
                  pl.BlockSpec((SG_GROUPS, lc, lc), lambda i: (0, 0, 0)),
                  pl.BlockSpec((lc, SG_GROUPS), full2)],
        out_specs=[pl.BlockSpec((tm, SG_WIDTH), row), pl.BlockSpec((tm, SG_WIDTH), row)],
        out_shape=[jax.ShapeDtypeStruct((n, SG_WIDTH), BF16),
                   jax.ShapeDtypeStruct((n, SG_WIDTH), F32)],
        compiler_params=_params(("parallel",)),
        name="gmlp",
    )(x, wu, wv, ln_g, ln_b, ws, bst)


def _hgrn_kernel(x_ref, w_ref, loglb_ref, log1mlb_ref, omlb_ref, ng_ref, s0_ref, c_ref, sout_ref,
                 q_scr, k_scr, v_scr, lf_scr, o_scr, st_scr, *, tc):
    t = pl.program_id(1)

    @pl.when(t == 0)
    def _():
        for h in range(HG_HEADS):
            st_scr[h] = s0_ref[0, h].T

    xb = x_ref[0].astype(BF16)
    z = jnp.dot(xb, w_ref[...], preferred_element_type=F32)
    hq = z[:, 0:HG_WIDTH]
    hf = z[:, HG_WIDTH:2 * HG_WIDTH]
    hg = z[:, 3 * HG_WIDTH:4 * HG_WIDTH]
    q_scr[...] = hq * _sigmoid(hq) * HG_DK ** -0.5
    log_sig = jnp.minimum(hf, 0.0) - jnp.log1p(jnp.exp(-jnp.abs(hf)))
    y = log1mlb_ref[...] + log_sig
    a = loglb_ref[...]
    lf_scr[...] = jnp.maximum(a, y) + jnp.log1p(jnp.exp(-jnp.abs(a - y)))
    k_scr[...] = omlb_ref[...] * _sigmoid(-hf)
    v_scr[...] = z[:, 2 * HG_WIDTH:3 * HG_WIDTH]

    ch = min(tc, CHUNK)
    ri = lax.broadcasted_iota(I32, (tc, tc), 0)
    ci = lax.broadcasted_iota(I32, (tc, tc), 1)
    same = (ri // ch) == (ci // ch)
    causal = same & (ri >= ci)
    lf = lf_scr[...]
    lf_hi = lf.astype(BF16)
    lf_r = lf - lf_hi.astype(F32)
    lf_mid = lf_r.astype(BF16)
    lf_lo = (lf_r - lf_mid.astype(F32)).astype(BF16)
    cmask = jnp.where(causal, 1.0, 0.0).astype(BF16)
    bcum_all = (jnp.dot(cmask, lf_hi, preferred_element_type=F32)
                + jnp.dot(cmask, lf_mid, preferred_element_type=F32)
                + jnp.dot(cmask, lf_lo, preferred_element_type=F32))
    factored = jnp.min(bcum_all) > -HG_FACTOR_RANGE

    @pl.when(factored)
    def _():
        btot = jnp.concatenate(
            [jnp.broadcast_to(bcum_all[(n + 1) * ch - 1:(n + 1) * ch, :], (ch, HG_WIDTH))
             for n in range(tc // ch)], axis=0)
        qa = q_scr[...]
        ka = k_scr[...]
        qd = (qa * jnp.exp(bcum_all)).astype(BF16)
        ki = (ka * jnp.exp(-bcum_all)).astype(BF16)
        kd = (ka * jnp.exp(btot - bcum_all)).astype(BF16)
        va = v_scr[...].astype(BF16)
        for h in range(HG_HEADS):
            cs = slice(h * HG_DK, (h + 1) * HG_DK)
            att = lax.dot_general(qd[:, cs], ki[:, cs], _NT, preferred_element_type=F32)
            att = jnp.where(causal, att, 0.0).astype(BF16)
            o = jnp.dot(att, va[:, cs], preferred_element_type=F32)
            st = st_scr[h]
            for n in range(tc // ch):
                rs = slice(n * ch, (n + 1) * ch)
                o_scr[rs, cs] = o[rs] + lax.dot_general(qd[rs, cs], st.astype(BF16), _NT,
                                                        preferred_element_type=F32)
                upd = lax.dot_general(va[rs, cs], kd[rs, cs], _TN, preferred_element_type=F32)
                st = jnp.exp(btot[n * ch:n * ch + 1, cs]) * st + upd
            st_scr[h] = st

    nb = HG_BLOCK
    tri = jnp.where(lax.broadcasted_iota(I32, (nb, nb), 0) >= lax.broadcasted_iota(I32, (nb, nb), 1),
                    1.0, 0.0).astype(F32)
    trow = lax.broadcasted_iota(I32, (nb, 1), 0)

    def block(i, carry):
        r0 = pl.multiple_of(i * nb, nb)
        rows = pl.ds(r0, nb)
        for h in range(HG_HEADS):
            cs = slice(h * HG_DK, (h + 1) * HG_DK)
            bcum = jnp.dot(tri, lf_scr[rows, cs], preferred_element_type=F32,
                           precision=lax.Precision.HIGHEST)
            qb = q_scr[rows, cs]
            kb = k_scr[rows, cs]
            vb = v_scr[rows, cs]
            o = jnp.zeros((nb, HG_DV), F32)
            for s in range(nb):
                e = jnp.exp(jnp.where(trow >= s, bcum - bcum[s:s + 1, :], -jnp.inf))
                a_ts = jnp.sum(qb * kb[s:s + 1, :] * e, axis=1, keepdims=True)
                o = o + a_ts * vb[s:s + 1, :]
            st = st_scr[h]
            qd = (qb * jnp.exp(bcum)).astype(BF16)
            o = o + lax.dot_general(qd, st.astype(BF16), _NT, preferred_element_type=F32)
            blast = bcum[nb - 1:nb, :]
            kd = (kb * jnp.exp(blast - bcum)).astype(BF16)
            upd = lax.dot_general(vb.astype(BF16), kd, _TN, preferred_element_type=F32)
            st_scr[h] = jnp.exp(blast) * st + upd
            o_scr[rows, cs] = o
        return carry

    @pl.when(jnp.logical_not(factored))
    def _():
        lax.fori_loop(0, tc // nb, block, 0)

    for h in range(HG_HEADS):
        cs = slice(h * HG_DV, (h + 1) * HG_DV)
        o = o_scr[:, cs]
        o = o * lax.rsqrt(jnp.mean(o * o, axis=-1, keepdims=True) + LN_EPS)
        g = hg[:, cs]
        c_ref[0, :, cs] = (o * ng_ref[:, cs] * (g * _sigmoid(g))).astype(BF16)

    @pl.when(t == pl.num_programs(1) - 1)
    def _():
        for h in range(HG_HEADS):
            sout_ref[0, h] = st_scr[h].T


def _hgrn(x, w, loglb, log1mlb, omlb, ng, s0, *, tc):
    b, t, _ = x.shape
    xmap = lambda bi, j: (bi, j, 0)
    vec = pl.BlockSpec((1, HG_WIDTH), lambda bi, j: (0, 0))
    smap = lambda bi, j: (bi, 0, 0, 0)
    return pl.pallas_call(
        functools.partial(_hgrn_kernel, tc=tc),
        grid=(b, t // tc),
        in_specs=[pl.BlockSpec((1, tc, D_MODEL), xmap),
                  pl.BlockSpec((D_MODEL, 4 * HG_WIDTH), lambda bi, j: (0, 0)),
                  vec, vec, vec, vec,
                  pl.BlockSpec((1, HG_HEADS, HG_DK, HG_DV), smap)],
        out_specs=[pl.BlockSpec((1, tc, HG_WIDTH), xmap),
                   pl.BlockSpec((1, HG_HEADS, HG_DK, HG_DV), smap)],
        out_shape=[jax.ShapeDtypeStruct((b, t, HG_WIDTH), BF16),
                   jax.ShapeDtypeStruct((b, HG_HEADS, HG_DK, HG_DV), F32)],
        scratch_shapes=[pltpu.VMEM((tc, HG_WIDTH), F32)] * 5
                       + [pltpu.VMEM((HG_HEADS, HG_DV, HG_DK), F32)],
        compiler_params=_params(("parallel", "arbitrary")),
        name="hgrn",
    )(x, w, loglb, log1mlb, omlb, ng, s0)


def _merge_kernel(x_ref, a_ref, b_ref, c_ref, wg_ref, wa_ref, wb_ref, wc_ref, wo_ref,
                  g_ref, bt_ref, y_ref):
    x = x_ref[...]
    xb = x.astype(BF16)
    m = None
    for i, (br, w) in enumerate(((a_ref, wa_ref), (b_ref, wb_ref), (c_ref, wc_ref))):
        gate = _sigmoid(jnp.dot(xb, wg_ref[:, i * D_MODEL:(i + 1) * D_MODEL],
                                preferred_element_type=F32))
        term = gate * jnp.dot(br[...], w[...], preferred_element_type=F32)
        m = term if m is None else m + term
    mo = jnp.dot(m.astype(BF16), wo_ref[...], preferred_element_type=F32)
    y_ref[...] = _layer_norm(ALPHA * x + mo, g_ref[...], bt_ref[...])


def _merge(x, a, b, c, wg, wa, wb, wc, wo, ln_g, ln_b, *, tm):
    n = x.shape[0]
    row = lambda i: (i, 0)
    full = lambda i: (0, 0)
    br = pl.BlockSpec((tm, ATT_WIDTH), row)
    bw = pl.BlockSpec((ATT_WIDTH, D_MODEL), full)
    vec = pl.BlockSpec((1, D_MODEL), full)
    return pl.pallas_call(
        _merge_kernel,
        grid=(n // tm,),
        in_specs=[pl.BlockSpec((tm, D_MODEL), row), br, br, br,
                  pl.BlockSpec((D_MODEL, N_BRANCH * D_MODEL), full), bw, bw, bw,
                  pl.BlockSpec((D_MODEL, D_MODEL), full), vec, vec],
        out_specs=pl.BlockSpec((tm, D_MODEL), row),
        out_shape=jax.ShapeDtypeStruct((n, D_MODEL), F32),
        compiler_params=_params(("parallel",)),
        name="merge",
    )(x, a, b, c, wg, wa, wb, wc, wo, ln_g, ln_b)


def _route(logits):
    ex = jnp.exp(logits - jnp.max(logits, axis=0, keepdims=True))
    probs = ex / jnp.sum(ex, axis=0, keepdims=True)
    p = [[probs[g * EXP_PER_GROUP + k:g * EXP_PER_GROUP + k + 1, :] for k in range(EXP_PER_GROUP)]
         for g in range(N_GROUPS)]
    score = []
    for g in range(N_GROUPS):
        best = None
        for k1 in range(EXP_PER_GROUP):
            for k2 in range(k1 + 1, EXP_PER_GROUP):
                pair = p[g][k1] + p[g][k2]
                best = pair if best is None else jnp.maximum(best, pair)
        score.append(best)
    gsel = jnp.zeros(score[0].shape, I32)
    top = score[0]
    for g in range(1, N_GROUPS):
        better = score[g] > top
        top = jnp.where(better, score[g], top)
        gsel = jnp.where(better, g, gsel)
    val = []
    for k in range(EXP_PER_GROUP):
        v = p[0][k]
        for g in range(1, N_GROUPS):
            v = jnp.where(gsel == g, p[g][k], v)
        val.append(v)
    v1, i1 = val[0], jnp.zeros(gsel.shape, I32)
    for k in range(1, EXP_PER_GROUP):
        better = val[k] > v1
        v1 = jnp.where(better, val[k], v1)
        i1 = jnp.where(better, k, i1)
    v2, i2 = jnp.full(v1.shape, -1.0, F32), jnp.zeros(gsel.shape, I32)
    for k in range(EXP_PER_GROUP):
        better = (i1 != k) & (val[k] > v2)
        v2 = jnp.where(better, val[k], v2)
        i2 = jnp.where(better, k, i2)
    den = v1 + v2
    w1, w2 = v1 / den, v2 / den
    out = []
    for g in range(N_GROUPS):
        rows = [jnp.where(gsel == g, jnp.where(i1 == k, w1, jnp.where(i2 == k, w2, 0.0)), 0.0)
                for k in range(EXP_PER_GROUP)]
        out.append(jnp.concatenate(rows, axis=0))
    return out


def _moe_kernel(x_ref, wr_ref, br_ref, wg_ref, wu_ref, wd_ref, g_ref, bt_ref, y_ref,
                xb_scr, gate_scr, acc_scr, *, tm):
    grp = pl.program_id(1)

    @pl.when(grp == 0)
    def _():
        xb = x_ref[...].astype(BF16)
        xb_scr[...] = xb
        logits = lax.dot_general(wr_ref[...], xb, _NT, preferred_element_type=F32) + br_ref[...]
        gates = _route(logits)
        pad = jnp.zeros((LANES - EXP_PER_GROUP, tm), F32)
        for g in range(N_GROUPS):
            gate_scr[g] = jnp.concatenate([gates[g], pad], axis=0).T
        acc_scr[...] = jnp.zeros(acc_scr.shape, F32)

    xb = xb_scr[...]
    gate = gate_scr[grp]
    acc = acc_scr[...]
    for k in range(EXP_PER_GROUP):
        h = jnp.dot(xb, wg_ref[k], preferred_element_type=F32)
        up = jnp.dot(xb, wu_ref[k], preferred_element_type=F32)
        act = h * _sigmoid(h) * up * gate[:, k:k + 1]
        acc = acc + jnp.dot(act.astype(BF16), wd_ref[k], preferred_element_type=F32)
    acc_scr[...] = acc

    @pl.when(grp == N_GROUPS - 1)
    def _():
        y_ref[...] = _layer_norm(ALPHA * x_ref[...] + acc, g_ref[...], bt_ref[...])


def _moe(x, wr_t, br, wg, wu, wd, ln_g, ln_b, *, tm):
    n = x.shape[0]
    row = lambda i, g: (i, 0)
    full = lambda i, g: (0, 0)
    vec = pl.BlockSpec((1, D_MODEL), full)
    return pl.pallas_call(
        functools.partial(_moe_kernel, tm=tm),
        grid=(n // tm, N_GROUPS),
        in_specs=[pl.BlockSpec((tm, D_MODEL), row),
                  pl.BlockSpec((N_EXPERTS, D_MODEL), full),
                  pl.BlockSpec((N_EXPERTS, 1), full),
                  pl.BlockSpec((EXP_PER_GROUP, D_MODEL, D_FF), lambda i, g: (g, 0, 0)),
                  pl.BlockSpec((EXP_PER_GROUP, D_MODEL, D_FF), lambda i, g: (g, 0, 0)),
                  pl.BlockSpec((EXP_PER_GROUP, D_FF, D_MODEL), lambda i, g: (g, 0, 0)),
                  vec, vec],
        out_specs=pl.BlockSpec((tm, D_MODEL), row),
        out_shape=jax.ShapeDtypeStruct((n, D_MODEL), F32),
        scratch_shapes=[pltpu.VMEM((tm, D_MODEL), BF16),
                        pltpu.VMEM((N_GROUPS, tm, LANES), F32),
                        pltpu.VMEM((tm, D_MODEL), F32)],
        compiler_params=_params(("parallel", "arbitrary")),
        name="moe",
    )(x, wr_t, br, wg, wu, wd, ln_g, ln_b)


WPREP_ROWS = 128


def _split_w_in_kernel(w_ref, wn_ref, wt_ref, wu_ref, wv_ref, whg_ref, wg_ref):
    w = w_ref[0]
    offs = [0]
    for n in SPLIT_SIZES:
        offs.append(offs[-1] + n)
    seg = lambda i, j=None: w[:, offs[i]:offs[i + 1 if j is None else j]]
    zeros = lambda c: jnp.zeros((w.shape[0], c), w.dtype)
    wn_ref[0] = jnp.concatenate([seg(1), seg(2), seg(4), zeros(LANES - IDX_DIM)], axis=1).astype(BF16)
    wt = jnp.concatenate([seg(0), seg(3), seg(2), seg(5), zeros(ATT_T_PAD - ATT_T_ROWS + 8 - IDX_HEADS)],
                         axis=1)
    for c in range(ATT_T_PAD // LANES):
        rows = min(LANES, ATT_T_ROWS - c * LANES)
        wt_ref[0, c * LANES:c * LANES + rows, :] = wt[:, c * LANES:(c + 1) * LANES].T[:rows].astype(BF16)
    wu_ref[0] = seg(6).astype(BF16)
    wv_ref[0] = seg(7).astype(BF16)
    whg_ref[0] = seg(8, 12).astype(BF16)
    wg_ref[0] = seg(12).astype(BF16)


def _split_w_in(w_in):
    depth, d, ncol = w_in.shape
    tr = WPREP_ROWS
    cols = (ATT_N_COLS, None, SG_WIDTH, SG_WIDTH, 4 * HG_WIDTH, N_BRANCH * D_MODEL)
    shapes = [(depth, ATT_T_ROWS, d) if c is None else (depth, d, c) for c in cols]
    specs = [pl.BlockSpec((1, ATT_T_ROWS, tr), lambda l, i: (l, 0, i)) if c is None
             else pl.BlockSpec((1, tr, c), lambda l, i: (l, i, 0)) for c in cols]
    return pl.pallas_call(
        _split_w_in_kernel,
        grid=(depth, d // tr),
        in_specs=[pl.BlockSpec((1, tr, ncol), lambda l, i: (l, i, 0))],
        out_specs=specs,
        out_shape=[jax.ShapeDtypeStruct(sh, BF16) for sh in shapes],
        compiler_params=_params(("parallel", "parallel")),
        name="split_w_in",
    )(w_in)


def _row_tile(n, want):
    while n % want:
        want //= 2
    return want


def _layer(x, bsz, t, lw, *, pos0, topk, cache, s0):
    n = bsz * t
    lc = min(t, SG_LEN)
    k, v, ki, kb, kib, qt, qit, vt, wit = _attn_proj(x, lw["w_att_n"], lw["w_att_t"], bsz, t,
                                                     _row_tile(t, 512))
    kb3 = kb.reshape(bsz, t, KV_WIDTH)
    kib3 = kib.reshape(bsz, t, IDX_DIM)
    if cache is not None:
        ck, cvt, cki = cache
        kb3 = jnp.concatenate([ck, kb3], axis=1)
        kib3 = jnp.concatenate([cki, kib3], axis=1)
        vt = jnp.concatenate([cvt, vt], axis=2)
    ltot = kb3.shape[1]
    lp = -(-ltot // DSA_KEY_BLOCK) * DSA_KEY_BLOCK
    tq = LANES
    tp = -(-t // tq) * tq
    padq = lambda a: jnp.pad(a, ((0, 0), (0, 0), (0, tp - t)))
    a = _dsa(padq(qt), padq(qit), padq(wit), jnp.pad(kb3, ((0, 0), (0, lp - ltot), (0, 0))),
             jnp.pad(vt, ((0, 0), (0, 0), (0, lp - ltot))),
             jnp.pad(kib3, ((0, 0), (0, lp - ltot), (0, 0))), tq=tq, pos0=pos0, topk=topk)[:, :t]
    b, v_gm = _gmlp(x, lw["w_u"], lw["w_v"], lw["ln_sg_g"], lw["ln_sg_b"],
                    lw["w_sg"][:, :lc, :lc], lw["b_sg"][:, :lc].T, tm=_row_tile(n, 256), lc=lc)
    c, s_new = _hgrn(x.reshape(bsz, t, D_MODEL), lw["w_hg"], lw["log_lb"], lw["log1m_lb"],
                     lw["om_lb"], lw["hg_norm_g"], s0, tc=min(t, 256))
    x1 = _merge(x, a.reshape(n, ATT_WIDTH), b, c.reshape(n, HG_WIDTH), lw["w_gates"],
                lw["w_branch_a"], lw["w_branch_b"], lw["w_branch_c"], lw["w_out"],
                lw["ln1_g"], lw["ln1_b"], tm=_row_tile(n, 512))
    x2 = _moe(x1, lw["w_router_t"], lw["b_router"], lw["w_exp_gate"], lw["w_exp_up"],
              lw["w_exp_down"], lw["ln2_g"], lw["ln2_b"], tm=_row_tile(n, 1024))
    return x2, k, v, ki, s_new, v_gm


def kernel(x_prompt, x_sample, cache_k, cache_v, cache_kidx, state_hgrn, w_in, w_sg, b_sg, ln_sg_g, ln_sg_b, hg_lb_logits, hg_norm_g, w_branch_a, w_branch_b, w_branch_c, w_out, ln1_g, ln1_b, w_router, b_router, w_exp_gate, w_exp_up, w_exp_down, ln2_g, ln2_b):
    bp, sp, _ = x_prompt.shape
    bs, ss, _ = x_sample.shape
    past = cache_k.shape[2]
    topk_p = min(TOPK_MAX, sp // 4)
    topk_s = min(TOPK_MAX, (past + ss) // 4)

    lb_all = jnp.cumsum(jax.nn.softmax(hg_lb_logits.astype(F32), axis=0), axis=0)
    lb_all = lb_all - lb_all[0:1]
    vec = lambda a: a.reshape(1, -1)

    xp = x_prompt.reshape(bp * sp, D_MODEL)
    xs = x_sample.reshape(bs * ss, D_MODEL)
    s0_p = jnp.zeros((bp, HG_HEADS, HG_DK, HG_DV), F32)
    outs_p, outs_s = [], []
    w_att_n, w_att_t, w_u, w_v, w_hg, w_gates = _split_w_in(w_in)
    for l in range(DEPTH):
        lw = dict(
            w_att_n=w_att_n[l], w_att_t=w_att_t[l], w_u=w_u[l], w_v=w_v[l], w_hg=w_hg[l],
            w_gates=w_gates[l],
            ln_sg_g=vec(ln_sg_g[l]), ln_sg_b=vec(ln_sg_b[l]), w_sg=w_sg[l], b_sg=b_sg[l],
            log_lb=vec(jnp.log(lb_all[l])), log1m_lb=vec(jnp.log1p(-lb_all[l])),
            om_lb=vec(1.0 - lb_all[l]), hg_norm_g=vec(hg_norm_g[l].astype(F32)),
            w_branch_a=w_branch_a[l].astype(BF16), w_branch_b=w_branch_b[l].astype(BF16),
            w_branch_c=w_branch_c[l].astype(BF16), w_out=w_out[l].astype(BF16),
            ln1_g=vec(ln1_g[l]), ln1_b=vec(ln1_b[l]),
            w_router_t=w_router.T.astype(BF16), b_router=b_router.astype(F32).reshape(-1, 1),
            w_exp_gate=w_exp_gate[l].astype(BF16), w_exp_up=w_exp_up[l].astype(BF16),
            w_exp_down=w_exp_down[l].astype(BF16),
            ln2_g=vec(ln2_g[l]), ln2_b=vec(ln2_b[l]))
        xp, k, v, ki, s_new, _ = _layer(xp, bp, sp, lw, pos0=0, topk=topk_p, cache=None, s0=s0_p)
        outs_p.append((k.reshape(bp, sp, KV_HEADS, HEAD_DIM), v.reshape(bp, sp, KV_HEADS, HEAD_DIM),
                       ki.reshape(bp, sp, IDX_DIM), s_new))
        cache = (cache_k[l].reshape(bs, past, KV_WIDTH).astype(BF16),
                 jnp.swapaxes(cache_v[l].reshape(bs, past, KV_WIDTH), 1, 2).astype(BF16),
                 cache_kidx[l].astype(BF16))
        xs, k, v, ki, s_new, v_gm = _layer(xs, bs, ss, lw, pos0=past, topk=topk_s, cache=cache,
                                           s0=state_hgrn[l].astype(F32))
        outs_s.append((k.reshape(bs, ss, KV_HEADS, HEAD_DIM), v.reshape(bs, ss, KV_HEADS, HEAD_DIM),
                       ki.reshape(bs, ss, IDX_DIM), s_new, v_gm.reshape(bs, ss, SG_WIDTH)))

    stack = lambda rows, i: jnp.stack([r[i] for r in rows])
    return (xp.reshape(bp, sp, D_MODEL), xs.reshape(bs, ss, D_MODEL),
            stack(outs_p, 0), stack(outs_p, 1), stack(outs_p, 2), stack(outs_p, 3),
            stack(outs_s, 0), stack(outs_s, 1), stack(outs_s, 2), stack(outs_s, 3),
            stack(outs_s, 4))
```

```python
import functools

import jax
import jax.numpy as jnp
from jax import lax
from jax.experimental import pallas as pl
from jax.experimental.pallas import tpu as pltpu

F32 = jnp.float32
BF16 = jnp.bfloat16
I32 = jnp.int32

D_MODEL = 1024
DEPTH = 4
CHUNK = 64
ATT_HEADS = 8
KV_HEADS = 2
HEAD_DIM = 64
ATT_WIDTH = ATT_HEADS * HEAD_DIM
KV_WIDTH = KV_HEADS * HEAD_DIM
IDX_HEADS = 4
IDX_DIM = 64
TOPK_MAX = 256
SG_LEN = 128
SG_GROUPS = 4
SG_WIDTH = 512
SG_GW = SG_WIDTH // SG_GROUPS
HG_HEADS = 4
HG_DK = 128
HG_DV = 128
HG_WIDTH = HG_HEADS * HG_DV
HG_BLOCK = 16
HG_FACTOR_RANGE = 80.0
N_BRANCH = 3
N_EXPERTS = 16
N_GROUPS = 4
EXP_PER_GROUP = 4
D_FF = 256
ALPHA = (2 * DEPTH) ** 0.25
LN_EPS = 1e-5

SPLIT_SIZES = (ATT_WIDTH, KV_WIDTH, KV_WIDTH, IDX_HEADS * IDX_DIM, IDX_DIM, IDX_HEADS,
               SG_WIDTH, SG_WIDTH, HG_HEADS * HG_DK, HG_HEADS * HG_DK, HG_WIDTH, HG_WIDTH,
               N_BRANCH * D_MODEL)

LANES = 128
INT_MIN = -2 ** 31
NEG_BIG = -1e30
LOG2_E = 1.4426950408889634
DSA_KEY_BLOCK = 512
DSA_QUERY_TILE = 256
VMEM_LIMIT = 56 * 1024 * 1024

_NT = (((1,), (1,)), ((), ()))
_TN = (((0,), (0,)), ((), ()))


def _params(sem):
    return pltpu.CompilerParams(dimension_semantics=sem, vmem_limit_bytes=VMEM_LIMIT)


def _layer_norm(x, g, b):
    mu = jnp.mean(x, axis=-1, keepdims=True)
    d = x - mu
    var = jnp.mean(d * d, axis=-1, keepdims=True)
    return d * lax.rsqrt(var + LN_EPS) * g + b


def _gelu_tanh(x):
    return 0.5 * x * (1.0 + jnp.tanh(0.7978845608028654 * (x + 0.044715 * (x * x * x))))


def _sigmoid(x):
    return 0.5 * jnp.tanh(0.5 * x) + 0.5


def _fold_rows(x, op):
    parts = [x[i:i + 8] for i in range(0, x.shape[0], 8)]
    while len(parts) > 1:
        parts = [op(parts[i], parts[i + 1]) for i in range(0, len(parts), 2)]
    return parts[0]


ATT_T_ROWS = ATT_WIDTH + IDX_HEADS * IDX_DIM + KV_WIDTH + 8
ATT_T_PAD = -(-ATT_T_ROWS // LANES) * LANES
ATT_N_COLS = 3 * LANES


def _attn_proj_kernel(x_ref, wn_ref, wt_ref, k_ref, v_ref, ki_ref, kb_ref, kib_ref,
                      qt_ref, qit_ref, vt_ref, wit_ref):
    xb = x_ref[...].astype(BF16)
    z = jnp.dot(xb, wn_ref[...], preferred_element_type=F32)
    k = z[:, 0:KV_WIDTH]
    ki = z[:, 2 * KV_WIDTH:2 * KV_WIDTH + IDX_DIM]
    k_ref[...] = k
    v_ref[...] = z[:, KV_WIDTH:2 * KV_WIDTH]
    ki_ref[...] = ki
    kb_ref[...] = k.astype(BF16)
    kib_ref[...] = ki.astype(BF16)
    zt = lax.dot_general(wt_ref[...], xb, _NT, preferred_element_type=F32)
    o = 0
    qt_ref[0] = (zt[o:o + ATT_WIDTH] * (HEAD_DIM ** -0.5 * LOG2_E)).astype(BF16)
    o += ATT_WIDTH
    qit_ref[0] = (zt[o:o + IDX_HEADS * IDX_DIM] * IDX_DIM ** -0.5).astype(BF16)
    o += IDX_HEADS * IDX_DIM
    vt_ref[0] = zt[o:o + KV_WIDTH].astype(BF16)
    o += KV_WIDTH
    wit_ref[0] = zt[o:o + 8] * IDX_HEADS ** -0.5


def _attn_proj(x, wn, wt, bsz, t, tm):
    n = bsz * t
    per = t // tm
    row = lambda i: (i, 0)
    tmap = lambda i: (i // per, 0, i % per)
    nat = ((KV_WIDTH, F32), (KV_WIDTH, F32), (IDX_DIM, F32), (KV_WIDTH, BF16), (IDX_DIM, BF16))
    tr = ((ATT_WIDTH, BF16), (IDX_HEADS * IDX_DIM, BF16), (KV_WIDTH, BF16), (8, F32))
    return pl.pallas_call(
        _attn_proj_kernel,
        grid=(n // tm,),
        in_specs=[pl.BlockSpec((tm, D_MODEL), row),
                  pl.BlockSpec((D_MODEL, ATT_N_COLS), lambda i: (0, 0)),
                  pl.BlockSpec((ATT_T_ROWS, D_MODEL), lambda i: (0, 0))],
        out_specs=[pl.BlockSpec((tm, c), row) for c, _ in nat]
                  + [pl.BlockSpec((1, c, tm), tmap) for c, _ in tr],
        out_shape=[jax.ShapeDtypeStruct((n, c), d) for c, d in nat]
                  + [jax.ShapeDtypeStruct((bsz, c, t), d) for c, d in tr],
        compiler_params=_params(("parallel",)),
        name="attn_proj",
    )(x, wn, wt)


def _dsa_kernel(qt_ref, qit_ref, wit_ref, kb_ref, vt_ref, kib_ref, o_ref,
                key_scr, sort_scr, bias_scr, s_scr, acc_scr, m_scr, a_scr,
                *, tq, kblk, pos0, topk):
    j = pl.program_id(1)
    qpos0 = pos0 + j * tq
    qcol = lax.broadcasted_iota(I32, (1, tq), 1)
    chunk_end = ((qpos0 + qcol) // CHUNK + 1) * CHUNK
    lvis = ((qpos0 + tq - 1) // CHUNK + 1) * CHUNK
    nblk = (lvis + kblk - 1) // kblk
    krow = lax.broadcasted_iota(I32, (kblk, tq), 0)

    wi = wit_ref[0]
    qit = qit_ref[0]

    def score_block(i, carry):
        off = pl.multiple_of(i * kblk, kblk)
        kib = kib_ref[0, pl.ds(off, kblk), :]
        sc = jnp.zeros((kblk, tq), F32)
        for h in range(IDX_HEADS):
            raw = jnp.dot(kib, qit[h * IDX_DIM:(h + 1) * IDX_DIM, :], preferred_element_type=F32)
            sc = sc + jnp.maximum(raw, 0.0) * wi[h:h + 1, :]
        bits = pltpu.bitcast(sc, I32)
        key = bits ^ ((bits >> 31) & 0x7FFFFFFF)
        key = jnp.where(off + krow < chunk_end, key, INT_MIN)
        key_scr[i] = key
        for g in range(kblk // 32):
            a, b, c, d = (key[g * 32 + 8 * r:g * 32 + 8 * r + 8] for r in range(4))
            a, b = jnp.maximum(a, b), jnp.minimum(a, b)
            c, d = jnp.maximum(c, d), jnp.minimum(c, d)
            a, c = jnp.maximum(a, c), jnp.minimum(a, c)
            b, d = jnp.maximum(b, d), jnp.minimum(b, d)
            b, c = jnp.maximum(b, c), jnp.minimum(b, c)
            sort_scr[i, g * 32:(g + 1) * 32] = jnp.concatenate([a, b, c, d], axis=0)
        return carry

    lax.fori_loop(0, nblk, score_block, 0)

    def count(hits):
        def body(i, acc):
            return acc + _fold_rows(hits(i, key_scr[i]), jnp.add)
        acc = lax.fori_loop(0, nblk, body, jnp.zeros((8, tq), F32))
        return jnp.sum(acc, axis=0, keepdims=True)

    def key_bit(i, thr):
        cand = thr + jnp.left_shift(jnp.int32(1), 31 - i)

        def body(b, accs):
            ks = sort_scr[b]
            accs = list(accs)
            for g in range(kblk // 32):
                s0, s1, s2, s3 = (ks[g * 32 + 8 * r:g * 32 + 8 * r + 8] for r in range(4))
                n = jnp.where(s3 >= cand, 4.0, jnp.where(s2 >= cand, 3.0, jnp.where(
                    s1 >= cand, 2.0, jnp.where(s0 >= cand, 1.0, 0.0))))
                accs[g % 4] = accs[g % 4] + n
            return tuple(accs)

        accs = lax.fori_loop(0, nblk, body, tuple(jnp.zeros((8, tq), F32) for _ in range(4)))
        cnt = jnp.sum((accs[0] + accs[1]) + (accs[2] + accs[3]), axis=0, keepdims=True)
        return jnp.where(cnt >= topk, cand, thr)

    thr = lax.fori_loop(0, 32, key_bit, jnp.full((1, tq), INT_MIN, I32))

    need = topk - count(lambda _, kb: jnp.where(kb > thr, 1.0, 0.0))
    tri = jnp.where(lax.broadcasted_iota(I32, (kblk, kblk), 0) >= lax.broadcasted_iota(I32, (kblk, kblk), 1),
                    1.0, 0.0).astype(BF16)

    def select_block(i, taken):
        off = pl.multiple_of(i * kblk, kblk)
        kb = key_scr[i]
        tied = jnp.where(kb == thr, 1.0, 0.0)
        rank = taken + jnp.dot(tri, tied.astype(BF16), preferred_element_type=F32)
        tie = jnp.where(rank <= need, 0.0, NEG_BIG)
        bias = jnp.where(kb > thr, 0.0, jnp.where(kb == thr, tie, NEG_BIG))
        bias_scr[i] = jnp.where(off + krow < chunk_end, bias, NEG_BIG)
        return rank[kblk - 1:kblk, :]

    lax.fori_loop(0, nblk, select_block, jnp.zeros((1, tq), F32))

    m_scr[...] = jnp.full(m_scr.shape, NEG_BIG, F32)
    acc_scr[...] = jnp.zeros(acc_scr.shape, F32)
    rep = ATT_HEADS // KV_HEADS
    qt = qt_ref[0]
    qg = [jnp.concatenate([qt[(g * rep + r) * HEAD_DIM:(g * rep + r + 1) * HEAD_DIM, :]
                           for r in range(rep)], axis=1) for g in range(KV_HEADS)]
    ones = jnp.ones((16, kblk), BF16)

    def score_phase(i, buf):
        off = pl.multiple_of(i * kblk, kblk)
        bias = jnp.concatenate([bias_scr[i]] * rep, axis=1)
        kk = kb_ref[0, pl.ds(off, kblk), :]
        for g in range(KV_HEADS):
            s = jnp.dot(kk[:, g * HEAD_DIM:(g + 1) * HEAD_DIM], qg[g],
                        preferred_element_type=F32) + bias
            s_scr[buf, g] = s
            m_blk = jnp.max(_fold_rows(s, jnp.maximum), axis=0, keepdims=True)
            m_old = m_scr[g]
            m_new = jnp.maximum(m_old, m_blk)
            a_scr[g] = jnp.exp2(m_old - m_new)
            m_scr[g] = m_new

    def value_phase(i, buf):
        off = pl.multiple_of(i * kblk, kblk)
        vv = vt_ref[0, :, pl.ds(off, kblk)]
        for g in range(KV_HEADS):
            p = jnp.exp2(s_scr[buf, g] - m_scr[g]).astype(BF16)
            vg = jnp.concatenate([vv[g * HEAD_DIM:(g + 1) * HEAD_DIM, :], ones], axis=0)
            acc_scr[g] = a_scr[g] * acc_scr[g] + jnp.dot(vg, p, preferred_element_type=F32)

    score_phase(0, 0)

    def attn_pair(k, carry):
        i = 2 * k
        value_phase(i, 0)
        score_phase(i + 1, 1)
        value_phase(i + 1, 1)
        score_phase(jnp.minimum(i + 2, nblk - 1), 0)
        return carry

    lax.fori_loop(0, nblk // 2, attn_pair, 0)

    @pl.when(nblk % 2 == 1)
    def _():
        value_phase(nblk - 1, 0)

    heads = []
    for g in range(KV_HEADS):
        acc = acc_scr[g]
        og = acc[:HEAD_DIM] / acc[HEAD_DIM:HEAD_DIM + 1]
        heads += [og[:, r * tq:(r + 1) * tq] for r in range(rep)]
    o_ref[0] = jnp.concatenate(heads, axis=0).T.astype(BF16)


def _dsa(qt, qit, wit, kb, vt, kib, *, tq, pos0, topk):
    b, _, t = qt.shape
    lp = kb.shape[1]
    kblk = DSA_KEY_BLOCK
    assert lp % kblk == 0 and t % tq == 0 and tq % LANES == 0
    kern = functools.partial(_dsa_kernel, tq=tq, kblk=kblk, pos0=pos0, topk=topk)
    rep = ATT_HEADS // KV_HEADS
    qmap = lambda bi, j: (bi, 0, j)
    kmap = lambda bi, j: (bi, 0, 0)
    return pl.pallas_call(
        kern,
        grid=(b, t // tq),
        in_specs=[pl.BlockSpec((1, ATT_WIDTH, tq), qmap),
                  pl.BlockSpec((1, IDX_HEADS * IDX_DIM, tq), qmap),
                  pl.BlockSpec((1, 8, tq), qmap),
                  pl.BlockSpec((1, lp, KV_WIDTH), kmap),
                  pl.BlockSpec((1, KV_WIDTH, lp), kmap),
                  pl.BlockSpec((1, lp, IDX_DIM), kmap)],
        out_specs=pl.BlockSpec((1, tq, ATT_WIDTH), lambda bi, j: (bi, j, 0)),
        out_shape=jax.ShapeDtypeStruct((b, t, ATT_WIDTH), BF16),
        scratch_shapes=[pltpu.VMEM((lp // kblk, kblk, tq), I32),
                        pltpu.VMEM((lp // kblk, kblk, tq), I32),
                        pltpu.VMEM((lp // kblk, kblk, tq), F32),
                        pltpu.VMEM((2, KV_HEADS, kblk, rep * tq), F32),
                        pltpu.VMEM((KV_HEADS, HEAD_DIM + 16, rep * tq), F32),
                        pltpu.VMEM((KV_HEADS, 1, rep * tq), F32),
                        pltpu.VMEM((KV_HEADS, 1, rep * tq), F32)],
        compiler_params=_params(("parallel", "arbitrary")),
        name="dsa",
    )(qt, qit, wit, kb, vt, kib)


def _gmlp_kernel(x_ref, wu_ref, wv_ref, g_ref, b_ref, ws_ref, bst_ref, o_ref, v_ref, *, tm, lc):
    xb = x_ref[...].astype(BF16)
    u = _gelu_tanh(jnp.dot(xb, wu_ref[...], preferred_element_type=F32))
    v = _layer_norm(_gelu_tanh(jnp.dot(xb, wv_ref[...], preferred_element_type=F32)),
                    g_ref[...], b_ref[...])
    v_ref[...] = v
    vb = v.astype(BF16)
    r = lax.broadcasted_iota(I32, (lc, lc), 0)
    c = lax.broadcasted_iota(I32, (lc, lc), 1)
    for g in range(SG_GROUPS):
        w = jnp.where(r >= c, ws_ref[g], 0.0).astype(BF16)
        bias = bst_ref[:, g:g + 1]
        cs = slice(g * SG_GW, (g + 1) * SG_GW)
        for n in range(tm // lc):
            rs = slice(n * lc, (n + 1) * lc)
            mixed = jnp.dot(w, vb[rs, cs], preferred_element_type=F32) + bias
            o_ref[rs, cs] = (u[rs, cs] * mixed).astype(BF16)


def _gmlp(x, wu, wv, ln_g, ln_b, ws, bst, *, tm, lc):
    n = x.shape[0]
    row = lambda i: (i, 0)
    full2 = lambda i: (0, 0)
    return pl.pallas_call(
        functools.partial(_gmlp_kernel, tm=tm, lc=lc),
        grid=(n // tm,),
        in_specs=[pl.BlockSpec((tm, D_MODEL), row),
                  pl.BlockSpec((D_MODEL, SG_WIDTH), full2),
                  pl.BlockSpec((D_MODEL, SG_WIDTH), full2),
                  pl.BlockSpec((1, SG_WIDTH), full2),
                  pl.BlockSpec((1, SG_WIDTH), full2),
                  pl.BlockSpec((SG_GROUPS, lc, lc), lambda i: (0, 0, 0)),
                  pl.BlockSpec((lc, SG_GROUPS), full2)],
        out_specs=[pl.BlockSpec((tm, SG_WIDTH), row), pl.BlockSpec((tm, SG_WIDTH), row)],
        out_shape=[jax.ShapeDtypeStruct((n, SG_WIDTH), BF16),
                   jax.ShapeDtypeStruct((n, SG_WIDTH), F32)],
        compiler_params=_params(("parallel",)),
        name="gmlp",
    )(x, wu, wv, ln_g, ln_b, ws, bst)


def _hgrn_kernel(x_ref, w_ref, loglb_ref, log1mlb_ref, omlb_ref, ng_ref, s0_ref, c_ref, sout_ref,
                 q_scr, k_scr, v_scr, lf_scr, o_scr, st_scr, *, tc):
    t = pl.program_id(1)

    @pl.when(t == 0)
    def _():
        for h in range(HG_HEADS):
            st_scr[h] = s0_ref[0, h].T

    xb = x_ref[0].astype(BF16)
    z = jnp.dot(xb, w_ref[...], preferred_element_type=F32)
    hq = z[:, 0:HG_WIDTH]
    hf = z[:, HG_WIDTH:2 * HG_WIDTH]
    hg = z[:, 3 * HG_WIDTH:4 * HG_WIDTH]
    q_scr[...] = hq * _sigmoid(hq) * HG_DK ** -0.5
    u = jnp.exp(-jnp.abs(hf))
    w = 1.0 + u
    log_sig = jnp.minimum(hf, 0.0) - jnp.log(w)
    y = log1mlb_ref[...] + log_sig
    a = loglb_ref[...]
    lf_scr[...] = jnp.maximum(a, y) + jnp.log(1.0 + jnp.exp(-jnp.abs(a - y)))
    k_scr[...] = omlb_ref[...] * (jnp.where(hf >= 0.0, u, 1.0) / w)
    v_scr[...] = z[:, 2 * HG_WIDTH:3 * HG_WIDTH]

    ch = min(tc, CHUNK)
    ri = lax.broadcasted_iota(I32, (tc, tc), 0)
    ci = lax.broadcasted_iota(I32, (tc, tc), 1)
    same = (ri // ch) == (ci // ch)
    causal = same & (ri >= ci)
    lf = lf_scr[...]
    lf_hi = lf.astype(BF16)
    lf_r = lf - lf_hi.astype(F32)
    lf_mid = lf_r.astype(BF16)
    lf_lo = (lf_r - lf_mid.astype(F32)).astype(BF16)
    cmask = jnp.where(causal, 1.0, 0.0).astype(BF16)
    bcum_all = (jnp.dot(cmask, lf_hi, preferred_element_type=F32)
                + jnp.dot(cmask, lf_mid, preferred_element_type=F32)
                + jnp.dot(cmask, lf_lo, preferred_element_type=F32))
    factored = jnp.min(bcum_all) > -HG_FACTOR_RANGE

    @pl.when(factored)
    def _():
        btot = jnp.concatenate(
            [jnp.broadcast_to(bcum_all[(n + 1) * ch - 1:(n + 1) * ch, :], (ch, HG_WIDTH))
             for n in range(tc // ch)], axis=0)
        qa = q_scr[...]
        ka = k_scr[...]
        qd = (qa * jnp.exp(bcum_all)).astype(BF16)
        ki = (ka * jnp.exp(-bcum_all)).astype(BF16)
        kd = (ka * jnp.exp(btot - bcum_all)).astype(BF16)
        va = v_scr[...].astype(BF16)
        for h in range(HG_HEADS):
            cs = slice(h * HG_DK, (h + 1) * HG_DK)
            att = lax.dot_general(qd[:, cs], ki[:, cs], _NT, preferred_element_type=F32)
            att = jnp.where(causal, att, 0.0).astype(BF16)
            o = jnp.dot(att, va[:, cs], preferred_element_type=F32)
            st = st_scr[h]
            for n in range(tc // ch):
                rs = slice(n * ch, (n + 1) * ch)
                o_scr[rs, cs] = o[rs] + lax.dot_general(qd[rs, cs], st.astype(BF16), _NT,
                                                        preferred_element_type=F32)
                upd = lax.dot_general(va[rs, cs], kd[rs, cs], _TN, preferred_element_type=F32)
                st = jnp.exp(btot[n * ch:n * ch + 1, cs]) * st + upd
            st_scr[h] = st

    nb = HG_BLOCK
    tri = jnp.where(lax.broadcasted_iota(I32, (nb, nb), 0) >= lax.broadcasted_iota(I32, (nb, nb), 1),
                    1.0, 0.0).astype(F32)
    trow = lax.broadcasted_iota(I32, (nb, 1), 0)

    def block(i, carry):
        r0 = pl.multiple_of(i * nb, nb)
        rows = pl.ds(r0, nb)
        for h in range(HG_HEADS):
            cs = slice(h * HG_DK, (h + 1) * HG_DK)
            bcum = jnp.dot(tri, lf_scr[rows, cs], preferred_element_type=F32,
                           precision=lax.Precision.HIGHEST)
            qb = q_scr[rows, cs]
            kb = k_scr[rows, cs]
            vb = v_scr[rows, cs]
            o = jnp.zeros((nb, HG_DV), F32)
            for s in range(nb):
                e = jnp.exp(jnp.where(trow >= s, bcum - bcum[s:s + 1, :], -jnp.inf))
                a_ts = jnp.sum(qb * kb[s:s + 1, :] * e, axis=1, keepdims=True)
                o = o + a_ts * vb[s:s + 1, :]
            st = st_scr[h]
            qd = (qb * jnp.exp(bcum)).astype(BF16)
            o = o + lax.dot_general(qd, st.astype(BF16), _NT, preferred_element_type=F32)
            blast = bcum[nb - 1:nb, :]
            kd = (kb * jnp.exp(blast - bcum)).astype(BF16)
            upd = lax.dot_general(vb.astype(BF16), kd, _TN, preferred_element_type=F32)
            st_scr[h] = jnp.exp(blast) * st + upd
            o_scr[rows, cs] = o
        return carry

    @pl.when(jnp.logical_not(factored))
    def _():
        lax.fori_loop(0, tc // nb, block, 0)

    for h in range(HG_HEADS):
        cs = slice(h * HG_DV, (h + 1) * HG_DV)
        o = o_scr[:, cs]
        o = o * lax.rsqrt(jnp.mean(o * o, axis=-1, keepdims=True) + LN_EPS)
        g = hg[:, cs]
        c_ref[0, :, cs] = (o * ng_ref[:, cs] * (g * _sigmoid(g))).astype(BF16)

    @pl.when(t == pl.num_programs(1) - 1)
    def _():
        for h in range(HG_HEADS):
            sout_ref[0, h] = st_scr[h].T


def _hgrn(x, w, loglb, log1mlb, omlb, ng, s0, *, tc):
    b, t, _ = x.shape
    xmap = lambda bi, j: (bi, j, 0)
    vec = pl.BlockSpec((1, HG_WIDTH), lambda bi, j: (0, 0))
    smap = lambda bi, j: (bi, 0, 0, 0)
    return pl.pallas_call(
        functools.partial(_hgrn_kernel, tc=tc),
        grid=(b, t // tc),
        in_specs=[pl.BlockSpec((1, tc, D_MODEL), xmap),
                  pl.BlockSpec((D_MODEL, 4 * HG_WIDTH), lambda bi, j: (0, 0)),
                  vec, vec, vec, vec,
                  pl.BlockSpec((1, HG_HEADS, HG_DK, HG_DV), smap)],
        out_specs=[pl.BlockSpec((1, tc, HG_WIDTH), xmap),
                   pl.BlockSpec((1, HG_HEADS, HG_DK, HG_DV), smap)],
        out_shape=[jax.ShapeDtypeStruct((b, t, HG_WIDTH), BF16),
                   jax.ShapeDtypeStruct((b, HG_HEADS, HG_DK, HG_DV), F32)],
        scratch_shapes=[pltpu.VMEM((tc, HG_WIDTH), F32)] * 5
                       + [pltpu.VMEM((HG_HEADS, HG_DV, HG_DK), F32)],
        compiler_params=_params(("parallel", "arbitrary")),
        name="hgrn",
    )(x, w, loglb, log1mlb, omlb, ng, s0)


def _merge_kernel(x_ref, a_ref, b_ref, c_ref, wg_ref, wa_ref, wb_ref, wc_ref, wo_ref,
                  g_ref, bt_ref, y_ref):
    x = x_ref[...]
    xb = x.astype(BF16)
    m = None
    for i, (br, w) in enumerate(((a_ref, wa_ref), (b_ref, wb_ref), (c_ref, wc_ref))):
        gate = _sigmoid(jnp.dot(xb, wg_ref[:, i * D_MODEL:(i + 1) * D_MODEL],
                                preferred_element_type=F32))
        term = gate * jnp.dot(br[...], w[...], preferred_element_type=F32)
        m = term if m is None else m + term
    mo = jnp.dot(m.astype(BF16), wo_ref[...], preferred_element_type=F32)
    y_ref[...] = _layer_norm(ALPHA * x + mo, g_ref[...], bt_ref[...])


def _merge(x, a, b, c, wg, wa, wb, wc, wo, ln_g, ln_b, *, tm):
    n = x.shape[0]
    row = lambda i: (i, 0)
    full = lambda i: (0, 0)
    br = pl.BlockSpec((tm, ATT_WIDTH), row)
    bw = pl.BlockSpec((ATT_WIDTH, D_MODEL), full)
    vec = pl.BlockSpec((1, D_MODEL), full)
    return pl.pallas_call(
        _merge_kernel,
        grid=(n // tm,),
        in_specs=[pl.BlockSpec((tm, D_MODEL), row), br, br, br,
                  pl.BlockSpec((D_MODEL, N_BRANCH * D_MODEL), full), bw, bw, bw,
                  pl.BlockSpec((D_MODEL, D_MODEL), full), vec, vec],
        out_specs=pl.BlockSpec((tm, D_MODEL), row),
        out_shape=jax.ShapeDtypeStruct((n, D_MODEL), F32),
        compiler_params=_params(("parallel",)),
        name="merge",
    )(x, a, b, c, wg, wa, wb, wc, wo, ln_g, ln_b)


def _route(logits):
    ex = jnp.exp(logits - jnp.max(logits, axis=0, keepdims=True))
    probs = ex / jnp.sum(ex, axis=0, keepdims=True)
    p = [[probs[g * EXP_PER_GROUP + k:g * EXP_PER_GROUP + k + 1, :] for k in range(EXP_PER_GROUP)]
         for g in range(N_GROUPS)]
    score = []
    for g in range(N_GROUPS):
        best = None
        for k1 in range(EXP_PER_GROUP):
            for k2 in range(k1 + 1, EXP_PER_GROUP):
                pair = p[g][k1] + p[g][k2]
                best = pair if best is None else jnp.maximum(best, pair)
        score.append(best)
    gsel = jnp.zeros(score[0].shape, I32)
    top = score[0]
    for g in range(1, N_GROUPS):
        better = score[g] > top
        top = jnp.where(better, score[g], top)
        gsel = jnp.where(better, g, gsel)
    val = []
    for k in range(EXP_PER_GROUP):
        v = p[0][k]
        for g in range(1, N_GROUPS):
            v = jnp.where(gsel == g, p[g][k], v)
        val.append(v)
    v1, i1 = val[0], jnp.zeros(gsel.shape, I32)
    for k in range(1, EXP_PER_GROUP):
        better = val[k] > v1
        v1 = jnp.where(better, val[k], v1)
        i1 = jnp.where(better, k, i1)
    v2, i2 = jnp.full(v1.shape, -1.0, F32), jnp.zeros(gsel.shape, I32)
    for k in range(EXP_PER_GROUP):
        better = (i1 != k) & (val[k] > v2)
        v2 = jnp.where(better, val[k], v2)
        i2 = jnp.where(better, k, i2)
    den = v1 + v2
    w1, w2 = v1 / den, v2 / den
    out = []
    for g in range(N_GROUPS):
        rows = [jnp.where(gsel == g, jnp.where(i1 == k, w1, jnp.where(i2 == k, w2, 0.0)), 0.0)
                for k in range(EXP_PER_GROUP)]
        out.append(jnp.concatenate(rows, axis=0))
    return out


def _moe_kernel(x_ref, wr_ref, br_ref, wg_ref, wu_ref, wd_ref, g_ref, bt_ref, y_ref,
                xb_scr, gate_scr, acc_scr, *, tm):
    grp = pl.program_id(1)

    @pl.when(grp == 0)
    def _():
        xb = x_ref[...].astype(BF16)
        xb_scr[...] = xb
        logits = lax.dot_general(wr_ref[...], xb, _NT, preferred_element_type=F32) + br_ref[...]
        gates = _route(logits)
        pad = jnp.zeros((LANES - EXP_PER_GROUP, tm), F32)
        for g in range(N_GROUPS):
            gate_scr[g] = jnp.concatenate([gates[g], pad], axis=0).T
        acc_scr[...] = jnp.zeros(acc_scr.shape, F32)

    xb = xb_scr[...]
    gate = gate_scr[grp]
    acc = acc_scr[...]
    for k in range(EXP_PER_GROUP):
        h = jnp.dot(xb, wg_ref[k], preferred_element_type=F32)
        up = jnp.dot(xb, wu_ref[k], preferred_element_type=F32)
        act = h * _sigmoid(h) * up * gate[:, k:k + 1]
        acc = acc + jnp.dot(act.astype(BF16), wd_ref[k], preferred_element_type=F32)
    acc_scr[...] = acc

    @pl.when(grp == N_GROUPS - 1)
    def _():
        y_ref[...] = _layer_norm(ALPHA * x_ref[...] + acc, g_ref[...], bt_ref[...])


def _moe(x, wr_t, br, wg, wu, wd, ln_g, ln_b, *, tm):
    n = x.shape[0]
    row = lambda i, g: (i, 0)
    full = lambda i, g: (0, 0)
    vec = pl.BlockSpec((1, D_MODEL), full)
    return pl.pallas_call(
        functools.partial(_moe_kernel, tm=tm),
        grid=(n // tm, N_GROUPS),
        in_specs=[pl.BlockSpec((tm, D_MODEL), row),
                  pl.BlockSpec((N_EXPERTS, D_MODEL), full),
                  pl.BlockSpec((N_EXPERTS, 1), full),
                  pl.BlockSpec((EXP_PER_GROUP, D_MODEL, D_FF), lambda i, g: (g, 0, 0)),
                  pl.BlockSpec((EXP_PER_GROUP, D_MODEL, D_FF), lambda i, g: (g, 0, 0)),
                  pl.BlockSpec((EXP_PER_GROUP, D_FF, D_MODEL), lambda i, g: (g, 0, 0)),
                  vec, vec],
        out_specs=pl.BlockSpec((tm, D_MODEL), row),
        out_shape=jax.ShapeDtypeStruct((n, D_MODEL), F32),
        scratch_shapes=[pltpu.VMEM((tm, D_MODEL), BF16),
                        pltpu.VMEM((N_GROUPS, tm, LANES), F32),
                        pltpu.VMEM((tm, D_MODEL), F32)],
        compiler_params=_params(("parallel", "arbitrary")),
        name="moe",
    )(x, wr_t, br, wg, wu, wd, ln_g, ln_b)


WPREP_ROWS = 128


def _split_w_in_kernel(w_ref, wn_ref, wt_ref, wu_ref, wv_ref, whg_ref, wg_ref):
    w = w_ref[0]
    offs = [0]
    for n in SPLIT_SIZES:
        offs.append(offs[-1] + n)
    seg = lambda i, j=None: w[:, offs[i]:offs[i + 1 if j is None else j]]
    zeros = lambda c: jnp.zeros((w.shape[0], c), w.dtype)
    wn_ref[0] = jnp.concatenate([seg(1), seg(2), seg(4), zeros(LANES - IDX_DIM)], axis=1).astype(BF16)
    wt = jnp.concatenate([seg(0), seg(3), seg(2), seg(5), zeros(ATT_T_PAD - ATT_T_ROWS + 8 - IDX_HEADS)],
                         axis=1)
    for c in range(ATT_T_PAD // LANES):
        rows = min(LANES, ATT_T_ROWS - c * LANES)
        wt_ref[0, c * LANES:c * LANES + rows, :] = wt[:, c * LANES:(c + 1) * LANES].T[:rows].astype(BF16)
    wu_ref[0] = seg(6).astype(BF16)
    wv_ref[0] = seg(7).astype(BF16)
    whg_ref[0] = seg(8, 12).astype(BF16)
    wg_ref[0] = seg(12).astype(BF16)


def _split_w_in(w_in):
    depth, d, ncol = w_in.shape
    tr = WPREP_ROWS
    cols = (ATT_N_COLS, None, SG_WIDTH, SG_WIDTH, 4 * HG_WIDTH, N_BRANCH * D_MODEL)
    shapes = [(depth, ATT_T_ROWS, d) if c is None else (depth, d, c) for c in cols]
    specs = [pl.BlockSpec((1, ATT_T_ROWS, tr), lambda l, i: (l, 0, i)) if c is None
             else pl.BlockSpec((1, tr, c), lambda l, i: (l, i, 0)) for c in cols]
    return pl.pallas_call(
        _split_w_in_kernel,
        grid=(depth, d // tr),
        in_specs=[pl.BlockSpec((1, tr, ncol), lambda l, i: (l, i, 0))],
        out_specs=specs,
        out_shape=[jax.ShapeDtypeStruct(sh, BF16) for sh in shapes],
        compiler_params=_params(("parallel", "parallel")),
        name="split_w_in",
    )(w_in)


def _row_tile(n, want):
    while n % want:
        want //= 2
    return want


def _layer(x, bsz, t, lw, *, pos0, topk, cache, s0):
    n = bsz * t
    lc = min(t, SG_LEN)
    k, v, ki, kb, kib, qt, qit, vt, wit = _attn_proj(x, lw["w_att_n"], lw["w_att_t"], bsz, t,
                                                     _row_tile(t, 512))
    kb3 = kb.reshape(bsz, t, KV_WIDTH)
    kib3 = kib.reshape(bsz, t, IDX_DIM)
    if cache is not None:
        ck, cvt, cki = cache
        kb3 = jnp.concatenate([ck, kb3], axis=1)
        kib3 = jnp.concatenate([cki, kib3], axis=1)
        vt = jnp.concatenate([cvt, vt], axis=2)
    ltot = kb3.shape[1]
    lp = -(-ltot // DSA_KEY_BLOCK) * DSA_KEY_BLOCK
    tq = DSA_QUERY_TILE if t % DSA_QUERY_TILE == 0 else LANES
    tp = -(-t // tq) * tq
    padq = lambda a: jnp.pad(a, ((0, 0), (0, 0), (0, tp - t)))
    a = _dsa(padq(qt), padq(qit), padq(wit), jnp.pad(kb3, ((0, 0), (0, lp - ltot), (0, 0))),
             jnp.pad(vt, ((0, 0), (0, 0), (0, lp - ltot))),
             jnp.pad(kib3, ((0, 0), (0, lp - ltot), (0, 0))), tq=tq, pos0=pos0, topk=topk)[:, :t]
    b, v_gm = _gmlp(x, lw["w_u"], lw["w_v"], lw["ln_sg_g"], lw["ln_sg_b"],
                    lw["w_sg"][:, :lc, :lc], lw["b_sg"][:, :lc].T, tm=_row_tile(n, 256), lc=lc)
    c, s_new = _hgrn(x.reshape(bsz, t, D_MODEL), lw["w_hg"], lw["log_lb"], lw["log1m_lb"],
                     lw["om_lb"], lw["hg_norm_g"], s0, tc=min(t, 256))
    x1 = _merge(x, a.reshape(n, ATT_WIDTH), b, c.reshape(n, HG_WIDTH), lw["w_gates"],
                lw["w_branch_a"], lw["w_branch_b"], lw["w_branch_c"], lw["w_out"],
                lw["ln1_g"], lw["ln1_b"], tm=_row_tile(n, 512))
    x2 = _moe(x1, lw["w_router_t"], lw["b_router"], lw["w_exp_gate"], lw["w_exp_up"],
              lw["w_exp_down"], lw["ln2_g"], lw["ln2_b"], tm=_row_tile(n, 1024))
    return x2, k, v, ki, s_new, v_gm


def kernel(x_prompt, x_sample, cache_k, cache_v, cache_kidx, state_hgrn, w_in, w_sg, b_sg, ln_sg_g, ln_sg_b, hg_lb_logits, hg_norm_g, w_branch_a, w_branch_b, w_branch_c, w_out, ln1_g, ln1_b, w_router, b_router, w_exp_gate, w_exp_up, w_exp_down, ln2_g, ln2_b):
    bp, sp, _ = x_prompt.shape
    bs, ss, _ = x_sample.shape
    past = cache_k.shape[2]
    topk_p = min(TOPK_MAX, sp // 4)
    topk_s = min(TOPK_MAX, (past + ss) // 4)

    lb_all = jnp.cumsum(jax.nn.softmax(hg_lb_logits.astype(F32), axis=0), axis=0)
    lb_all = lb_all - lb_all[0:1]
    vec = lambda a: a.reshape(1, -1)

    xp = x_prompt.reshape(bp * sp, D_MODEL)
    xs = x_sample.reshape(bs * ss, D_MODEL)
    s0_p = jnp.zeros((bp, HG_HEADS, HG_DK, HG_DV), F32)
    outs_p, outs_s = [], []
    w_att_n, w_att_t, w_u, w_v, w_hg, w_gates = _split_w_in(w_in)
    for l in range(DEPTH):
        lw = dict(
            w_att_n=w_att_n[l], w_att_t=w_att_t[l], w_u=w_u[l], w_v=w_v[l], w_hg=w_hg[l],
            w_gates=w_gates[l],
            ln_sg_g=vec(ln_sg_g[l]), ln_sg_b=vec(ln_sg_b[l]), w_sg=w_sg[l], b_sg=b_sg[l],
            log_lb=vec(jnp.log(lb_all[l])), log1m_lb=vec(jnp.log1p(-lb_all[l])),
            om_lb=vec(1.0 - lb_all[l]), hg_norm_g=vec(hg_norm_g[l].astype(F32)),
            w_branch_a=w_branch_a[l].astype(BF16), w_branch_b=w_branch_b[l].astype(BF16),
            w_branch_c=w_branch_c[l].astype(BF16), w_out=w_out[l].astype(BF16),
            ln1_g=vec(ln1_g[l]), ln1_b=vec(ln1_b[l]),
            w_router_t=w_router.T.astype(BF16), b_router=b_router.astype(F32).reshape(-1, 1),
            w_exp_gate=w_exp_gate[l].astype(BF16), w_exp_up=w_exp_up[l].astype(BF16),
            w_exp_down=w_exp_down[l].astype(BF16),
            ln2_g=vec(ln2_g[l]), ln2_b=vec(ln2_b[l]))
        xp, k, v, ki, s_new, _ = _layer(xp, bp, sp, lw, pos0=0, topk=topk_p, cache=None, s0=s0_p)
        outs_p.append((k.reshape(bp, sp, KV_HEADS, HEAD_DIM), v.reshape(bp, sp, KV_HEADS, HEAD_DIM),
                       ki.reshape(bp, sp, IDX_DIM), s_new))
        cache = (cache_k[l].reshape(bs, past, KV_WIDTH).astype(BF16),
                 jnp.swapaxes(cache_v[l].reshape(bs, past, KV_WIDTH), 1, 2).astype(BF16),
                 cache_kidx[l].astype(BF16))
        xs, k, v, ki, s_new, v_gm = _layer(xs, bs, ss, lw, pos0=past, topk=topk_s, cache=cache,
                                           s0=state_hgrn[l].astype(F32))
        outs_s.append((k.reshape(bs, ss, KV_HEADS, HEAD_DIM), v.reshape(bs, ss, KV_HEADS, HEAD_DIM),
                       ki.reshape(bs, ss, IDX_DIM), s_new, v_gm.reshape(bs, ss, SG_WIDTH)))

    stack = lambda rows, i: jnp.stack([r[i] for r in rows])
    return (xp.reshape(bp, sp, D_MODEL), xs.reshape(bs, ss, D_MODEL),
            stack(outs_p, 0), stack(outs_p, 1), stack(outs_p, 2), stack(outs_p, 3),
            stack(outs_s, 0), stack(outs_s, 1), stack(outs_s, 2), stack(outs_s, 3),
            stack(outs_s, 4))
```

```python
import functools

import jax
import jax.numpy as jnp
from jax import lax
from jax.experimental import pallas as pl
from jax.experimental.pallas import tpu as pltpu

F32 = jnp.float32
BF16 = jnp.bfloat16
I32 = jnp.int32

D_MODEL = 1024
DEPTH = 4
CHUNK = 64
ATT_HEADS = 8
KV_HEADS = 2
HEAD_DIM = 64
ATT_WIDTH = ATT_HEADS * HEAD_DIM
KV_WIDTH = KV_HEADS * HEAD_DIM
IDX_HEADS = 4
IDX_DIM = 64
TOPK_MAX = 256
SG_LEN = 128
SG_GROUPS = 4
SG_WIDTH = 512
SG_GW = SG_WIDTH // SG_GROUPS
HG_HEADS = 4
HG_DK = 128
HG_DV = 128
HG_WIDTH = HG_HEADS * HG_DV
HG_BLOCK = 16
HG_FACTOR_RANGE = 80.0
N_BRANCH = 3
N_EXPERTS = 16
N_GROUPS = 4
EXP_PER_GROUP = 4
D_FF = 256
ALPHA = (2 * DEPTH) ** 0.25
LN_EPS = 1e-5

SPLIT_SIZES = (ATT_WIDTH, KV_WIDTH, KV_WIDTH, IDX_HEADS * IDX_DIM, IDX_DIM, IDX_HEADS,
               SG_WIDTH, SG_WIDTH, HG_HEADS * HG_DK, HG_HEADS * HG_DK, HG_WIDTH, HG_WIDTH,
               N_BRANCH * D_MODEL)

LANES = 128
INT_MIN = -2 ** 31
NEG_BIG = -1e30
LOG2_E = 1.4426950408889634
DSA_KEY_BLOCK = 512
DSA_QUERY_TILE = 512
VMEM_LIMIT = 56 * 1024 * 1024

_NT = (((1,), (1,)), ((), ()))
_TN = (((0,), (0,)), ((), ()))


def _params(sem):
    return pltpu.CompilerParams(dimension_semantics=sem, vmem_limit_bytes=VMEM_LIMIT)


def _layer_norm(x, g, b):
    mu = jnp.mean(x, axis=-1, keepdims=True)
    d = x - mu
    var = jnp.mean(d * d, axis=-1, keepdims=True)
    return d * lax.rsqrt(var + LN_EPS) * g + b


def _gelu_tanh(x):
    return 0.5 * x * (1.0 + jnp.tanh(0.7978845608028654 * (x + 0.044715 * (x * x * x))))


def _sigmoid(x):
    return 0.5 * jnp.tanh(0.5 * x) + 0.5


def _fold_rows(x, op):
    parts = [x[i:i + 8] for i in range(0, x.shape[0], 8)]
    while len(parts) > 1:
        parts = [op(parts[i], parts[i + 1]) for i in range(0, len(parts), 2)]
    return parts[0]


ATT_T_ROWS = ATT_WIDTH + IDX_HEADS * IDX_DIM + KV_WIDTH + 8
ATT_T_PAD = -(-ATT_T_ROWS // LANES) * LANES
ATT_N_COLS = 3 * LANES


def _attn_proj_kernel(x_ref, wn_ref, wt_ref, k_ref, v_ref, ki_ref, kb_ref, kib_ref,
                      qt_ref, qit_ref, vt_ref, wit_ref):
    xb = x_ref[...].astype(BF16)
    z = jnp.dot(xb, wn_ref[...], preferred_element_type=F32)
    k = z[:, 0:KV_WIDTH]
    ki = z[:, 2 * KV_WIDTH:2 * KV_WIDTH + IDX_DIM]
    k_ref[...] = k
    v_ref[...] = z[:, KV_WIDTH:2 * KV_WIDTH]
    ki_ref[...] = ki
    kb_ref[...] = k.astype(BF16)
    kib_ref[...] = ki.astype(BF16)
    zt = lax.dot_general(wt_ref[...], xb, _NT, preferred_element_type=F32)
    o = 0
    qt_ref[0] = (zt[o:o + ATT_WIDTH] * (HEAD_DIM ** -0.5 * LOG2_E)).astype(BF16)
    o += ATT_WIDTH
    qit_ref[0] = (zt[o:o + IDX_HEADS * IDX_DIM] * IDX_DIM ** -0.5).astype(BF16)
    o += IDX_HEADS * IDX_DIM
    vt_ref[0] = zt[o:o + KV_WIDTH].astype(BF16)
    o += KV_WIDTH
    wit_ref[0] = zt[o:o + 8] * IDX_HEADS ** -0.5


def _attn_proj(x, wn, wt, bsz, t, tm):
    n = bsz * t
    per = t // tm
    row = lambda i: (i, 0)
    tmap = lambda i: (i // per, 0, i % per)
    nat = ((KV_WIDTH, F32), (KV_WIDTH, F32), (IDX_DIM, F32), (KV_WIDTH, BF16), (IDX_DIM, BF16))
    tr = ((ATT_WIDTH, BF16), (IDX_HEADS * IDX_DIM, BF16), (KV_WIDTH, BF16), (8, F32))
    return pl.pallas_call(
        _attn_proj_kernel,
        grid=(n // tm,),
        in_specs=[pl.BlockSpec((tm, D_MODEL), row),
                  pl.BlockSpec((D_MODEL, ATT_N_COLS), lambda i: (0, 0)),
                  pl.BlockSpec((ATT_T_ROWS, D_MODEL), lambda i: (0, 0))],
        out_specs=[pl.BlockSpec((tm, c), row) for c, _ in nat]
                  + [pl.BlockSpec((1, c, tm), tmap) for c, _ in tr],
        out_shape=[jax.ShapeDtypeStruct((n, c), d) for c, d in nat]
                  + [jax.ShapeDtypeStruct((bsz, c, t), d) for c, d in tr],
        compiler_params=_params(("parallel",)),
        name="attn_proj",
    )(x, wn, wt)


def _dsa_kernel(qt_ref, qit_ref, wit_ref, kb_ref, vt_ref, kib_ref, o_ref,
                key_scr, sort_scr, bias_scr, s_scr, acc_scr, m_scr, a_scr,
                *, tq, kblk, pos0, topk):
    j = pl.program_id(1)
    qpos0 = pos0 + j * tq
    qcol = lax.broadcasted_iota(I32, (1, tq), 1)
    chunk_end = ((qpos0 + qcol) // CHUNK + 1) * CHUNK
    lvis = ((qpos0 + tq - 1) // CHUNK + 1) * CHUNK
    nblk = (lvis + kblk - 1) // kblk
    krow = lax.broadcasted_iota(I32, (kblk, tq), 0)

    wi = wit_ref[0]
    qit = qit_ref[0]

    def score_block(i, carry):
        off = pl.multiple_of(i * kblk, kblk)
        kib = kib_ref[0, pl.ds(off, kblk), :]
        sc = jnp.zeros((kblk, tq), F32)
        for h in range(IDX_HEADS):
            raw = jnp.dot(kib, qit[h * IDX_DIM:(h + 1) * IDX_DIM, :], preferred_element_type=F32)
            sc = sc + jnp.maximum(raw, 0.0) * wi[h:h + 1, :]
        bits = pltpu.bitcast(sc, I32)
        key = bits ^ ((bits >> 31) & 0x7FFFFFFF)
        key = jnp.where(off + krow < chunk_end, key, INT_MIN)
        key_scr[i] = key
        for g in range(kblk // 32):
            a, b, c, d = (key[g * 32 + 8 * r:g * 32 + 8 * r + 8] for r in range(4))
            a, b = jnp.maximum(a, b), jnp.minimum(a, b)
            c, d = jnp.maximum(c, d), jnp.minimum(c, d)
            a, c = jnp.maximum(a, c), jnp.minimum(a, c)
            b, d = jnp.maximum(b, d), jnp.minimum(b, d)
            b, c = jnp.maximum(b, c), jnp.minimum(b, c)
            sort_scr[i, g * 32:(g + 1) * 32] = jnp.concatenate([a, b, c, d], axis=0)
        return carry

    lax.fori_loop(0, nblk, score_block, 0)

    def count(hits):
        def body(i, acc):
            return acc + _fold_rows(hits(i, key_scr[i]), jnp.add)
        acc = lax.fori_loop(0, nblk, body, jnp.zeros((8, tq), F32))
        return jnp.sum(acc, axis=0, keepdims=True)

    def key_bit(i, thr):
        cand = thr + jnp.left_shift(jnp.int32(1), 31 - i)

        def body(b, accs):
            ks = sort_scr[b]
            accs = list(accs)
            for g in range(kblk // 32):
                s0, s1, s2, s3 = (ks[g * 32 + 8 * r:g * 32 + 8 * r + 8] for r in range(4))
                n = jnp.where(s3 >= cand, 4.0, jnp.where(s2 >= cand, 3.0, jnp.where(
                    s1 >= cand, 2.0, jnp.where(s0 >= cand, 1.0, 0.0))))
                accs[g % 4] = accs[g % 4] + n
            return tuple(accs)

        accs = lax.fori_loop(0, nblk, body, tuple(jnp.zeros((8, tq), F32) for _ in range(4)))
        cnt = jnp.sum((accs[0] + accs[1]) + (accs[2] + accs[3]), axis=0, keepdims=True)
        return jnp.where(cnt >= topk, cand, thr)

    thr = lax.fori_loop(0, 32, key_bit, jnp.full((1, tq), INT_MIN, I32))

    need = topk - count(lambda _, kb: jnp.where(kb > thr, 1.0, 0.0))
    tri = jnp.where(lax.broadcasted_iota(I32, (kblk, kblk), 0) >= lax.broadcasted_iota(I32, (kblk, kblk), 1),
                    1.0, 0.0).astype(BF16)

    def select_block(i, taken):
        off = pl.multiple_of(i * kblk, kblk)
        kb = key_scr[i]
        tied = jnp.where(kb == thr, 1.0, 0.0)
        rank = taken + jnp.dot(tri, tied.astype(BF16), preferred_element_type=F32)
        tie = jnp.where(rank <= need, 0.0, NEG_BIG)
        bias = jnp.where(kb > thr, 0.0, jnp.where(kb == thr, tie, NEG_BIG))
        bias_scr[i] = jnp.where(off + krow < chunk_end, bias, NEG_BIG)
        return rank[kblk - 1:kblk, :]

    lax.fori_loop(0, nblk, select_block, jnp.zeros((1, tq), F32))

    m_scr[...] = jnp.full(m_scr.shape, NEG_BIG, F32)
    acc_scr[...] = jnp.zeros(acc_scr.shape, F32)
    rep = ATT_HEADS // KV_HEADS
    qt = qt_ref[0]
    qg = [jnp.concatenate([qt[(g * rep + r) * HEAD_DIM:(g * rep + r + 1) * HEAD_DIM, :]
                           for r in range(rep)], axis=1) for g in range(KV_HEADS)]
    ones = jnp.ones((16, kblk), BF16)

    def score_phase(i, buf):
        off = pl.multiple_of(i * kblk, kblk)
        bias = jnp.concatenate([bias_scr[i]] * rep, axis=1)
        kk = kb_ref[0, pl.ds(off, kblk), :]
        for g in range(KV_HEADS):
            s = jnp.dot(kk[:, g * HEAD_DIM:(g + 1) * HEAD_DIM], qg[g],
                        preferred_element_type=F32) + bias
            s_scr[buf, g] = s
            m_blk = jnp.max(_fold_rows(s, jnp.maximum), axis=0, keepdims=True)
            m_old = m_scr[g]
            m_new = jnp.maximum(m_old, m_blk)
            a_scr[g] = jnp.exp2(m_old - m_new)
            m_scr[g] = m_new

    def value_phase(i, buf):
        off = pl.multiple_of(i * kblk, kblk)
        vv = vt_ref[0, :, pl.ds(off, kblk)]
        for g in range(KV_HEADS):
            p = jnp.exp2(s_scr[buf, g] - m_scr[g]).astype(BF16)
            vg = jnp.concatenate([vv[g * HEAD_DIM:(g + 1) * HEAD_DIM, :], ones], axis=0)
            acc_scr[g] = a_scr[g] * acc_scr[g] + jnp.dot(vg, p, preferred_element_type=F32)

    score_phase(0, 0)

    def attn_pair(k, carry):
        i = 2 * k
        value_phase(i, 0)
        score_phase(i + 1, 1)
        value_phase(i + 1, 1)
        score_phase(jnp.minimum(i + 2, nblk - 1), 0)
        return carry

    lax.fori_loop(0, nblk // 2, attn_pair, 0)

    @pl.when(nblk % 2 == 1)
    def _():
        value_phase(nblk - 1, 0)

    heads = []
    for g in range(KV_HEADS):
        acc = acc_scr[g]
        og = acc[:HEAD_DIM] / acc[HEAD_DIM:HEAD_DIM + 1]
        heads += [og[:, r * tq:(r + 1) * tq] for r in range(rep)]
    o_ref[0] = jnp.concatenate(heads, axis=0).T.astype(BF16)


def _dsa(qt, qit, wit, kb, vt, kib, *, tq, pos0, topk):
    b, _, t = qt.shape
    lp = kb.shape[1]
    kblk = DSA_KEY_BLOCK
    assert lp % kblk == 0 and t % tq == 0 and tq % LANES == 0
    kern = functools.partial(_dsa_kernel, tq=tq, kblk=kblk, pos0=pos0, topk=topk)
    rep = ATT_HEADS // KV_HEADS
    qmap = lambda bi, j: (bi, 0, j)
    kmap = lambda bi, j: (bi, 0, 0)
    return pl.pallas_call(
        kern,
        grid=(b, t // tq),
        in_specs=[pl.BlockSpec((1, ATT_WIDTH, tq), qmap),
                  pl.BlockSpec((1, IDX_HEADS * IDX_DIM, tq), qmap),
                  pl.BlockSpec((1, 8, tq), qmap),
                  pl.BlockSpec((1, lp, KV_WIDTH), kmap),
                  pl.BlockSpec((1, KV_WIDTH, lp), kmap),
                  pl.BlockSpec((1, lp, IDX_DIM), kmap)],
        out_specs=pl.BlockSpec((1, tq, ATT_WIDTH), lambda bi, j: (bi, j, 0)),
        out_shape=jax.ShapeDtypeStruct((b, t, ATT_WIDTH), BF16),
        scratch_shapes=[pltpu.VMEM((lp // kblk, kblk, tq), I32),
                        pltpu.VMEM((lp // kblk, kblk, tq), I32),
                        pltpu.VMEM((lp // kblk, kblk, tq), F32),
                        pltpu.VMEM((2, KV_HEADS, kblk, rep * tq), F32),
                        pltpu.VMEM((KV_HEADS, HEAD_DIM + 16, rep * tq), F32),
                        pltpu.VMEM((KV_HEADS, 1, rep * tq), F32),
                        pltpu.VMEM((KV_HEADS, 1, rep * tq), F32)],
        compiler_params=_params(("parallel", "arbitrary")),
        name="dsa",
    )(qt, qit, wit, kb, vt, kib)


def _gmlp_kernel(x_ref, wu_ref, wv_ref, g_ref, b_ref, ws_ref, bst_ref, o_ref, v_ref, *, tm, lc):
    xb = x_ref[...].astype(BF16)
    u = _gelu_tanh(jnp.dot(xb, wu_ref[...], preferred_element_type=F32))
    v = _layer_norm(_gelu_tanh(jnp.dot(xb, wv_ref[...], preferred_element_type=F32)),
                    g_ref[...], b_ref[...])
    v_ref[...] = v
    vb = v.astype(BF16)
    r = lax.broadcasted_iota(I32, (lc, lc), 0)
    c = lax.broadcasted_iota(I32, (lc, lc), 1)
    for g in range(SG_GROUPS):
        w = jnp.where(r >= c, ws_ref[g], 0.0).astype(BF16)
        bias = bst_ref[:, g:g + 1]
        cs = slice(g * SG_GW, (g + 1) * SG_GW)
        for n in range(tm // lc):
            rs = slice(n * lc, (n + 1) * lc)
            mixed = jnp.dot(w, vb[rs, cs], preferred_element_type=F32) + bias
            o_ref[rs, cs] = (u[rs, cs] * mixed).astype(BF16)


def _gmlp(x, wu, wv, ln_g, ln_b, ws, bst, *, tm, lc):
    n = x.shape[0]
    row = lambda i: (i, 0)
    full2 = lambda i: (0, 0)
    return pl.pallas_call(
        functools.partial(_gmlp_kernel, tm=tm, lc=lc),
        grid=(n // tm,),
        in_specs=[pl.BlockSpec((tm, D_MODEL), row),
                  pl.BlockSpec((D_MODEL, SG_WIDTH), full2),
                  pl.BlockSpec((D_MODEL, SG_WIDTH), full2),
                  pl.BlockSpec((1, SG_WIDTH), full2),
                  pl.BlockSpec((1, SG_WIDTH), full2),
                  pl.BlockSpec((SG_GROUPS, lc, lc), lambda i: (0, 0, 0)),
                  pl.BlockSpec((lc, SG_GROUPS), full2)],
        out_specs=[pl.BlockSpec((tm, SG_WIDTH), row), pl.BlockSpec((tm, SG_WIDTH), row)],
        out_shape=[jax.ShapeDtypeStruct((n, SG_WIDTH), BF16),
                   jax.ShapeDtypeStruct((n, SG_WIDTH), F32)],
        compiler_params=_params(("parallel",)),
        name="gmlp",
    )(x, wu, wv, ln_g, ln_b, ws, bst)


def _hgrn_kernel(x_ref, w_ref, loglb_ref, log1mlb_ref, omlb_ref, ng_ref, s0_ref, c_ref, sout_ref,
                 q_scr, k_scr, v_scr, lf_scr, o_scr, st_scr, *, tc):
    t = pl.program_id(1)

    @pl.when(t == 0)
    def _():
        for h in range(HG_HEADS):
            st_scr[h] = s0_ref[0, h].T

    xb = x_ref[0].astype(BF16)
    z = jnp.dot(xb, w_ref[...], preferred_element_type=F32)
    hq = z[:, 0:HG_WIDTH]
    hf = z[:, HG_WIDTH:2 * HG_WIDTH]
    hg = z[:, 3 * HG_WIDTH:4 * HG_WIDTH]
    q_scr[...] = hq * _sigmoid(hq) * HG_DK ** -0.5
    u = jnp.exp(-jnp.abs(hf))
    w = 1.0 + u
    log_sig = jnp.minimum(hf, 0.0) - jnp.log(w)
    y = log1mlb_ref[...] + log_sig
    a = loglb_ref[...]
    lf_scr[...] = jnp.maximum(a, y) + jnp.log(1.0 + jnp.exp(-jnp.abs(a - y)))
    k_scr[...] = omlb_ref[...] * (jnp.where(hf >= 0.0, u, 1.0) / w)
    v_scr[...] = z[:, 2 * HG_WIDTH:3 * HG_WIDTH]

    ch = min(tc, CHUNK)
    ri = lax.broadcasted_iota(I32, (tc, tc), 0)
    ci = lax.broadcasted_iota(I32, (tc, tc), 1)
    same = (ri // ch) == (ci // ch)
    causal = same & (ri >= ci)
    lf = lf_scr[...]
    lf_hi = lf.astype(BF16)
    lf_r = lf - lf_hi.astype(F32)
    lf_mid = lf_r.astype(BF16)
    lf_lo = (lf_r - lf_mid.astype(F32)).astype(BF16)
    cmask = jnp.where(causal, 1.0, 0.0).astype(BF16)
    bcum_all = (jnp.dot(cmask, lf_hi, preferred_element_type=F32)
                + jnp.dot(cmask, lf_mid, preferred_element_type=F32)
                + jnp.dot(cmask, lf_lo, preferred_element_type=F32))
    factored = jnp.min(bcum_all) > -HG_FACTOR_RANGE

    @pl.when(factored)
    def _():
        btot = jnp.concatenate(
            [jnp.broadcast_to(bcum_all[(n + 1) * ch - 1:(n + 1) * ch, :], (ch, HG_WIDTH))
             for n in range(tc // ch)], axis=0)
        qa = q_scr[...]
        ka = k_scr[...]
        qd = (qa * jnp.exp(bcum_all)).astype(BF16)
        ki = (ka * jnp.exp(-bcum_all)).astype(BF16)
        kd = (ka * jnp.exp(btot - bcum_all)).astype(BF16)
        va = v_scr[...].astype(BF16)
        for h in range(HG_HEADS):
            cs = slice(h * HG_DK, (h + 1) * HG_DK)
            att = lax.dot_general(qd[:, cs], ki[:, cs], _NT, preferred_element_type=F32)
            att = jnp.where(causal, att, 0.0).astype(BF16)
            o = jnp.dot(att, va[:, cs], preferred_element_type=F32)
            st = st_scr[h]
            for n in range(tc // ch):
                rs = slice(n * ch, (n + 1) * ch)
                o_scr[rs, cs] = o[rs] + lax.dot_general(qd[rs, cs], st.astype(BF16), _NT,
                                                        preferred_element_type=F32)
                upd = lax.dot_general(va[rs, cs], kd[rs, cs], _TN, preferred_element_type=F32)
                st = jnp.exp(btot[n * ch:n * ch + 1, cs]) * st + upd
            st_scr[h] = st

    nb = HG_BLOCK
    tri = jnp.where(lax.broadcasted_iota(I32, (nb, nb), 0) >= lax.broadcasted_iota(I32, (nb, nb), 1),
                    1.0, 0.0).astype(F32)
    trow = lax.broadcasted_iota(I32, (nb, 1), 0)

    def block(i, carry):
        r0 = pl.multiple_of(i * nb, nb)
        rows = pl.ds(r0, nb)
        for h in range(HG_HEADS):
            cs = slice(h * HG_DK, (h + 1) * HG_DK)
            bcum = jnp.dot(tri, lf_scr[rows, cs], preferred_element_type=F32,
                           precision=lax.Precision.HIGHEST)
            qb = q_scr[rows, cs]
            kb = k_scr[rows, cs]
            vb = v_scr[rows, cs]
            o = jnp.zeros((nb, HG_DV), F32)
            for s in range(nb):
                e = jnp.exp(jnp.where(trow >= s, bcum - bcum[s:s + 1, :], -jnp.inf))
                a_ts = jnp.sum(qb * kb[s:s + 1, :] * e, axis=1, keepdims=True)
                o = o + a_ts * vb[s:s + 1, :]
            st = st_scr[h]
            qd = (qb * jnp.exp(bcum)).astype(BF16)
            o = o + lax.dot_general(qd, st.astype(BF16), _NT, preferred_element_type=F32)
            blast = bcum[nb - 1:nb, :]
            kd = (kb * jnp.exp(blast - bcum)).astype(BF16)
            upd = lax.dot_general(vb.astype(BF16), kd, _TN, preferred_element_type=F32)
            st_scr[h] = jnp.exp(blast) * st + upd
            o_scr[rows, cs] = o
        return carry

    @pl.when(jnp.logical_not(factored))
    def _():
        lax.fori_loop(0, tc // nb, block, 0)

    for h in range(HG_HEADS):
        cs = slice(h * HG_DV, (h + 1) * HG_DV)
        o = o_scr[:, cs]
        o = o * lax.rsqrt(jnp.mean(o * o, axis=-1, keepdims=True) + LN_EPS)
        g = hg[:, cs]
        c_ref[0, :, cs] = (o * ng_ref[:, cs] * (g * _sigmoid(g))).astype(BF16)

    @pl.when(t == pl.num_programs(1) - 1)
    def _():
        for h in range(HG_HEADS):
            sout_ref[0, h] = st_scr[h].T


def _hgrn(x, w, loglb, log1mlb, omlb, ng, s0, *, tc):
    b, t, _ = x.shape
    xmap = lambda bi, j: (bi, j, 0)
    vec = pl.BlockSpec((1, HG_WIDTH), lambda bi, j: (0, 0))
    smap = lambda bi, j: (bi, 0, 0, 0)
    return pl.pallas_call(
        functools.partial(_hgrn_kernel, tc=tc),
        grid=(b, t // tc),
        in_specs=[pl.BlockSpec((1, tc, D_MODEL), xmap),
                  pl.BlockSpec((D_MODEL, 4 * HG_WIDTH), lambda bi, j: (0, 0)),
                  vec, vec, vec, vec,
                  pl.BlockSpec((1, HG_HEADS, HG_DK, HG_DV), smap)],
        out_specs=[pl.BlockSpec((1, tc, HG_WIDTH), xmap),
                   pl.BlockSpec((1, HG_HEADS, HG_DK, HG_DV), smap)],
        out_shape=[jax.ShapeDtypeStruct((b, t, HG_WIDTH), BF16),
                   jax.ShapeDtypeStruct((b, HG_HEADS, HG_DK, HG_DV), F32)],
        scratch_shapes=[pltpu.VMEM((tc, HG_WIDTH), F32)] * 5
                       + [pltpu.VMEM((HG_HEADS, HG_DV, HG_DK), F32)],
        compiler_params=_params(("parallel", "arbitrary")),
        name="hgrn",
    )(x, w, loglb, log1mlb, omlb, ng, s0)


def _merge_kernel(x_ref, a_ref, b_ref, c_ref, wg_ref, wa_ref, wb_ref, wc_ref, wo_ref,
                  g_ref, bt_ref, y_ref):
    x = x_ref[...]
    xb = x.astype(BF16)
    m = None
    for i, (br, w) in enumerate(((a_ref, wa_ref), (b_ref, wb_ref), (c_ref, wc_ref))):
        gate = _sigmoid(jnp.dot(xb, wg_ref[:, i * D_MODEL:(i + 1) * D_MODEL],
                                preferred_element_type=F32))
        term = gate * jnp.dot(br[...], w[...], preferred_element_type=F32)
        m = term if m is None else m + term
    mo = jnp.dot(m.astype(BF16), wo_ref[...], preferred_element_type=F32)
    y_ref[...] = _layer_norm(ALPHA * x + mo, g_ref[...], bt_ref[...])


def _merge(x, a, b, c, wg, wa, wb, wc, wo, ln_g, ln_b, *, tm):
    n = x.shape[0]
    row = lambda i: (i, 0)
    full = lambda i: (0, 0)
    br = pl.BlockSpec((tm, ATT_WIDTH), row)
    bw = pl.BlockSpec((ATT_WIDTH, D_MODEL), full)
    vec = pl.BlockSpec((1, D_MODEL), full)
    return pl.pallas_call(
        _merge_kernel,
        grid=(n // tm,),
        in_specs=[pl.BlockSpec((tm, D_MODEL), row), br, br, br,
                  pl.BlockSpec((D_MODEL, N_BRANCH * D_MODEL), full), bw, bw, bw,
                  pl.BlockSpec((D_MODEL, D_MODEL), full), vec, vec],
        out_specs=pl.BlockSpec((tm, D_MODEL), row),
        out_shape=jax.ShapeDtypeStruct((n, D_MODEL), F32),
        compiler_params=_params(("parallel",)),
        name="merge",
    )(x, a, b, c, wg, wa, wb, wc, wo, ln_g, ln_b)


def _route(logits):
    ex = jnp.exp(logits - jnp.max(logits, axis=0, keepdims=True))
    probs = ex / jnp.sum(ex, axis=0, keepdims=True)
    p = [[probs[g * EXP_PER_GROUP + k:g * EXP_PER_GROUP + k + 1, :] for k in range(EXP_PER_GROUP)]
         for g in range(N_GROUPS)]
    score = []
    for g in range(N_GROUPS):
        best = None
        for k1 in range(EXP_PER_GROUP):
            for k2 in range(k1 + 1, EXP_PER_GROUP):
                pair = p[g][k1] + p[g][k2]
                best = pair if best is None else jnp.maximum(best, pair)
        score.append(best)
    gsel = jnp.zeros(score[0].shape, I32)
    top = score[0]
    for g in range(1, N_GROUPS):
        better = score[g] > top
        top = jnp.where(better, score[g], top)
        gsel = jnp.where(better, g, gsel)
    val = []
    for k in range(EXP_PER_GROUP):
        v = p[0][k]
        for g in range(1, N_GROUPS):
            v = jnp.where(gsel == g, p[g][k], v)
        val.append(v)
    v1, i1 = val[0], jnp.zeros(gsel.shape, I32)
    for k in range(1, EXP_PER_GROUP):
        better = val[k] > v1
        v1 = jnp.where(better, val[k], v1)
        i1 = jnp.where(better, k, i1)
    v2, i2 = jnp.full(v1.shape, -1.0, F32), jnp.zeros(gsel.shape, I32)
    for k in range(EXP_PER_GROUP):
        better = (i1 != k) & (val[k] > v2)
        v2 = jnp.where(better, val[k], v2)
        i2 = jnp.where(better, k, i2)
    den = v1 + v2
    w1, w2 = v1 / den, v2 / den
    out = []
    for g in range(N_GROUPS):
        rows = [jnp.where(gsel == g, jnp.where(i1 == k, w1, jnp.where(i2 == k, w2, 0.0)), 0.0)
                for k in range(EXP_PER_GROUP)]
        out.append(jnp.concatenate(rows, axis=0))
    return out


def _moe_kernel(x_ref, wr_ref, br_ref, wg_ref, wu_ref, wd_ref, g_ref, bt_ref, y_ref,
                xb_scr, gate_scr, acc_scr, *, tm):
    grp = pl.program_id(1)

    @pl.when(grp == 0)
    def _():
        xb = x_ref[...].astype(BF16)
        xb_scr[...] = xb
        logits = lax.dot_general(wr_ref[...], xb, _NT, preferred_element_type=F32) + br_ref[...]
        gates = _route(logits)
        pad = jnp.zeros((LANES - EXP_PER_GROUP, tm), F32)
        for g in range(N_GROUPS):
            gate_scr[g] = jnp.concatenate([gates[g], pad], axis=0).T
        acc_scr[...] = jnp.zeros(acc_scr.shape, F32)

    xb = xb_scr[...]
    gate = gate_scr[grp]
    acc = acc_scr[...]
    for k in range(EXP_PER_GROUP):
        h = jnp.dot(xb, wg_ref[k], preferred_element_type=F32)
        up = jnp.dot(xb, wu_ref[k], preferred_element_type=F32)
        act = h * _sigmoid(h) * up * gate[:, k:k + 1]
        acc = acc + jnp.dot(act.astype(BF16), wd_ref[k], preferred_element_type=F32)
    acc_scr[...] = acc

    @pl.when(grp == N_GROUPS - 1)
    def _():
        y_ref[...] = _layer_norm(ALPHA * x_ref[...] + acc, g_ref[...], bt_ref[...])


def _moe(x, wr_t, br, wg, wu, wd, ln_g, ln_b, *, tm):
    n = x.shape[0]
    row = lambda i, g: (i, 0)
    full = lambda i, g: (0, 0)
    vec = pl.BlockSpec((1, D_MODEL), full)
    return pl.pallas_call(
        functools.partial(_moe_kernel, tm=tm),
        grid=(n // tm, N_GROUPS),
        in_specs=[pl.BlockSpec((tm, D_MODEL), row),
                  pl.BlockSpec((N_EXPERTS, D_MODEL), full),
                  pl.BlockSpec((N_EXPERTS, 1), full),
                  pl.BlockSpec((EXP_PER_GROUP, D_MODEL, D_FF), lambda i, g: (g, 0, 0)),
                  pl.BlockSpec((EXP_PER_GROUP, D_MODEL, D_FF), lambda i, g: (g, 0, 0)),
                  pl.BlockSpec((EXP_PER_GROUP, D_FF, D_MODEL), lambda i, g: (g, 0, 0)),
                  vec, vec],
        out_specs=pl.BlockSpec((tm, D_MODEL), row),
        out_shape=jax.ShapeDtypeStruct((n, D_MODEL), F32),
        scratch_shapes=[pltpu.VMEM((tm, D_MODEL), BF16),
                        pltpu.VMEM((N_GROUPS, tm, LANES), F32),
                        pltpu.VMEM((tm, D_MODEL), F32)],
        compiler_params=_params(("parallel", "arbitrary")),
        name="moe",
    )(x, wr_t, br, wg, wu, wd, ln_g, ln_b)


WPREP_ROWS = 128


def _split_w_in_kernel(w_ref, wn_ref, wt_ref, wu_ref, wv_ref, whg_ref, wg_ref):
    w = w_ref[0]
    offs = [0]
    for n in SPLIT_SIZES:
        offs.append(offs[-1] + n)
    seg = lambda i, j=None: w[:, offs[i]:offs[i + 1 if j is None else j]]
    zeros = lambda c: jnp.zeros((w.shape[0], c), w.dtype)
    wn_ref[0] = jnp.concatenate([seg(1), seg(2), seg(4), zeros(LANES - IDX_DIM)], axis=1).astype(BF16)
    wt = jnp.concatenate([seg(0), seg(3), seg(2), seg(5), zeros(ATT_T_PAD - ATT_T_ROWS + 8 - IDX_HEADS)],
                         axis=1)
    for c in range(ATT_T_PAD // LANES):
        rows = min(LANES, ATT_T_ROWS - c * LANES)
        wt_ref[0, c * LANES:c * LANES + rows, :] = wt[:, c * LANES:(c + 1) * LANES].T[:rows].astype(BF16)
    wu_ref[0] = seg(6).astype(BF16)
    wv_ref[0] = seg(7).astype(BF16)
    whg_ref[0] = seg(8, 12).astype(BF16)
    wg_ref[0] = seg(12).astype(BF16)


def _split_w_in(w_in):
    depth, d, ncol = w_in.shape
    tr = WPREP_ROWS
    cols = (ATT_N_COLS, None, SG_WIDTH, SG_WIDTH, 4 * HG_WIDTH, N_BRANCH * D_MODEL)
    shapes = [(depth, ATT_T_ROWS, d) if c is None else (depth, d, c) for c in cols]
    specs = [pl.BlockSpec((1, ATT_T_ROWS, tr), lambda l, i: (l, 0, i)) if c is None
             else pl.BlockSpec((1, tr, c), lambda l, i: (l, i, 0)) for c in cols]
    return pl.pallas_call(
        _split_w_in_kernel,
        grid=(depth, d // tr),
        in_specs=[pl.BlockSpec((1, tr, ncol), lambda l, i: (l, i, 0))],
        out_specs=specs,
        out_shape=[jax.ShapeDtypeStruct(sh, BF16) for sh in shapes],
        compiler_params=_params(("parallel", "parallel")),
        name="split_w_in",
    )(w_in)


def _row_tile(n, want):
    while n % want:
        want //= 2
    return want


def _layer(x, bsz, t, lw, *, pos0, topk, cache, s0):
    n = bsz * t
    lc = min(t, SG_LEN)
    k, v, ki, kb, kib, qt, qit, vt, wit = _attn_proj(x, lw["w_att_n"], lw["w_att_t"], bsz, t,
                                                     _row_tile(t, 512))
    kb3 = kb.reshape(bsz, t, KV_WIDTH)
    kib3 = kib.reshape(bsz, t, IDX_DIM)
    if cache is not None:
        ck, cvt, cki = cache
        kb3 = jnp.concatenate([ck, kb3], axis=1)
        kib3 = jnp.concatenate([cki, kib3], axis=1)
        vt = jnp.concatenate([cvt, vt], axis=2)
    ltot = kb3.shape[1]
    lp = -(-ltot // DSA_KEY_BLOCK) * DSA_KEY_BLOCK
    tq = DSA_QUERY_TILE if t % DSA_QUERY_TILE == 0 else LANES
    tp = -(-t // tq) * tq
    padq = lambda a: jnp.pad(a, ((0, 0), (0, 0), (0, tp - t)))
    a = _dsa(padq(qt), padq(qit), padq(wit), jnp.pad(kb3, ((0, 0), (0, lp - ltot), (0, 0))),
             jnp.pad(vt, ((0, 0), (0, 0), (0, lp - ltot))),
             jnp.pad(kib3, ((0, 0), (0, lp - ltot), (0, 0))), tq=tq, pos0=pos0, topk=topk)[:, :t]
    b, v_gm = _gmlp(x, lw["w_u"], lw["w_v"], lw["ln_sg_g"], lw["ln_sg_b"],
                    lw["w_sg"][:, :lc, :lc], lw["b_sg"][:, :lc].T, tm=_row_tile(n, 256), lc=lc)
    c, s_new = _hgrn(x.reshape(bsz, t, D_MODEL), lw["w_hg"], lw["log_lb"], lw["log1m_lb"],
                     lw["om_lb"], lw["hg_norm_g"], s0, tc=min(t, 256))
    x1 = _merge(x, a.reshape(n, ATT_WIDTH), b, c.reshape(n, HG_WIDTH), lw["w_gates"],
                lw["w_branch_a"], lw["w_branch_b"], lw["w_branch_c"], lw["w_out"],
                lw["ln1_g"], lw["ln1_b"], tm=_row_tile(n, 512))
    x2 = _moe(x1, lw["w_router_t"], lw["b_router"], lw["w_exp_gate"], lw["w_exp_up"],
              lw["w_exp_down"], lw["ln2_g"], lw["ln2_b"], tm=_row_tile(n, 1024))
    return x2, k, v, ki, s_new, v_gm


def kernel(x_prompt, x_sample, cache_k, cache_v, cache_kidx, state_hgrn, w_in, w_sg, b_sg, ln_sg_g, ln_sg_b, hg_lb_logits, hg_norm_g, w_branch_a, w_branch_b, w_branch_c, w_out, ln1_g, ln1_b, w_router, b_router, w_exp_gate, w_exp_up, w_exp_down, ln2_g, ln2_b):
    bp, sp, _ = x_prompt.shape
    bs, ss, _ = x_sample.shape
    past = cache_k.shape[2]
    topk_p = min(TOPK_MAX, sp // 4)
    topk_s = min(TOPK_MAX, (past + ss) // 4)

    lb_all = jnp.cumsum(jax.nn.softmax(hg_lb_logits.astype(F32), axis=0), axis=0)
    lb_all = lb_all - lb_all[0:1]
    vec = lambda a: a.reshape(1, -1)

    xp = x_prompt.reshape(bp * sp, D_MODEL)
    xs = x_sample.reshape(bs * ss, D_MODEL)
    s0_p = jnp.zeros((bp, HG_HEADS, HG_DK, HG_DV), F32)
    outs_p, outs_s = [], []
    w_att_n, w_att_t, w_u, w_v, w_hg, w_gates = _split_w_in(w_in)
    for l in range(DEPTH):
        lw = dict(
            w_att_n=w_att_n[l], w_att_t=w_att_t[l], w_u=w_u[l], w_v=w_v[l], w_hg=w_hg[l],
            w_gates=w_gates[l],
            ln_sg_g=vec(ln_sg_g[l]), ln_sg_b=vec(ln_sg_b[l]), w_sg=w_sg[l], b_sg=b_sg[l],
            log_lb=vec(jnp.log(lb_all[l])), log1m_lb=vec(jnp.log1p(-lb_all[l])),
            om_lb=vec(1.0 - lb_all[l]), hg_norm_g=vec(hg_norm_g[l].astype(F32)),
            w_branch_a=w_branch_a[l].astype(BF16), w_branch_b=w_branch_b[l].astype(BF16),
            w_branch_c=w_branch_c[l].astype(BF16), w_out=w_out[l].astype(BF16),
            ln1_g=vec(ln1_g[l]), ln1_b=vec(ln1_b[l]),
            w_router_t=w_router.T.astype(BF16), b_router=b_router.astype(F32).reshape(-1, 1),
            w_exp_gate=w_exp_gate[l].astype(BF16), w_exp_up=w_exp_up[l].astype(BF16),
            w_exp_down=w_exp_down[l].astype(BF16),
            ln2_g=vec(ln2_g[l]), ln2_b=vec(ln2_b[l]))
        xp, k, v, ki, s_new, _ = _layer(xp, bp, sp, lw, pos0=0, topk=topk_p, cache=None, s0=s0_p)
        outs_p.append((k.reshape(bp, sp, KV_HEADS, HEAD_DIM), v.reshape(bp, sp, KV_HEADS, HEAD_DIM),
                       ki.reshape(bp, sp, IDX_DIM), s_new))
        cache = (cache_k[l].reshape(bs, past, KV_WIDTH).astype(BF16),
                 jnp.swapaxes(cache_v[l].reshape(bs, past, KV_WIDTH), 1, 2).astype(BF16),
                 cache_kidx[l].astype(BF16))
        xs, k, v, ki, s_new, v_gm = _layer(xs, bs, ss, lw, pos0=past, topk=topk_s, cache=cache,
                                           s0=state_hgrn[l].astype(F32))
        outs_s.append((k.reshape(bs, ss, KV_HEADS, HEAD_DIM), v.reshape(bs, ss, KV_HEADS, HEAD_DIM),
                       ki.reshape(bs, ss, IDX_DIM), s_new, v_gm.reshape(bs, ss, SG_WIDTH)))

    stack = lambda rows, i: jnp.stack([r[i] for r in rows])
    return (xp.reshape(bp, sp, D_MODEL), xs.reshape(bs, ss, D_MODEL),
            stack(outs_p, 0), stack(outs_p, 1), stack(outs_p, 2), stack(outs_p, 3),
            stack(outs_s, 0), stack(outs_s, 1), stack(outs_s, 2), stack(outs_s, 3),
            stack(outs_s, 4))
```

```python
import functools

import jax
import jax.numpy as jnp
from jax import lax
from jax.experimental import pallas as pl
from jax.experimental.pallas import tpu as pltpu

F32 = jnp.float32
BF16 = jnp.bfloat16
I32 = jnp.int32

D_MODEL = 1024
DEPTH = 4
CHUNK = 64
ATT_HEADS = 8
KV_HEADS = 2
HEAD_DIM = 64
ATT_WIDTH = ATT_HEADS * HEAD_DIM
KV_WIDTH = KV_HEADS * HEAD_DIM
IDX_HEADS = 4
IDX_DIM = 64
TOPK_MAX = 256
SG_LEN = 128
SG_GROUPS = 4
SG_WIDTH = 512
SG_GW = SG_WIDTH // SG_GROUPS
HG_HEADS = 4
HG_DK = 128
HG_DV = 128
HG_WIDTH = HG_HEADS * HG_DV
HG_BLOCK = 16
HG_FACTOR_RANGE = 80.0
N_BRANCH = 3
N_EXPERTS = 16
N_GROUPS = 4
EXP_PER_GROUP = 4
D_FF = 256
ALPHA = (2 * DEPTH) ** 0.25
LN_EPS = 1e-5

SPLIT_SIZES = (ATT_WIDTH, KV_WIDTH, KV_WIDTH, IDX_HEADS * IDX_DIM, IDX_DIM, IDX_HEADS,
               SG_WIDTH, SG_WIDTH, HG_HEADS * HG_DK, HG_HEADS * HG_DK, HG_WIDTH, HG_WIDTH,
               N_BRANCH * D_MODEL)

LANES = 128
INT_MIN = -2 ** 31
NEG_BIG = -1e30
LOG2_E = 1.4426950408889634
DSA_KEY_BLOCK = 512
DSA_QUERY_TILE = 512
VMEM_LIMIT = 56 * 1024 * 1024

_NT = (((1,), (1,)), ((), ()))
_TN = (((0,), (0,)), ((), ()))


def _params(sem):
    return pltpu.CompilerParams(dimension_semantics=sem, vmem_limit_bytes=VMEM_LIMIT)


def _layer_norm(x, g, b):
    mu = jnp.mean(x, axis=-1, keepdims=True)
    d = x - mu
    var = jnp.mean(d * d, axis=-1, keepdims=True)
    return d * lax.rsqrt(var + LN_EPS) * g + b


def _gelu_tanh(x):
    return 0.5 * x * (1.0 + jnp.tanh(0.7978845608028654 * (x + 0.044715 * (x * x * x))))


def _sigmoid(x):
    return 0.5 * jnp.tanh(0.5 * x) + 0.5


def _fold_rows(x, op):
    parts = [x[i:i + 8] for i in range(0, x.shape[0], 8)]
    while len(parts) > 1:
        parts = [op(parts[i], parts[i + 1]) for i in range(0, len(parts), 2)]
    return parts[0]


ATT_T_ROWS = ATT_WIDTH + IDX_HEADS * IDX_DIM + KV_WIDTH + 8
ATT_T_PAD = -(-ATT_T_ROWS // LANES) * LANES
ATT_N_COLS = 3 * LANES


def _attn_proj_kernel(x_ref, wn_ref, wt_ref, k_ref, v_ref, ki_ref, kb_ref, kib_ref,
                      qt_ref, qit_ref, vt_ref, wit_ref):
    xb = x_ref[...].astype(BF16)
    z = jnp.dot(xb, wn_ref[...], preferred_element_type=F32)
    k = z[:, 0:KV_WIDTH]
    ki = z[:, 2 * KV_WIDTH:2 * KV_WIDTH + IDX_DIM]
    k_ref[...] = k
    v_ref[...] = z[:, KV_WIDTH:2 * KV_WIDTH]
    ki_ref[...] = ki
    kb_ref[...] = k.astype(BF16)
    kib_ref[...] = ki.astype(BF16)
    zt = lax.dot_general(wt_ref[...], xb, _NT, preferred_element_type=F32)
    o = 0
    qt_ref[0] = (zt[o:o + ATT_WIDTH] * (HEAD_DIM ** -0.5 * LOG2_E)).astype(BF16)
    o += ATT_WIDTH
    qit_ref[0] = (zt[o:o + IDX_HEADS * IDX_DIM] * IDX_DIM ** -0.5).astype(BF16)
    o += IDX_HEADS * IDX_DIM
    vt_ref[0] = zt[o:o + KV_WIDTH].astype(BF16)
    o += KV_WIDTH
    wit_ref[0] = zt[o:o + 8] * IDX_HEADS ** -0.5


def _attn_proj(x, wn, wt, bsz, t, tm):
    n = bsz * t
    per = t // tm
    row = lambda i: (i, 0)
    tmap = lambda i: (i // per, 0, i % per)
    nat = ((KV_WIDTH, F32), (KV_WIDTH, F32), (IDX_DIM, F32), (KV_WIDTH, BF16), (IDX_DIM, BF16))
    tr = ((ATT_WIDTH, BF16), (IDX_HEADS * IDX_DIM, BF16), (KV_WIDTH, BF16), (8, F32))
    return pl.pallas_call(
        _attn_proj_kernel,
        grid=(n // tm,),
        in_specs=[pl.BlockSpec((tm, D_MODEL), row),
                  pl.BlockSpec((D_MODEL, ATT_N_COLS), lambda i: (0, 0)),
                  pl.BlockSpec((ATT_T_ROWS, D_MODEL), lambda i: (0, 0))],
        out_specs=[pl.BlockSpec((tm, c), row) for c, _ in nat]
                  + [pl.BlockSpec((1, c, tm), tmap) for c, _ in tr],
        out_shape=[jax.ShapeDtypeStruct((n, c), d) for c, d in nat]
                  + [jax.ShapeDtypeStruct((bsz, c, t), d) for c, d in tr],
        compiler_params=_params(("parallel",)),
        name="attn_proj",
    )(x, wn, wt)


def _dsa_kernel(qt_ref, qit_ref, wit_ref, kb_ref, vt_ref, kib_ref, o_ref,
                key_scr, sort_scr, bias_scr, s_scr, acc_scr, m_scr, a_scr,
                *, tq, kblk, pos0, topk):
    j = pl.program_id(1)
    qpos0 = pos0 + j * tq
    qcol = lax.broadcasted_iota(I32, (1, tq), 1)
    chunk_end = ((qpos0 + qcol) // CHUNK + 1) * CHUNK
    lvis = ((qpos0 + tq - 1) // CHUNK + 1) * CHUNK
    nblk = (lvis + kblk - 1) // kblk
    krow = lax.broadcasted_iota(I32, (kblk, tq), 0)

    wi = wit_ref[0]
    qit = qit_ref[0]

    def score_block(i, carry):
        off = pl.multiple_of(i * kblk, kblk)
        kib = kib_ref[0, pl.ds(off, kblk), :]
        sc = jnp.zeros((kblk, tq), F32)
        for h in range(IDX_HEADS):
            raw = jnp.dot(kib, qit[h * IDX_DIM:(h + 1) * IDX_DIM, :], preferred_element_type=F32)
            sc = sc + jnp.maximum(raw, 0.0) * wi[h:h + 1, :]
        bits = pltpu.bitcast(sc, I32)
        key = bits ^ ((bits >> 31) & 0x7FFFFFFF)
        key = jnp.where(off + krow < chunk_end, key, INT_MIN)
        key_scr[i] = key
        for g in range(kblk // 32):
            a, b, c, d = (key[g * 32 + 8 * r:g * 32 + 8 * r + 8] for r in range(4))
            a, b = jnp.maximum(a, b), jnp.minimum(a, b)
            c, d = jnp.maximum(c, d), jnp.minimum(c, d)
            a, c = jnp.maximum(a, c), jnp.minimum(a, c)
            b, d = jnp.maximum(b, d), jnp.minimum(b, d)
            b, c = jnp.maximum(b, c), jnp.minimum(b, c)
            sort_scr[i, g * 32:(g + 1) * 32] = jnp.concatenate([a, b, c, d], axis=0)
        return carry

    lax.fori_loop(0, nblk, score_block, 0)

    def count(hits):
        def body(i, acc):
            return acc + _fold_rows(hits(i, key_scr[i]), jnp.add)
        acc = lax.fori_loop(0, nblk, body, jnp.zeros((8, tq), F32))
        return jnp.sum(acc, axis=0, keepdims=True)

    def key_bit(i, thr):
        cand = thr + jnp.left_shift(jnp.int32(1), 31 - i)

        def body(b, accs):
            ks = sort_scr[b]
            accs = list(accs)
            for g in range(kblk // 32):
                s0, s1, s2, s3 = (ks[g * 32 + 8 * r:g * 32 + 8 * r + 8] for r in range(4))
                n = jnp.where(s3 >= cand, 4.0, jnp.where(s2 >= cand, 3.0, jnp.where(
                    s1 >= cand, 2.0, jnp.where(s0 >= cand, 1.0, 0.0))))
                accs[g % 4] = accs[g % 4] + n
            return tuple(accs)

        accs = lax.fori_loop(0, nblk, body, tuple(jnp.zeros((8, tq), F32) for _ in range(4)))
        cnt = jnp.sum((accs[0] + accs[1]) + (accs[2] + accs[3]), axis=0, keepdims=True)
        return jnp.where(cnt >= topk, cand, thr)

    thr = lax.fori_loop(0, 32, key_bit, jnp.full((1, tq), INT_MIN, I32))

    need = topk - count(lambda _, kb: jnp.where(kb > thr, 1.0, 0.0))
    tri = jnp.where(lax.broadcasted_iota(I32, (kblk, kblk), 0) >= lax.broadcasted_iota(I32, (kblk, kblk), 1),
                    1.0, 0.0).astype(BF16)

    def select_block(i, taken):
        off = pl.multiple_of(i * kblk, kblk)
        kb = key_scr[i]
        tied = jnp.where(kb == thr, 1.0, 0.0)
        rank = taken + jnp.dot(tri, tied.astype(BF16), preferred_element_type=F32)
        tie = jnp.where(rank <= need, 0.0, NEG_BIG)
        bias = jnp.where(kb > thr, 0.0, jnp.where(kb == thr, tie, NEG_BIG))
        bias_scr[i] = jnp.where(off + krow < chunk_end, bias, NEG_BIG)
        return rank[kblk - 1:kblk, :]

    lax.fori_loop(0, nblk, select_block, jnp.zeros((1, tq), F32))

    m_scr[...] = jnp.full(m_scr.shape, NEG_BIG, F32)
    acc_scr[...] = jnp.zeros(acc_scr.shape, F32)
    rep = ATT_HEADS // KV_HEADS
    qt = qt_ref[0]
    qg = [jnp.concatenate([qt[(g * rep + r) * HEAD_DIM:(g * rep + r + 1) * HEAD_DIM, :]
                           for r in range(rep)], axis=1) for g in range(KV_HEADS)]
    ones = jnp.ones((16, kblk), BF16)

    def score_phase(i, buf):
        off = pl.multiple_of(i * kblk, kblk)
        bias = jnp.concatenate([bias_scr[i]] * rep, axis=1)
        kk = kb_ref[0, pl.ds(off, kblk), :]
        for g in range(KV_HEADS):
            s = jnp.dot(kk[:, g * HEAD_DIM:(g + 1) * HEAD_DIM], qg[g],
                        preferred_element_type=F32) + bias
            s_scr[buf, g] = s
            m_blk = jnp.max(_fold_rows(s, jnp.maximum), axis=0, keepdims=True)
            m_old = m_scr[g]
            m_new = jnp.maximum(m_old, m_blk)
            a_scr[g] = jnp.exp2(m_old - m_new)
            m_scr[g] = m_new

    def value_phase(i, buf):
        off = pl.multiple_of(i * kblk, kblk)
        vv = vt_ref[0, :, pl.ds(off, kblk)]
        for g in range(KV_HEADS):
            p = jnp.exp2(s_scr[buf, g] - m_scr[g]).astype(BF16)
            vg = jnp.concatenate([vv[g * HEAD_DIM:(g + 1) * HEAD_DIM, :], ones], axis=0)
            acc_scr[g] = a_scr[g] * acc_scr[g] + jnp.dot(vg, p, preferred_element_type=F32)

    score_phase(0, 0)

    def attn_pair(k, carry):
        i = 2 * k
        value_phase(i, 0)
        score_phase(i + 1, 1)
        value_phase(i + 1, 1)
        score_phase(jnp.minimum(i + 2, nblk - 1), 0)
        return carry

    lax.fori_loop(0, nblk // 2, attn_pair, 0)

    @pl.when(nblk % 2 == 1)
    def _():
        value_phase(nblk - 1, 0)

    heads = []
    for g in range(KV_HEADS):
        acc = acc_scr[g]
        og = acc[:HEAD_DIM] / acc[HEAD_DIM:HEAD_DIM + 1]
        heads += [og[:, r * tq:(r + 1) * tq] for r in range(rep)]
    o_ref[0] = jnp.concatenate(heads, axis=0).T.astype(BF16)


def _dsa(qt, qit, wit, kb, vt, kib, *, tq, pos0, topk):
    b, _, t = qt.shape
    lp = kb.shape[1]
    kblk = DSA_KEY_BLOCK
    assert lp % kblk == 0 and t % tq == 0 and tq % LANES == 0
    kern = functools.partial(_dsa_kernel, tq=tq, kblk=kblk, pos0=pos0, topk=topk)
    rep = ATT_HEADS // KV_HEADS
    qmap = lambda bi, j: (bi, 0, j)
    kmap = lambda bi, j: (bi, 0, 0)
    return pl.pallas_call(
        kern,
        grid=(b, t // tq),
        in_specs=[pl.BlockSpec((1, ATT_WIDTH, tq), qmap),
                  pl.BlockSpec((1, IDX_HEADS * IDX_DIM, tq), qmap),
                  pl.BlockSpec((1, 8, tq), qmap),
                  pl.BlockSpec((1, lp, KV_WIDTH), kmap),
                  pl.BlockSpec((1, KV_WIDTH, lp), kmap),
                  pl.BlockSpec((1, lp, IDX_DIM), kmap)],
        out_specs=pl.BlockSpec((1, tq, ATT_WIDTH), lambda bi, j: (bi, j, 0)),
        out_shape=jax.ShapeDtypeStruct((b, t, ATT_WIDTH), BF16),
        scratch_shapes=[pltpu.VMEM((lp // kblk, kblk, tq), I32),
                        pltpu.VMEM((lp // kblk, kblk, tq), I32),
                        pltpu.VMEM((lp // kblk, kblk, tq), F32),
                        pltpu.VMEM((2, KV_HEADS, kblk, rep * tq), F32),
                        pltpu.VMEM((KV_HEADS, HEAD_DIM + 16, rep * tq), F32),
                        pltpu.VMEM((KV_HEADS, 1, rep * tq), F32),
                        pltpu.VMEM((KV_HEADS, 1, rep * tq), F32)],
        compiler_params=_params(("parallel", "arbitrary")),
        name="dsa",
    )(qt, qit, wit, kb, vt, kib)


def _gmlp_kernel(x_ref, wu_ref, wv_ref, g_ref, b_ref, ws_ref, bst_ref, o_ref, v_ref, *, tm, lc):
    xb = x_ref[...].astype(BF16)
    u = _gelu_tanh(jnp.dot(xb, wu_ref[...], preferred_element_type=F32))
    v = _layer_norm(_gelu_tanh(jnp.dot(xb, wv_ref[...], preferred_element_type=F32)),
                    g_ref[...], b_ref[...])
    v_ref[...] = v
    vb = v.astype(BF16)
    r = lax.broadcasted_iota(I32, (lc, lc), 0)
    c = lax.broadcasted_iota(I32, (lc, lc), 1)
    for g in range(SG_GROUPS):
        w = jnp.where(r >= c, ws_ref[g], 0.0).astype(BF16)
        bias = bst_ref[:, g:g + 1]
        cs = slice(g * SG_GW, (g + 1) * SG_GW)
        for n in range(tm // lc):
            rs = slice(n * lc, (n + 1) * lc)
            mixed = jnp.dot(w, vb[rs, cs], preferred_element_type=F32) + bias
            o_ref[rs, cs] = (u[rs, cs] * mixed).astype(BF16)


def _gmlp(x, wu, wv, ln_g, ln_b, ws, bst, *, tm, lc):
    n = x.shape[0]
    row = lambda i: (i, 0)
    full2 = lambda i: (0, 0)
    return pl.pallas_call(
        functools.partial(_gmlp_kernel, tm=tm, lc=lc),
        grid=(n // tm,),
        in_specs=[pl.BlockSpec((tm, D_MODEL), row),
                  pl.BlockSpec((D_MODEL, SG_WIDTH), full2),
                  pl.BlockSpec((D_MODEL, SG_WIDTH), full2),
                  pl.BlockSpec((1, SG_WIDTH), full2),
                  pl.BlockSpec((1, SG_WIDTH), full2),
                  pl.BlockSpec((SG_GROUPS, lc, lc), lambda i: (0, 0, 0)),
                  pl.BlockSpec((lc, SG_GROUPS), full2)],
        out_specs=[pl.BlockSpec((tm, SG_WIDTH), row), pl.BlockSpec((tm, SG_WIDTH), row)],
        out_shape=[jax.ShapeDtypeStruct((n, SG_WIDTH), BF16),
                   jax.ShapeDtypeStruct((n, SG_WIDTH), F32)],
        compiler_params=_params(("parallel",)),
        name="gmlp",
    )(x, wu, wv, ln_g, ln_b, ws, bst)


def _hgrn_kernel(x_ref, w_ref, loglb_ref, log1mlb_ref, omlb_ref, ng_ref, s0_ref, c_ref, sout_ref,
                 q_scr, k_scr, v_scr, lf_scr, o_scr, st_scr, *, tc):
    t = pl.program_id(1)

    @pl.when(t == 0)
    def _():
        for h in range(HG_HEADS):
            st_scr[h] = s0_ref[0, h].T

    xb = x_ref[0].astype(BF16)
    z = jnp.dot(xb, w_ref[...], preferred_element_type=F32)
    hq = z[:, 0:HG_WIDTH]
    hf = z[:, HG_WIDTH:2 * HG_WIDTH]
    hg = z[:, 3 * HG_WIDTH:4 * HG_WIDTH]
    q_scr[...] = hq * _sigmoid(hq) * HG_DK ** -0.5
    u = jnp.exp(-jnp.abs(hf))
    w = 1.0 + u
    log_sig = jnp.minimum(hf, 0.0) - jnp.log(w)
    y = log1mlb_ref[...] + log_sig
    a = loglb_ref[...]
    lf_scr[...] = jnp.maximum(a, y) + jnp.log(1.0 + jnp.exp(-jnp.abs(a - y)))
    k_scr[...] = omlb_ref[...] * (jnp.where(hf >= 0.0, u, 1.0) / w)
    v_scr[...] = z[:, 2 * HG_WIDTH:3 * HG_WIDTH]

    ch = min(tc, CHUNK)
    ri = lax.broadcasted_iota(I32, (tc, tc), 0)
    ci = lax.broadcasted_iota(I32, (tc, tc), 1)
    same = (ri // ch) == (ci // ch)
    causal = same & (ri >= ci)
    lf = lf_scr[...]
    lf_hi = lf.astype(BF16)
    lf_r = lf - lf_hi.astype(F32)
    lf_mid = lf_r.astype(BF16)
    lf_lo = (lf_r - lf_mid.astype(F32)).astype(BF16)
    cmask = jnp.where(causal, 1.0, 0.0).astype(BF16)
    bcum_all = (jnp.dot(cmask, lf_hi, preferred_element_type=F32)
                + jnp.dot(cmask, lf_mid, preferred_element_type=F32)
                + jnp.dot(cmask, lf_lo, preferred_element_type=F32))
    factored = jnp.min(bcum_all) > -HG_FACTOR_RANGE

    @pl.when(factored)
    def _():
        btot = jnp.concatenate(
            [jnp.broadcast_to(bcum_all[(n + 1) * ch - 1:(n + 1) * ch, :], (ch, HG_WIDTH))
             for n in range(tc // ch)], axis=0)
        qa = q_scr[...]
        ka = k_scr[...]
        qd = (qa * jnp.exp(bcum_all)).astype(BF16)
        ki = (ka * jnp.exp(-bcum_all)).astype(BF16)
        kd = (ka * jnp.exp(btot - bcum_all)).astype(BF16)
        va = v_scr[...].astype(BF16)
        for h in range(HG_HEADS):
            cs = slice(h * HG_DK, (h + 1) * HG_DK)
            att = lax.dot_general(qd[:, cs], ki[:, cs], _NT, preferred_element_type=F32)
            att = jnp.where(causal, att, 0.0).astype(BF16)
            o = jnp.dot(att, va[:, cs], preferred_element_type=F32)
            st = st_scr[h]
            for n in range(tc // ch):
                rs = slice(n * ch, (n + 1) * ch)
                o_scr[rs, cs] = o[rs] + lax.dot_general(qd[rs, cs], st.astype(BF16), _NT,
                                                        preferred_element_type=F32)
                upd = lax.dot_general(va[rs, cs], kd[rs, cs], _TN, preferred_element_type=F32)
                st = jnp.exp(btot[n * ch:n * ch + 1, cs]) * st + upd
            st_scr[h] = st

    nb = HG_BLOCK
    tri = jnp.where(lax.broadcasted_iota(I32, (nb, nb), 0) >= lax.broadcasted_iota(I32, (nb, nb), 1),
                    1.0, 0.0).astype(F32)
    trow = lax.broadcasted_iota(I32, (nb, 1), 0)

    def block(i, carry):
        r0 = pl.multiple_of(i * nb, nb)
        rows = pl.ds(r0, nb)
        for h in range(HG_HEADS):
            cs = slice(h * HG_DK, (h + 1) * HG_DK)
            bcum = jnp.dot(tri, lf_scr[rows, cs], preferred_element_type=F32,
                           precision=lax.Precision.HIGHEST)
            qb = q_scr[rows, cs]
            kb = k_scr[rows, cs]
            vb = v_scr[rows, cs]
            o = jnp.zeros((nb, HG_DV), F32)
            for s in range(nb):
                e = jnp.exp(jnp.where(trow >= s, bcum - bcum[s:s + 1, :], -jnp.inf))
                a_ts = jnp.sum(qb * kb[s:s + 1, :] * e, axis=1, keepdims=True)
                o = o + a_ts * vb[s:s + 1, :]
            st = st_scr[h]
            qd = (qb * jnp.exp(bcum)).astype(BF16)
            o = o + lax.dot_general(qd, st.astype(BF16), _NT, preferred_element_type=F32)
            blast = bcum[nb - 1:nb, :]
            kd = (kb * jnp.exp(blast - bcum)).astype(BF16)
            upd = lax.dot_general(vb.astype(BF16), kd, _TN, preferred_element_type=F32)
            st_scr[h] = jnp.exp(blast) * st + upd
            o_scr[rows, cs] = o
        return carry

    @pl.when(jnp.logical_not(factored))
    def _():
        lax.fori_loop(0, tc // nb, block, 0)

    for h in range(HG_HEADS):
        cs = slice(h * HG_DV, (h + 1) * HG_DV)
        o = o_scr[:, cs]
        o = o * lax.rsqrt(jnp.mean(o * o, axis=-1, keepdims=True) + LN_EPS)
        g = hg[:, cs]
        c_ref[0, :, cs] = (o * ng_ref[:, cs] * (g * _sigmoid(g))).astype(BF16)

    @pl.when(t == pl.num_programs(1) - 1)
    def _():
        for h in range(HG_HEADS):
            sout_ref[0, h] = st_scr[h].T


def _hgrn(x, w, loglb, log1mlb, omlb, ng, s0, *, tc):
    b, t, _ = x.shape
    xmap = lambda bi, j: (bi, j, 0)
    vec = pl.BlockSpec((1, HG_WIDTH), lambda bi, j: (0, 0))
    smap = lambda bi, j: (bi, 0, 0, 0)
    return pl.pallas_call(
        functools.partial(_hgrn_kernel, tc=tc),
        grid=(b, t // tc),
        in_specs=[pl.BlockSpec((1, tc, D_MODEL), xmap),
                  pl.BlockSpec((D_MODEL, 4 * HG_WIDTH), lambda bi, j: (0, 0)),
                  vec, vec, vec, vec,
                  pl.BlockSpec((1, HG_HEADS, HG_DK, HG_DV), smap)],
        out_specs=[pl.BlockSpec((1, tc, HG_WIDTH), xmap),
                   pl.BlockSpec((1, HG_HEADS, HG_DK, HG_DV), smap)],
        out_shape=[jax.ShapeDtypeStruct((b, t, HG_WIDTH), BF16),
                   jax.ShapeDtypeStruct((b, HG_HEADS, HG_DK, HG_DV), F32)],
        scratch_shapes=[pltpu.VMEM((tc, HG_WIDTH), F32)] * 5
                       + [pltpu.VMEM((HG_HEADS, HG_DV, HG_DK), F32)],
        compiler_params=_params(("parallel", "arbitrary")),
        name="hgrn",
    )(x, w, loglb, log1mlb, omlb, ng, s0)


def _merge_kernel(x_ref, a_ref, b_ref, c_ref, wg_ref, wa_ref, wb_ref, wc_ref, wo_ref,
                  g_ref, bt_ref, y_ref):
    x = x_ref[...]
    xb = x.astype(BF16)
    m = None
    for i, (br, w) in enumerate(((a_ref, wa_ref), (b_ref, wb_ref), (c_ref, wc_ref))):
        gate = _sigmoid(jnp.dot(xb, wg_ref[:, i * D_MODEL:(i + 1) * D_MODEL],
                                preferred_element_type=F32))
        term = gate * jnp.dot(br[...], w[...], preferred_element_type=F32)
        m = term if m is None else m + term
    mo = jnp.dot(m.astype(BF16), wo_ref[...], preferred_element_type=F32)
    y_ref[...] = _layer_norm(ALPHA * x + mo, g_ref[...], bt_ref[...])


def _merge(x, a, b, c, wg, wa, wb, wc, wo, ln_g, ln_b, *, tm):
    n = x.shape[0]
    row = lambda i: (i, 0)
    full = lambda i: (0, 0)
    br = pl.BlockSpec((tm, ATT_WIDTH), row)
    bw = pl.BlockSpec((ATT_WIDTH, D_MODEL), full)
    vec = pl.BlockSpec((1, D_MODEL), full)
    return pl.pallas_call(
        _merge_kernel,
        grid=(n // tm,),
        in_specs=[pl.BlockSpec((tm, D_MODEL), row), br, br, br,
                  pl.BlockSpec((D_MODEL, N_BRANCH * D_MODEL), full), bw, bw, bw,
                  pl.BlockSpec((D_MODEL, D_MODEL), full), vec, vec],
        out_specs=pl.BlockSpec((tm, D_MODEL), row),
        out_shape=jax.ShapeDtypeStruct((n, D_MODEL), F32),
        compiler_params=_params(("parallel",)),
        name="merge",
    )(x, a, b, c, wg, wa, wb, wc, wo, ln_g, ln_b)


def _route(logits):
    ex = jnp.exp(logits - jnp.max(logits, axis=0, keepdims=True))
    probs = ex / jnp.sum(ex, axis=0, keepdims=True)
    p = [[probs[g * EXP_PER_GROUP + k:g * EXP_PER_GROUP + k + 1, :] for k in range(EXP_PER_GROUP)]
         for g in range(N_GROUPS)]
    score = []
    for g in range(N_GROUPS):
        best = None
        for k1 in range(EXP_PER_GROUP):
            for k2 in range(k1 + 1, EXP_PER_GROUP):
                pair = p[g][k1] + p[g][k2]
                best = pair if best is None else jnp.maximum(best, pair)
        score.append(best)
    gsel = jnp.zeros(score[0].shape, I32)
    top = score[0]
    for g in range(1, N_GROUPS):
        better = score[g] > top
        top = jnp.where(better, score[g], top)
        gsel = jnp.where(better, g, gsel)
    val = []
    for k in range(EXP_PER_GROUP):
        v = p[0][k]
        for g in range(1, N_GROUPS):
            v = jnp.where(gsel == g, p[g][k], v)
        val.append(v)
    v1, i1 = val[0], jnp.zeros(gsel.shape, I32)
    for k in range(1, EXP_PER_GROUP):
        better = val[k] > v1
        v1 = jnp.where(better, val[k], v1)
        i1 = jnp.where(better, k, i1)
    v2, i2 = jnp.full(v1.shape, -1.0, F32), jnp.zeros(gsel.shape, I32)
    for k in range(EXP_PER_GROUP):
        better = (i1 != k) & (val[k] > v2)
        v2 = jnp.where(better, val[k], v2)
        i2 = jnp.where(better, k, i2)
    den = v1 + v2
    w1, w2 = v1 / den, v2 / den
    rows = [jnp.where(i1 == k, w1, jnp.where(i2 == k, w2, 0.0)) for k in range(EXP_PER_GROUP)]
    return gsel, jnp.concatenate(rows, axis=0)


MOE_SUB = 128


def _moe_kernel(x_ref, wr_ref, br_ref, wg_ref, wu_ref, wd_ref, g_ref, bt_ref, y_ref,
                xs_scr, gs_scr, ys_scr, pos_scr, seg_scr, *, tm, cap):
    tile = pl.program_id(0)
    grp = pl.program_id(1)
    sub = MOE_SUB

    @pl.when((tile == 0) & (grp == 0))
    def _():
        ys_scr[...] = jnp.zeros(ys_scr.shape, F32)

    @pl.when(grp == 0)
    def _():
        xb = x_ref[...].astype(BF16)
        logits = lax.dot_general(wr_ref[...], xb, _NT, preferred_element_type=F32) + br_ref[...]
        gsel, gates = _route(logits)
        ind = jnp.concatenate([jnp.where(gsel == g, 1.0, 0.0) for g in range(N_GROUPS)]
                              + [jnp.zeros((8 - N_GROUPS, tm), F32)], axis=0)
        before = jnp.where(lax.broadcasted_iota(I32, (tm, tm), 0) < lax.broadcasted_iota(I32, (tm, tm), 1),
                           1.0, 0.0).astype(BF16)
        rank = jnp.dot(ind.astype(BF16), before, preferred_element_type=F32)
        cnt = jnp.sum(ind, axis=1, keepdims=True)
        nsub = jnp.ceil(cnt * (1.0 / sub))
        start = [jnp.zeros((1, 1), F32)]
        for g in range(1, N_GROUPS):
            start.append(start[-1] + nsub[g - 1:g] * sub)
        pos = jnp.zeros((1, tm), F32)
        for g in range(N_GROUPS):
            pos = pos + ind[g:g + 1] * (start[g] + rank[g:g + 1])
        pos_scr[...] = pos
        seg_scr[...] = jnp.concatenate(
            [jnp.broadcast_to(nsub[g:g + 1], (1, LANES)) for g in range(N_GROUPS)]
            + [jnp.broadcast_to(start[g], (1, LANES)) for g in range(N_GROUPS)], axis=0)
        perm = jnp.where(lax.broadcasted_iota(I32, (cap, tm), 0) == pos.astype(I32), 1.0, 0.0).astype(BF16)
        xs_scr[...] = jnp.dot(perm, xb, preferred_element_type=F32).astype(BF16)
        gt = jnp.concatenate([gates, jnp.zeros((LANES - EXP_PER_GROUP, tm), F32)], axis=0).T
        g_hi = gt.astype(BF16)
        g_lo = (gt - g_hi.astype(F32)).astype(BF16)
        gs_scr[...] = (jnp.dot(perm, g_hi, preferred_element_type=F32)
                       + jnp.dot(perm, g_lo, preferred_element_type=F32))

    n_sub = jnp.max(seg_scr[pl.ds(grp, 1), :]).astype(I32)
    start = jnp.max(seg_scr[pl.ds(N_GROUPS + grp, 1), :]).astype(I32)

    def sub_tile(i, carry):
        rows = pl.ds(pl.multiple_of(start + i * sub, sub), sub)
        xs = xs_scr[rows, :]
        gate = gs_scr[rows, :]
        acc = jnp.zeros((sub, D_MODEL), F32)
        for k in range(EXP_PER_GROUP):
            h = jnp.dot(xs, wg_ref[k], preferred_element_type=F32)
            up = jnp.dot(xs, wu_ref[k], preferred_element_type=F32)
            act = h * _sigmoid(h) * up * gate[:, k:k + 1]
            acc = acc + jnp.dot(act.astype(BF16), wd_ref[k], preferred_element_type=F32)
        ys_scr[rows, :] = acc
        return carry

    lax.fori_loop(0, n_sub, sub_tile, 0)

    @pl.when(grp == N_GROUPS - 1)
    def _():
        posc = jnp.broadcast_to(pos_scr[...], (8, tm)).T[:, 0:1].astype(I32)
        back = jnp.where(lax.broadcasted_iota(I32, (tm, cap), 1) == posc, 1.0, 0.0).astype(BF16)
        ys = ys_scr[...]
        y_hi = ys.astype(BF16)
        y_lo = (ys - y_hi.astype(F32)).astype(BF16)
        ffn = (jnp.dot(back, y_hi, preferred_element_type=F32)
               + jnp.dot(back, y_lo, preferred_element_type=F32))
        y_ref[...] = _layer_norm(ALPHA * x_ref[...] + ffn, g_ref[...], bt_ref[...])


def _moe(x, wr_t, br, wg, wu, wd, ln_g, ln_b, *, tm):
    n = x.shape[0]
    cap = tm + N_GROUPS * MOE_SUB
    row = lambda i, g: (i, 0)
    full = lambda i, g: (0, 0)
    vec = pl.BlockSpec((1, D_MODEL), full)
    return pl.pallas_call(
        functools.partial(_moe_kernel, tm=tm, cap=cap),
        grid=(n // tm, N_GROUPS),
        in_specs=[pl.BlockSpec((tm, D_MODEL), row),
                  pl.BlockSpec((N_EXPERTS, D_MODEL), full),
                  pl.BlockSpec((N_EXPERTS, 1), full),
                  pl.BlockSpec((EXP_PER_GROUP, D_MODEL, D_FF), lambda i, g: (g, 0, 0)),
                  pl.BlockSpec((EXP_PER_GROUP, D_MODEL, D_FF), lambda i, g: (g, 0, 0)),
                  pl.BlockSpec((EXP_PER_GROUP, D_FF, D_MODEL), lambda i, g: (g, 0, 0)),
                  vec, vec],
        out_specs=pl.BlockSpec((tm, D_MODEL), row),
        out_shape=jax.ShapeDtypeStruct((n, D_MODEL), F32),
        scratch_shapes=[pltpu.VMEM((cap, D_MODEL), BF16),
                        pltpu.VMEM((cap, LANES), F32),
                        pltpu.VMEM((cap, D_MODEL), F32),
                        pltpu.VMEM((1, tm), F32),
                        pltpu.VMEM((2 * N_GROUPS, LANES), F32)],
        compiler_params=_params(("arbitrary", "arbitrary")),
        name="moe",
    )(x, wr_t, br, wg, wu, wd, ln_g, ln_b)


WPREP_ROWS = 128


def _split_w_in_kernel(w_ref, wn_ref, wt_ref, wu_ref, wv_ref, whg_ref, wg_ref):
    w = w_ref[0]
    offs = [0]
    for n in SPLIT_SIZES:
        offs.append(offs[-1] + n)
    seg = lambda i, j=None: w[:, offs[i]:offs[i + 1 if j is None else j]]
    zeros = lambda c: jnp.zeros((w.shape[0], c), w.dtype)
    wn_ref[0] = jnp.concatenate([seg(1), seg(2), seg(4), zeros(LANES - IDX_DIM)], axis=1).astype(BF16)
    wt = jnp.concatenate([seg(0), seg(3), seg(2), seg(5), zeros(ATT_T_PAD - ATT_T_ROWS + 8 - IDX_HEADS)],
                         axis=1)
    for c in range(ATT_T_PAD // LANES):
        rows = min(LANES, ATT_T_ROWS - c * LANES)
        wt_ref[0, c * LANES:c * LANES + rows, :] = wt[:, c * LANES:(c + 1) * LANES].T[:rows].astype(BF16)
    wu_ref[0] = seg(6).astype(BF16)
    wv_ref[0] = seg(7).astype(BF16)
    whg_ref[0] = seg(8, 12).astype(BF16)
    wg_ref[0] = seg(12).astype(BF16)


def _split_w_in(w_in):
    depth, d, ncol = w_in.shape
    tr = WPREP_ROWS
    cols = (ATT_N_COLS, None, SG_WIDTH, SG_WIDTH, 4 * HG_WIDTH, N_BRANCH * D_MODEL)
    shapes = [(depth, ATT_T_ROWS, d) if c is None else (depth, d, c) for c in cols]
    specs = [pl.BlockSpec((1, ATT_T_ROWS, tr), lambda l, i: (l, 0, i)) if c is None
             else pl.BlockSpec((1, tr, c), lambda l, i: (l, i, 0)) for c in cols]
    return pl.pallas_call(
        _split_w_in_kernel,
        grid=(depth, d // tr),
        in_specs=[pl.BlockSpec((1, tr, ncol), lambda l, i: (l, i, 0))],
        out_specs=specs,
        out_shape=[jax.ShapeDtypeStruct(sh, BF16) for sh in shapes],
        compiler_params=_params(("parallel", "parallel")),
        name="split_w_in",
    )(w_in)


def _row_tile(n, want):
    while n % want:
        want //= 2
    return want


def _layer(x, bsz, t, lw, *, pos0, topk, cache, s0):
    n = bsz * t
    lc = min(t, SG_LEN)
    k, v, ki, kb, kib, qt, qit, vt, wit = _attn_proj(x, lw["w_att_n"], lw["w_att_t"], bsz, t,
                                                     _row_tile(t, 512))
    kb3 = kb.reshape(bsz, t, KV_WIDTH)
    kib3 = kib.reshape(bsz, t, IDX_DIM)
    if cache is not None:
        ck, cvt, cki = cache
        kb3 = jnp.concatenate([ck, kb3], axis=1)
        kib3 = jnp.concatenate([cki, kib3], axis=1)
        vt = jnp.concatenate([cvt, vt], axis=2)
    ltot = kb3.shape[1]
    lp = -(-ltot // DSA_KEY_BLOCK) * DSA_KEY_BLOCK
    tq = DSA_QUERY_TILE if t % DSA_QUERY_TILE == 0 else LANES
    tp = -(-t // tq) * tq
    padq = lambda a: jnp.pad(a, ((0, 0), (0, 0), (0, tp - t)))
    a = _dsa(padq(qt), padq(qit), padq(wit), jnp.pad(kb3, ((0, 0), (0, lp - ltot), (0, 0))),
             jnp.pad(vt, ((0, 0), (0, 0), (0, lp - ltot))),
             jnp.pad(kib3, ((0, 0), (0, lp - ltot), (0, 0))), tq=tq, pos0=pos0, topk=topk)[:, :t]
    b, v_gm = _gmlp(x, lw["w_u"], lw["w_v"], lw["ln_sg_g"], lw["ln_sg_b"],
                    lw["w_sg"][:, :lc, :lc], lw["b_sg"][:, :lc].T, tm=_row_tile(n, 256), lc=lc)
    c, s_new = _hgrn(x.reshape(bsz, t, D_MODEL), lw["w_hg"], lw["log_lb"], lw["log1m_lb"],
                     lw["om_lb"], lw["hg_norm_g"], s0, tc=min(t, 256))
    x1 = _merge(x, a.reshape(n, ATT_WIDTH), b, c.reshape(n, HG_WIDTH), lw["w_gates"],
                lw["w_branch_a"], lw["w_branch_b"], lw["w_branch_c"], lw["w_out"],
                lw["ln1_g"], lw["ln1_b"], tm=_row_tile(n, 512))
    x2 = _moe(x1, lw["w_router_t"], lw["b_router"], lw["w_exp_gate"], lw["w_exp_up"],
              lw["w_exp_down"], lw["ln2_g"], lw["ln2_b"], tm=_row_tile(n, 512))
    return x2, k, v, ki, s_new, v_gm


def kernel(x_prompt, x_sample, cache_k, cache_v, cache_kidx, state_hgrn, w_in, w_sg, b_sg, ln_sg_g, ln_sg_b, hg_lb_logits, hg_norm_g, w_branch_a, w_branch_b, w_branch_c, w_out, ln1_g, ln1_b, w_router, b_router, w_exp_gate, w_exp_up, w_exp_down, ln2_g, ln2_b):
    bp, sp, _ = x_prompt.shape
    bs, ss, _ = x_sample.shape
    past = cache_k.shape[2]
    topk_p = min(TOPK_MAX, sp // 4)
    topk_s = min(TOPK_MAX, (past + ss) // 4)

    lb_all = jnp.cumsum(jax.nn.softmax(hg_lb_logits.astype(F32), axis=0), axis=0)
    lb_all = lb_all - lb_all[0:1]
    vec = lambda a: a.reshape(1, -1)

    xp = x_prompt.reshape(bp * sp, D_MODEL)
    xs = x_sample.reshape(bs * ss, D_MODEL)
    s0_p = jnp.zeros((bp, HG_HEADS, HG_DK, HG_DV), F32)
    outs_p, outs_s = [], []
    w_att_n, w_att_t, w_u, w_v, w_hg, w_gates = _split_w_in(w_in)
    for l in range(DEPTH):
        lw = dict(
            w_att_n=w_att_n[l], w_att_t=w_att_t[l], w_u=w_u[l], w_v=w_v[l], w_hg=w_hg[l],
            w_gates=w_gates[l],
            ln_sg_g=vec(ln_sg_g[l]), ln_sg_b=vec(ln_sg_b[l]), w_sg=w_sg[l], b_sg=b_sg[l],
            log_lb=vec(jnp.log(lb_all[l])), log1m_lb=vec(jnp.log1p(-lb_all[l])),
            om_lb=vec(1.0 - lb_all[l]), hg_norm_g=vec(hg_norm_g[l].astype(F32)),
            w_branch_a=w_branch_a[l].astype(BF16), w_branch_b=w_branch_b[l].astype(BF16),
            w_branch_c=w_branch_c[l].astype(BF16), w_out=w_out[l].astype(BF16),
            ln1_g=vec(ln1_g[l]), ln1_b=vec(ln1_b[l]),
            w_router_t=w_router.T.astype(BF16), b_router=b_router.astype(F32).reshape(-1, 1),
            w_exp_gate=w_exp_gate[l].astype(BF16), w_exp_up=w_exp_up[l].astype(BF16),
            w_exp_down=w_exp_down[l].astype(BF16),
            ln2_g=vec(ln2_g[l]), ln2_b=vec(ln2_b[l]))
        xp, k, v, ki, s_new, _ = _layer(xp, bp, sp, lw, pos0=0, topk=topk_p, cache=None, s0=s0_p)
        outs_p.append((k.reshape(bp, sp, KV_HEADS, HEAD_DIM), v.reshape(bp, sp, KV_HEADS, HEAD_DIM),
                       ki.reshape(bp, sp, IDX_DIM), s_new))
        cache = (cache_k[l].reshape(bs, past, KV_WIDTH).astype(BF16),
                 jnp.swapaxes(cache_v[l].reshape(bs, past, KV_WIDTH), 1, 2).astype(BF16),
                 cache_kidx[l].astype(BF16))
        xs, k, v, ki, s_new, v_gm = _layer(xs, bs, ss, lw, pos0=past, topk=topk_s, cache=cache,
                                           s0=state_hgrn[l].astype(F32))
        outs_s.append((k.reshape(bs, ss, KV_HEADS, HEAD_DIM), v.reshape(bs, ss, KV_HEADS, HEAD_DIM),
                       ki.reshape(bs, ss, IDX_DIM), s_new, v_gm.reshape(bs, ss, SG_WIDTH)))

    stack = lambda rows, i: jnp.stack([r[i] for r in rows])
    return (xp.reshape(bp, sp, D_MODEL), xs.reshape(bs, ss, D_MODEL),
            stack(outs_p, 0), stack(outs_p, 1), stack(outs_p, 2), stack(outs_p, 3),
            stack(outs_s, 0), stack(outs_s, 1), stack(outs_s, 2), stack(outs_s, 3),
            stack(outs_s, 4))
```

```python
import functools

import jax
import jax.numpy as jnp
from jax import lax
from jax.experimental import pallas as pl
from jax.experimental.pallas import tpu as pltpu

F32 = jnp.float32
BF16 = jnp.bfloat16
I32 = jnp.int32

D_MODEL = 1024
DEPTH = 4
CHUNK = 64
ATT_HEADS = 8
KV_HEADS = 2
HEAD_DIM = 64
ATT_WIDTH = ATT_HEADS * HEAD_DIM
KV_WIDTH = KV_HEADS * HEAD_DIM
IDX_HEADS = 4
IDX_DIM = 64
TOPK_MAX = 256
SG_LEN = 128
SG_GROUPS = 4
SG_WIDTH = 512
SG_GW = SG_WIDTH // SG_GROUPS
HG_HEADS = 4
HG_DK = 128
HG_DV = 128
HG_WIDTH = HG_HEADS * HG_DV
HG_BLOCK = 16
HG_FACTOR_RANGE = 80.0
N_BRANCH = 3
N_EXPERTS = 16
N_GROUPS = 4
EXP_PER_GROUP = 4
D_FF = 256
ALPHA = (2 * DEPTH) ** 0.25
LN_EPS = 1e-5

SPLIT_SIZES = (ATT_WIDTH, KV_WIDTH, KV_WIDTH, IDX_HEADS * IDX_DIM, IDX_DIM, IDX_HEADS,
               SG_WIDTH, SG_WIDTH, HG_HEADS * HG_DK, HG_HEADS * HG_DK, HG_WIDTH, HG_WIDTH,
               N_BRANCH * D_MODEL)

LANES = 128
INT_MIN = -2 ** 31
NEG_BIG = -1e30
LOG2_E = 1.4426950408889634
DSA_KEY_BLOCK = 512
DSA_QUERY_TILE = 512
VMEM_LIMIT = 56 * 1024 * 1024

_NT = (((1,), (1,)), ((), ()))
_TN = (((0,), (0,)), ((), ()))


def _params(sem):
    return pltpu.CompilerParams(dimension_semantics=sem, vmem_limit_bytes=VMEM_LIMIT)


def _layer_norm(x, g, b):
    mu = jnp.mean(x, axis=-1, keepdims=True)
    d = x - mu
    var = jnp.mean(d * d, axis=-1, keepdims=True)
    return d * lax.rsqrt(var + LN_EPS) * g + b


def _gelu_tanh(x):
    return 0.5 * x * (1.0 + jnp.tanh(0.7978845608028654 * (x + 0.044715 * (x * x * x))))


def _sigmoid(x):
    return 0.5 * jnp.tanh(0.5 * x) + 0.5


def _fold_rows(x, op):
    parts = [x[i:i + 8] for i in range(0, x.shape[0], 8)]
    while len(parts) > 1:
        parts = [op(parts[i], parts[i + 1]) for i in range(0, len(parts), 2)]
    return parts[0]


ATT_T_ROWS = ATT_WIDTH + IDX_HEADS * IDX_DIM + KV_WIDTH + 8
ATT_T_PAD = -(-ATT_T_ROWS // LANES) * LANES
ATT_N_COLS = 3 * LANES


def _attn_proj_kernel(x_ref, wn_ref, wt_ref, k_ref, v_ref, ki_ref, kb_ref, kib_ref,
                      qt_ref, qit_ref, vt_ref, wit_ref):
    xb = x_ref[...].astype(BF16)
    z = jnp.dot(xb, wn_ref[...], preferred_element_type=F32)
    k = z[:, 0:KV_WIDTH]
    ki = z[:, 2 * KV_WIDTH:2 * KV_WIDTH + IDX_DIM]
    k_ref[...] = k
    v_ref[...] = z[:, KV_WIDTH:2 * KV_WIDTH]
    ki_ref[...] = ki
    kb_ref[...] = k.astype(BF16)
    kib_ref[...] = ki.astype(BF16)
    zt = lax.dot_general(wt_ref[...], xb, _NT, preferred_element_type=F32)
    o = 0
    qt_ref[0] = (zt[o:o + ATT_WIDTH] * (HEAD_DIM ** -0.5 * LOG2_E)).astype(BF16)
    o += ATT_WIDTH
    qit_ref[0] = (zt[o:o + IDX_HEADS * IDX_DIM] * IDX_DIM ** -0.5).astype(BF16)
    o += IDX_HEADS * IDX_DIM
    vt_ref[0] = zt[o:o + KV_WIDTH].astype(BF16)
    o += KV_WIDTH
    wit_ref[0] = zt[o:o + 8] * IDX_HEADS ** -0.5


def _attn_proj(x, wn, wt, bsz, t, tm):
    n = bsz * t
    per = t // tm
    row = lambda i: (i, 0)
    tmap = lambda i: (i // per, 0, i % per)
    nat = ((KV_WIDTH, F32), (KV_WIDTH, F32), (IDX_DIM, F32), (KV_WIDTH, BF16), (IDX_DIM, BF16))
    tr = ((ATT_WIDTH, BF16), (IDX_HEADS * IDX_DIM, BF16), (KV_WIDTH, BF16), (8, F32))
    return pl.pallas_call(
        _attn_proj_kernel,
        grid=(n // tm,),
        in_specs=[pl.BlockSpec((tm, D_MODEL), row),
                  pl.BlockSpec((D_MODEL, ATT_N_COLS), lambda i: (0, 0)),
                  pl.BlockSpec((ATT_T_ROWS, D_MODEL), lambda i: (0, 0))],
        out_specs=[pl.BlockSpec((tm, c), row) for c, _ in nat]
                  + [pl.BlockSpec((1, c, tm), tmap) for c, _ in tr],
        out_shape=[jax.ShapeDtypeStruct((n, c), d) for c, d in nat]
                  + [jax.ShapeDtypeStruct((bsz, c, t), d) for c, d in tr],
        compiler_params=_params(("parallel",)),
        name="attn_proj",
    )(x, wn, wt)


def _dsa_kernel(qt_ref, qit_ref, wit_ref, kb_ref, vt_ref, kib_ref, o_ref,
                key_scr, sort_scr, bias_scr, s_scr, acc_scr, m_scr, a_scr,
                *, tq, kblk, pos0, topk):
    j = pl.program_id(1)
    qpos0 = pos0 + j * tq
    qcol = lax.broadcasted_iota(I32, (1, tq), 1)
    chunk_end = ((qpos0 + qcol) // CHUNK + 1) * CHUNK
    lvis = ((qpos0 + tq - 1) // CHUNK + 1) * CHUNK
    nblk = (lvis + kblk - 1) // kblk
    krow = lax.broadcasted_iota(I32, (kblk, tq), 0)

    wi = wit_ref[0]
    qit = qit_ref[0]

    def score_block(i, carry):
        off = pl.multiple_of(i * kblk, kblk)
        kib = kib_ref[0, pl.ds(off, kblk), :]
        sc = jnp.zeros((kblk, tq), F32)
        for h in range(IDX_HEADS):
            raw = jnp.dot(kib, qit[h * IDX_DIM:(h + 1) * IDX_DIM, :], preferred_element_type=F32)
            sc = sc + jnp.maximum(raw, 0.0) * wi[h:h + 1, :]
        bits = pltpu.bitcast(sc, I32)
        key = bits ^ ((bits >> 31) & 0x7FFFFFFF)
        key = jnp.where(off + krow < chunk_end, key, INT_MIN)
        key_scr[i] = key
        for g in range(kblk // 32):
            a, b, c, d = (key[g * 32 + 8 * r:g * 32 + 8 * r + 8] for r in range(4))
            a, b = jnp.maximum(a, b), jnp.minimum(a, b)
            c, d = jnp.maximum(c, d), jnp.minimum(c, d)
            a, c = jnp.maximum(a, c), jnp.minimum(a, c)
            b, d = jnp.maximum(b, d), jnp.minimum(b, d)
            b, c = jnp.maximum(b, c), jnp.minimum(b, c)
            sort_scr[i, g * 32:(g + 1) * 32] = jnp.concatenate([a, b, c, d], axis=0)
        return carry

    lax.fori_loop(0, nblk, score_block, 0)

    def count(hits):
        def body(i, acc):
            return acc + _fold_rows(hits(i, key_scr[i]), jnp.add)
        acc = lax.fori_loop(0, nblk, body, jnp.zeros((8, tq), F32))
        return jnp.sum(acc, axis=0, keepdims=True)

    def key_bit(i, thr):
        cand = thr + jnp.left_shift(jnp.int32(1), 31 - i)

        def body(b, accs):
            ks = sort_scr[b]
            accs = list(accs)
            for g in range(kblk // 32):
                s0, s1, s2, s3 = (ks[g * 32 + 8 * r:g * 32 + 8 * r + 8] for r in range(4))
                n = jnp.where(s3 >= cand, 4.0, jnp.where(s2 >= cand, 3.0, jnp.where(
                    s1 >= cand, 2.0, jnp.where(s0 >= cand, 1.0, 0.0))))
                accs[g % 4] = accs[g % 4] + n
            return tuple(accs)

        accs = lax.fori_loop(0, nblk, body, tuple(jnp.zeros((8, tq), F32) for _ in range(4)))
        cnt = jnp.sum((accs[0] + accs[1]) + (accs[2] + accs[3]), axis=0, keepdims=True)
        return jnp.where(cnt >= topk, cand, thr)

    thr = lax.fori_loop(0, 32, key_bit, jnp.full((1, tq), INT_MIN, I32))

    need = topk - count(lambda _, kb: jnp.where(kb > thr, 1.0, 0.0))
    tri = jnp.where(lax.broadcasted_iota(I32, (kblk, kblk), 0) >= lax.broadcasted_iota(I32, (kblk, kblk), 1),
                    1.0, 0.0).astype(BF16)

    def select_block(i, taken):
        off = pl.multiple_of(i * kblk, kblk)
        kb = key_scr[i]
        tied = jnp.where(kb == thr, 1.0, 0.0)
        rank = taken + jnp.dot(tri, tied.astype(BF16), preferred_element_type=F32)
        tie = jnp.where(rank <= need, 0.0, NEG_BIG)
        bias = jnp.where(kb > thr, 0.0, jnp.where(kb == thr, tie, NEG_BIG))
        bias_scr[i] = jnp.where(off + krow < chunk_end, bias, NEG_BIG)
        return rank[kblk - 1:kblk, :]

    lax.fori_loop(0, nblk, select_block, jnp.zeros((1, tq), F32))

    m_scr[...] = jnp.full(m_scr.shape, NEG_BIG, F32)
    acc_scr[...] = jnp.zeros(acc_scr.shape, F32)
    rep = ATT_HEADS // KV_HEADS
    qt = qt_ref[0]
    qg = [jnp.concatenate([qt[(g * rep + r) * HEAD_DIM:(g * rep + r + 1) * HEAD_DIM, :]
                           for r in range(rep)], axis=1) for g in range(KV_HEADS)]
    ones = jnp.ones((16, kblk), BF16)

    def score_phase(i, buf):
        off = pl.multiple_of(i * kblk, kblk)
        bias = jnp.concatenate([bias_scr[i]] * rep, axis=1)
        kk = kb_ref[0, pl.ds(off, kblk), :]
        for g in range(KV_HEADS):
            s = jnp.dot(kk[:, g * HEAD_DIM:(g + 1) * HEAD_DIM], qg[g],
                        preferred_element_type=F32) + bias
            s_scr[buf, g] = s
            m_blk = jnp.max(_fold_rows(s, jnp.maximum), axis=0, keepdims=True)
            m_old = m_scr[g]
            m_new = jnp.maximum(m_old, m_blk)
            a_scr[g] = jnp.exp2(m_old - m_new)
            m_scr[g] = m_new

    def value_phase(i, buf):
        off = pl.multiple_of(i * kblk, kblk)
        vv = vt_ref[0, :, pl.ds(off, kblk)]
        for g in range(KV_HEADS):
            p = jnp.exp2(s_scr[buf, g] - m_scr[g]).astype(BF16)
            vg = jnp.concatenate([vv[g * HEAD_DIM:(g + 1) * HEAD_DIM, :], ones], axis=0)
            acc_scr[g] = a_scr[g] * acc_scr[g] + jnp.dot(vg, p, preferred_element_type=F32)

    score_phase(0, 0)

    def attn_pair(k, carry):
        i = 2 * k
        value_phase(i, 0)
        score_phase(i + 1, 1)
        value_phase(i + 1, 1)
        score_phase(jnp.minimum(i + 2, nblk - 1), 0)
        return carry

    lax.fori_loop(0, nblk // 2, attn_pair, 0)

    @pl.when(nblk % 2 == 1)
    def _():
        value_phase(nblk - 1, 0)

    heads = []
    for g in range(KV_HEADS):
        acc = acc_scr[g]
        og = acc[:HEAD_DIM] / acc[HEAD_DIM:HEAD_DIM + 1]
        heads += [og[:, r * tq:(r + 1) * tq] for r in range(rep)]
    o_ref[0] = jnp.concatenate(heads, axis=0).T.astype(BF16)


def _dsa(qt, qit, wit, kb, vt, kib, *, tq, pos0, topk):
    b, _, t = qt.shape
    lp = kb.shape[1]
    kblk = DSA_KEY_BLOCK
    assert lp % kblk == 0 and t % tq == 0 and tq % LANES == 0
    kern = functools.partial(_dsa_kernel, tq=tq, kblk=kblk, pos0=pos0, topk=topk)
    rep = ATT_HEADS // KV_HEADS
    qmap = lambda bi, j: (bi, 0, j)
    kmap = lambda bi, j: (bi, 0, 0)
    return pl.pallas_call(
        kern,
        grid=(b, t // tq),
        in_specs=[pl.BlockSpec((1, ATT_WIDTH, tq), qmap),
                  pl.BlockSpec((1, IDX_HEADS * IDX_DIM, tq), qmap),
                  pl.BlockSpec((1, 8, tq), qmap),
                  pl.BlockSpec((1, lp, KV_WIDTH), kmap),
                  pl.BlockSpec((1, KV_WIDTH, lp), kmap),
                  pl.BlockSpec((1, lp, IDX_DIM), kmap)],
        out_specs=pl.BlockSpec((1, tq, ATT_WIDTH), lambda bi, j: (bi, j, 0)),
        out_shape=jax.ShapeDtypeStruct((b, t, ATT_WIDTH), BF16),
        scratch_shapes=[pltpu.VMEM((lp // kblk, kblk, tq), I32),
                        pltpu.VMEM((lp // kblk, kblk, tq), I32),
                        pltpu.VMEM((lp // kblk, kblk, tq), F32),
                        pltpu.VMEM((2, KV_HEADS, kblk, rep * tq), F32),
                        pltpu.VMEM((KV_HEADS, HEAD_DIM + 16, rep * tq), F32),
                        pltpu.VMEM((KV_HEADS, 1, rep * tq), F32),
                        pltpu.VMEM((KV_HEADS, 1, rep * tq), F32)],
        compiler_params=_params(("parallel", "arbitrary")),
        name="dsa",
    )(qt, qit, wit, kb, vt, kib)


def _gmlp_kernel(x_ref, wu_ref, wv_ref, g_ref, b_ref, ws_ref, bst_ref, o_ref, v_ref, *, tm, lc):
    xb = x_ref[...].astype(BF16)
    u = _gelu_tanh(jnp.dot(xb, wu_ref[...], preferred_element_type=F32))
    v = _layer_norm(_gelu_tanh(jnp.dot(xb, wv_ref[...], preferred_element_type=F32)),
                    g_ref[...], b_ref[...])
    v_ref[...] = v
    vb = v.astype(BF16)
    r = lax.broadcasted_iota(I32, (lc, lc), 0)
    c = lax.broadcasted_iota(I32, (lc, lc), 1)
    for g in range(SG_GROUPS):
        w = jnp.where(r >= c, ws_ref[g], 0.0).astype(BF16)
        bias = bst_ref[:, g:g + 1]
        cs = slice(g * SG_GW, (g + 1) * SG_GW)
        for n in range(tm // lc):
            rs = slice(n * lc, (n + 1) * lc)
            mixed = jnp.dot(w, vb[rs, cs], preferred_element_type=F32) + bias
            o_ref[rs, cs] = (u[rs, cs] * mixed).astype(BF16)


def _gmlp(x, wu, wv, ln_g, ln_b, ws, bst, *, tm, lc):
    n = x.shape[0]
    row = lambda i: (i, 0)
    full2 = lambda i: (0, 0)
    return pl.pallas_call(
        functools.partial(_gmlp_kernel, tm=tm, lc=lc),
        grid=(n // tm,),
        in_specs=[pl.BlockSpec((tm, D_MODEL), row),
                  pl.BlockSpec((D_MODEL, SG_WIDTH), full2),
                  pl.BlockSpec((D_MODEL, SG_WIDTH), full2),
                  pl.BlockSpec((1, SG_WIDTH), full2),
                  pl.BlockSpec((1, SG_WIDTH), full2),
                  pl.BlockSpec((SG_GROUPS, lc, lc), lambda i: (0, 0, 0)),
                  pl.BlockSpec((lc, SG_GROUPS), full2)],
        out_specs=[pl.BlockSpec((tm, SG_WIDTH), row), pl.BlockSpec((tm, SG_WIDTH), row)],
        out_shape=[jax.ShapeDtypeStruct((n, SG_WIDTH), BF16),
                   jax.ShapeDtypeStruct((n, SG_WIDTH), F32)],
        compiler_params=_params(("parallel",)),
        name="gmlp",
    )(x, wu, wv, ln_g, ln_b, ws, bst)


def _hgrn_kernel(x_ref, w_ref, loglb_ref, log1mlb_ref, omlb_ref, ng_ref, s0_ref, c_ref, sout_ref,
                 q_scr, k_scr, v_scr, lf_scr, o_scr, st_scr, *, tc):
    t = pl.program_id(1)

    @pl.when(t == 0)
    def _():
        for h in range(HG_HEADS):
            st_scr[h] = s0_ref[0, h].T

    xb = x_ref[0].astype(BF16)
    z = jnp.dot(xb, w_ref[...], preferred_element_type=F32)
    hq = z[:, 0:HG_WIDTH]
    hf = z[:, HG_WIDTH:2 * HG_WIDTH]
    hg = z[:, 3 * HG_WIDTH:4 * HG_WIDTH]
    q_scr[...] = hq * _sigmoid(hq) * HG_DK ** -0.5
    u = jnp.exp(-jnp.abs(hf))
    w = 1.0 + u
    log_sig = jnp.minimum(hf, 0.0) - jnp.log(w)
    y = log1mlb_ref[...] + log_sig
    a = loglb_ref[...]
    lf_scr[...] = jnp.maximum(a, y) + jnp.log(1.0 + jnp.exp(-jnp.abs(a - y)))
    k_scr[...] = omlb_ref[...] * (jnp.where(hf >= 0.0, u, 1.0) / w)
    v_scr[...] = z[:, 2 * HG_WIDTH:3 * HG_WIDTH]

    ch = min(tc, CHUNK)
    ri = lax.broadcasted_iota(I32, (tc, tc), 0)
    ci = lax.broadcasted_iota(I32, (tc, tc), 1)
    same = (ri // ch) == (ci // ch)
    causal = same & (ri >= ci)
    lf = lf_scr[...]
    lf_hi = lf.astype(BF16)
    lf_r = lf - lf_hi.astype(F32)
    lf_mid = lf_r.astype(BF16)
    lf_lo = (lf_r - lf_mid.astype(F32)).astype(BF16)
    cmask = jnp.where(causal, 1.0, 0.0).astype(BF16)
    bcum_all = (jnp.dot(cmask, lf_hi, preferred_element_type=F32)
                + jnp.dot(cmask, lf_mid, preferred_element_type=F32)
                + jnp.dot(cmask, lf_lo, preferred_element_type=F32))
    factored = jnp.min(bcum_all) > -HG_FACTOR_RANGE

    @pl.when(factored)
    def _():
        btot = jnp.concatenate(
            [jnp.broadcast_to(bcum_all[(n + 1) * ch - 1:(n + 1) * ch, :], (ch, HG_WIDTH))
             for n in range(tc // ch)], axis=0)
        qa = q_scr[...]
        ka = k_scr[...]
        qd = (qa * jnp.exp(bcum_all)).astype(BF16)
        ki = (ka * jnp.exp(-bcum_all)).astype(BF16)
        kd = (ka * jnp.exp(btot - bcum_all)).astype(BF16)
        va = v_scr[...].astype(BF16)
        for h in range(HG_HEADS):
            cs = slice(h * HG_DK, (h + 1) * HG_DK)
            att = lax.dot_general(qd[:, cs], ki[:, cs], _NT, preferred_element_type=F32)
            att = jnp.where(causal, att, 0.0).astype(BF16)
            o = jnp.dot(att, va[:, cs], preferred_element_type=F32)
            st = st_scr[h]
            for n in range(tc // ch):
                rs = slice(n * ch, (n + 1) * ch)
                o_scr[rs, cs] = o[rs] + lax.dot_general(qd[rs, cs], st.astype(BF16), _NT,
                                                        preferred_element_type=F32)
                upd = lax.dot_general(va[rs, cs], kd[rs, cs], _TN, preferred_element_type=F32)
                st = jnp.exp(btot[n * ch:n * ch + 1, cs]) * st + upd
            st_scr[h] = st

    nb = HG_BLOCK
    tri = jnp.where(lax.broadcasted_iota(I32, (nb, nb), 0) >= lax.broadcasted_iota(I32, (nb, nb), 1),
                    1.0, 0.0).astype(F32)
    trow = lax.broadcasted_iota(I32, (nb, 1), 0)

    def block(i, carry):
        r0 = pl.multiple_of(i * nb, nb)
        rows = pl.ds(r0, nb)
        for h in range(HG_HEADS):
            cs = slice(h * HG_DK, (h + 1) * HG_DK)
            bcum = jnp.dot(tri, lf_scr[rows, cs], preferred_element_type=F32,
                           precision=lax.Precision.HIGHEST)
            qb = q_scr[rows, cs]
            kb = k_scr[rows, cs]
            vb = v_scr[rows, cs]
            o = jnp.zeros((nb, HG_DV), F32)
            for s in range(nb):
                e = jnp.exp(jnp.where(trow >= s, bcum - bcum[s:s + 1, :], -jnp.inf))
                a_ts = jnp.sum(qb * kb[s:s + 1, :] * e, axis=1, keepdims=True)
                o = o + a_ts * vb[s:s + 1, :]
            st = st_scr[h]
            qd = (qb * jnp.exp(bcum)).astype(BF16)
            o = o + lax.dot_general(qd, st.astype(BF16), _NT, preferred_element_type=F32)
            blast = bcum[nb - 1:nb, :]
            kd = (kb * jnp.exp(blast - bcum)).astype(BF16)
            upd = lax.dot_general(vb.astype(BF16), kd, _TN, preferred_element_type=F32)
            st_scr[h] = jnp.exp(blast) * st + upd
            o_scr[rows, cs] = o
        return carry

    @pl.when(jnp.logical_not(factored))
    def _():
        lax.fori_loop(0, tc // nb, block, 0)

    for h in range(HG_HEADS):
        cs = slice(h * HG_DV, (h + 1) * HG_DV)
        o = o_scr[:, cs]
        o = o * lax.rsqrt(jnp.mean(o * o, axis=-1, keepdims=True) + LN_EPS)
        g = hg[:, cs]
        c_ref[0, :, cs] = (o * ng_ref[:, cs] * (g * _sigmoid(g))).astype(BF16)

    @pl.when(t == pl.num_programs(1) - 1)
    def _():
        for h in range(HG_HEADS):
            sout_ref[0, h] = st_scr[h].T


def _hgrn(x, w, loglb, log1mlb, omlb, ng, s0, *, tc):
    b, t, _ = x.shape
    xmap = lambda bi, j: (bi, j, 0)
    vec = pl.BlockSpec((1, HG_WIDTH), lambda bi, j: (0, 0))
    smap = lambda bi, j: (bi, 0, 0, 0)
    return pl.pallas_call(
        functools.partial(_hgrn_kernel, tc=tc),
        grid=(b, t // tc),
        in_specs=[pl.BlockSpec((1, tc, D_MODEL), xmap),
                  pl.BlockSpec((D_MODEL, 4 * HG_WIDTH), lambda bi, j: (0, 0)),
                  vec, vec, vec, vec,
                  pl.BlockSpec((1, HG_HEADS, HG_DK, HG_DV), smap)],
        out_specs=[pl.BlockSpec((1, tc, HG_WIDTH), xmap),
                   pl.BlockSpec((1, HG_HEADS, HG_DK, HG_DV), smap)],
        out_shape=[jax.ShapeDtypeStruct((b, t, HG_WIDTH), BF16),
                   jax.ShapeDtypeStruct((b, HG_HEADS, HG_DK, HG_DV), F32)],
        scratch_shapes=[pltpu.VMEM((tc, HG_WIDTH), F32)] * 5
                       + [pltpu.VMEM((HG_HEADS, HG_DV, HG_DK), F32)],
        compiler_params=_params(("parallel", "arbitrary")),
        name="hgrn",
    )(x, w, loglb, log1mlb, omlb, ng, s0)


def _merge_kernel(x_ref, a_ref, b_ref, c_ref, wg_ref, wa_ref, wb_ref, wc_ref, wo_ref,
                  g_ref, bt_ref, y_ref):
    x = x_ref[...]
    xb = x.astype(BF16)
    m = None
    for i, (br, w) in enumerate(((a_ref, wa_ref), (b_ref, wb_ref), (c_ref, wc_ref))):
        gate = _sigmoid(jnp.dot(xb, wg_ref[:, i * D_MODEL:(i + 1) * D_MODEL],
                                preferred_element_type=F32))
        term = gate * jnp.dot(br[...], w[...], preferred_element_type=F32)
        m = term if m is None else m + term
    mo = jnp.dot(m.astype(BF16), wo_ref[...], preferred_element_type=F32)
    y_ref[...] = _layer_norm(ALPHA * x + mo, g_ref[...], bt_ref[...])


def _merge(x, a, b, c, wg, wa, wb, wc, wo, ln_g, ln_b, *, tm):
    n = x.shape[0]
    row = lambda i: (i, 0)
    full = lambda i: (0, 0)
    br = pl.BlockSpec((tm, ATT_WIDTH), row)
    bw = pl.BlockSpec((ATT_WIDTH, D_MODEL), full)
    vec = pl.BlockSpec((1, D_MODEL), full)
    return pl.pallas_call(
        _merge_kernel,
        grid=(n // tm,),
        in_specs=[pl.BlockSpec((tm, D_MODEL), row), br, br, br,
                  pl.BlockSpec((D_MODEL, N_BRANCH * D_MODEL), full), bw, bw, bw,
                  pl.BlockSpec((D_MODEL, D_MODEL), full), vec, vec],
        out_specs=pl.BlockSpec((tm, D_MODEL), row),
        out_shape=jax.ShapeDtypeStruct((n, D_MODEL), F32),
        compiler_params=_params(("parallel",)),
        name="merge",
    )(x, a, b, c, wg, wa, wb, wc, wo, ln_g, ln_b)


def _route(logits):
    ex = jnp.exp(logits - jnp.max(logits, axis=0, keepdims=True))
    probs = ex / jnp.sum(ex, axis=0, keepdims=True)
    p = [[probs[g * EXP_PER_GROUP + k:g * EXP_PER_GROUP + k + 1, :] for k in range(EXP_PER_GROUP)]
         for g in range(N_GROUPS)]
    score = []
    for g in range(N_GROUPS):
        best = None
        for k1 in range(EXP_PER_GROUP):
            for k2 in range(k1 + 1, EXP_PER_GROUP):
                pair = p[g][k1] + p[g][k2]
                best = pair if best is None else jnp.maximum(best, pair)
        score.append(best)
    gsel = jnp.zeros(score[0].shape, I32)
    top = score[0]
    for g in range(1, N_GROUPS):
        better = score[g] > top
        top = jnp.where(better, score[g], top)
        gsel = jnp.where(better, g, gsel)
    val = []
    for k in range(EXP_PER_GROUP):
        v = p[0][k]
        for g in range(1, N_GROUPS):
            v = jnp.where(gsel == g, p[g][k], v)
        val.append(v)
    v1, i1 = val[0], jnp.zeros(gsel.shape, I32)
    for k in range(1, EXP_PER_GROUP):
        better = val[k] > v1
        v1 = jnp.where(better, val[k], v1)
        i1 = jnp.where(better, k, i1)
    v2, i2 = jnp.full(v1.shape, -1.0, F32), jnp.zeros(gsel.shape, I32)
    for k in range(EXP_PER_GROUP):
        better = (i1 != k) & (val[k] > v2)
        v2 = jnp.where(better, val[k], v2)
        i2 = jnp.where(better, k, i2)
    den = v1 + v2
    w1, w2 = v1 / den, v2 / den
    out = []
    for g in range(N_GROUPS):
        rows = [jnp.where(gsel == g, jnp.where(i1 == k, w1, jnp.where(i2 == k, w2, 0.0)), 0.0)
                for k in range(EXP_PER_GROUP)]
        out.append(jnp.concatenate(rows, axis=0))
    return out


def _moe_kernel(x_ref, wr_ref, br_ref, wg_ref, wu_ref, wd_ref, g_ref, bt_ref, y_ref,
                xb_scr, gate_scr, acc_scr, *, tm):
    grp = pl.program_id(1)

    @pl.when(grp == 0)
    def _():
        xb = x_ref[...].astype(BF16)
        xb_scr[...] = xb
        logits = lax.dot_general(wr_ref[...], xb, _NT, preferred_element_type=F32) + br_ref[...]
        gates = _route(logits)
        pad = jnp.zeros((LANES - EXP_PER_GROUP, tm), F32)
        for g in range(N_GROUPS):
            gate_scr[g] = jnp.concatenate([gates[g], pad], axis=0).T
        acc_scr[...] = jnp.zeros(acc_scr.shape, F32)

    xb = xb_scr[...]
    gate = gate_scr[grp]
    acc = acc_scr[...]
    for k in range(EXP_PER_GROUP):
        h = jnp.dot(xb, wg_ref[k], preferred_element_type=F32)
        up = jnp.dot(xb, wu_ref[k], preferred_element_type=F32)
        act = h * _sigmoid(h) * up * gate[:, k:k + 1]
        acc = acc + jnp.dot(act.astype(BF16), wd_ref[k], preferred_element_type=F32)
    acc_scr[...] = acc

    @pl.when(grp == N_GROUPS - 1)
    def _():
        y_ref[...] = _layer_norm(ALPHA * x_ref[...] + acc, g_ref[...], bt_ref[...])


def _moe(x, wr_t, br, wg, wu, wd, ln_g, ln_b, *, tm):
    n = x.shape[0]
    row = lambda i, g: (i, 0)
    full = lambda i, g: (0, 0)
    vec = pl.BlockSpec((1, D_MODEL), full)
    return pl.pallas_call(
        functools.partial(_moe_kernel, tm=tm),
        grid=(n // tm, N_GROUPS),
        in_specs=[pl.BlockSpec((tm, D_MODEL), row),
                  pl.BlockSpec((N_EXPERTS, D_MODEL), full),
                  pl.BlockSpec((N_EXPERTS, 1), full),
                  pl.BlockSpec((EXP_PER_GROUP, D_MODEL, D_FF), lambda i, g: (g, 0, 0)),
                  pl.BlockSpec((EXP_PER_GROUP, D_MODEL, D_FF), lambda i, g: (g, 0, 0)),
                  pl.BlockSpec((EXP_PER_GROUP, D_FF, D_MODEL), lambda i, g: (g, 0, 0)),
                  vec, vec],
        out_specs=pl.BlockSpec((tm, D_MODEL), row),
        out_shape=jax.ShapeDtypeStruct((n, D_MODEL), F32),
        scratch_shapes=[pltpu.VMEM((tm, D_MODEL), BF16),
                        pltpu.VMEM((N_GROUPS, tm, LANES), F32),
                        pltpu.VMEM((tm, D_MODEL), F32)],
        compiler_params=_params(("parallel", "arbitrary")),
        name="moe",
    )(x, wr_t, br, wg, wu, wd, ln_g, ln_b)


WPREP_ROWS = 128


def _split_w_in_kernel(w_ref, wn_ref, wt_ref, wu_ref, wv_ref, whg_ref, wg_ref):
    w = w_ref[0]
    offs = [0]
    for n in SPLIT_SIZES:
        offs.append(offs[-1] + n)
    seg = lambda i, j=None: w[:, offs[i]:offs[i + 1 if j is None else j]]
    zeros = lambda c: jnp.zeros((w.shape[0], c), w.dtype)
    wn_ref[0] = jnp.concatenate([seg(1), seg(2), seg(4), zeros(LANES - IDX_DIM)], axis=1).astype(BF16)
    wt = jnp.concatenate([seg(0), seg(3), seg(2), seg(5), zeros(ATT_T_PAD - ATT_T_ROWS + 8 - IDX_HEADS)],
                         axis=1)
    for c in range(ATT_T_PAD // LANES):
        rows = min(LANES, ATT_T_ROWS - c * LANES)
        wt_ref[0, c * LANES:c * LANES + rows, :] = wt[:, c * LANES:(c + 1) * LANES].T[:rows].astype(BF16)
    wu_ref[0] = seg(6).astype(BF16)
    wv_ref[0] = seg(7).astype(BF16)
    whg_ref[0] = seg(8, 12).astype(BF16)
    wg_ref[0] = seg(12).astype(BF16)


def _split_w_in(w_in):
    depth, d, ncol = w_in.shape
    tr = WPREP_ROWS
    cols = (ATT_N_COLS, None, SG_WIDTH, SG_WIDTH, 4 * HG_WIDTH, N_BRANCH * D_MODEL)
    shapes = [(depth, ATT_T_ROWS, d) if c is None else (depth, d, c) for c in cols]
    specs = [pl.BlockSpec((1, ATT_T_ROWS, tr), lambda l, i: (l, 0, i)) if c is None
             else pl.BlockSpec((1, tr, c), lambda l, i: (l, i, 0)) for c in cols]
    return pl.pallas_call(
        _split_w_in_kernel,
        grid=(depth, d // tr),
        in_specs=[pl.BlockSpec((1, tr, ncol), lambda l, i: (l, i, 0))],
        out_specs=specs,
        out_shape=[jax.ShapeDtypeStruct(sh, BF16) for sh in shapes],
        compiler_params=_params(("parallel", "parallel")),
        name="split_w_in",
    )(w_in)


def _cast_kernel(*refs):
    n = len(refs) // 2
    for src, dst in zip(refs[:n], refs[n:]):
        dst[...] = src[...].astype(BF16)


def _experts_to_bf16(*ws):
    depth, e = ws[0].shape[:2]
    spec = lambda a: pl.BlockSpec((1, 1) + a.shape[2:], lambda l, j: (l, j, 0, 0))
    return pl.pallas_call(
        _cast_kernel,
        grid=(depth, e),
        in_specs=[spec(a) for a in ws],
        out_specs=[spec(a) for a in ws],
        out_shape=[jax.ShapeDtypeStruct(a.shape, BF16) for a in ws],
        compiler_params=_params(("parallel", "parallel")),
        name="experts_to_bf16",
    )(*ws)


def _row_tile(n, want):
    while n % want:
        want //= 2
    return want


def _layer(x, bsz, t, lw, *, pos0, topk, cache, s0):
    n = bsz * t
    lc = min(t, SG_LEN)
    k, v, ki, kb, kib, qt, qit, vt, wit = _attn_proj(x, lw["w_att_n"], lw["w_att_t"], bsz, t,
                                                     _row_tile(t, 512))
    kb3 = kb.reshape(bsz, t, KV_WIDTH)
    kib3 = kib.reshape(bsz, t, IDX_DIM)
    if cache is not None:
        ck, cvt, cki = cache
        kb3 = jnp.concatenate([ck, kb3], axis=1)
        kib3 = jnp.concatenate([cki, kib3], axis=1)
        vt = jnp.concatenate([cvt, vt], axis=2)
    ltot = kb3.shape[1]
    lp = -(-ltot // DSA_KEY_BLOCK) * DSA_KEY_BLOCK
    tq = DSA_QUERY_TILE if t % DSA_QUERY_TILE == 0 else LANES
    tp = -(-t // tq) * tq
    padq = lambda a: jnp.pad(a, ((0, 0), (0, 0), (0, tp - t)))
    a = _dsa(padq(qt), padq(qit), padq(wit), jnp.pad(kb3, ((0, 0), (0, lp - ltot), (0, 0))),
             jnp.pad(vt, ((0, 0), (0, 0), (0, lp - ltot))),
             jnp.pad(kib3, ((0, 0), (0, lp - ltot), (0, 0))), tq=tq, pos0=pos0, topk=topk)[:, :t]
    b, v_gm = _gmlp(x, lw["w_u"], lw["w_v"], lw["ln_sg_g"], lw["ln_sg_b"],
                    lw["w_sg"][:, :lc, :lc], lw["b_sg"][:, :lc].T, tm=_row_tile(n, 512), lc=lc)
    c, s_new = _hgrn(x.reshape(bsz, t, D_MODEL), lw["w_hg"], lw["log_lb"], lw["log1m_lb"],
                     lw["om_lb"], lw["hg_norm_g"], s0, tc=min(t, 256))
    x1 = _merge(x, a.reshape(n, ATT_WIDTH), b, c.reshape(n, HG_WIDTH), lw["w_gates"],
                lw["w_branch_a"], lw["w_branch_b"], lw["w_branch_c"], lw["w_out"],
                lw["ln1_g"], lw["ln1_b"], tm=_row_tile(n, 512))
    x2 = _moe(x1, lw["w_router_t"], lw["b_router"], lw["w_exp_gate"], lw["w_exp_up"],
              lw["w_exp_down"], lw["ln2_g"], lw["ln2_b"], tm=_row_tile(n, 1024))
    return x2, k, v, ki, s_new, v_gm


def kernel(x_prompt, x_sample, cache_k, cache_v, cache_kidx, state_hgrn, w_in, w_sg, b_sg, ln_sg_g, ln_sg_b, hg_lb_logits, hg_norm_g, w_branch_a, w_branch_b, w_branch_c, w_out, ln1_g, ln1_b, w_router, b_router, w_exp_gate, w_exp_up, w_exp_down, ln2_g, ln2_b):
    bp, sp, _ = x_prompt.shape
    bs, ss, _ = x_sample.shape
    past = cache_k.shape[2]
    topk_p = min(TOPK_MAX, sp // 4)
    topk_s = min(TOPK_MAX, (past + ss) // 4)

    lb_all = jnp.cumsum(jax.nn.softmax(hg_lb_logits.astype(F32), axis=0), axis=0)
    lb_all = lb_all - lb_all[0:1]
    vec = lambda a: a.reshape(1, -1)

    xp = x_prompt.reshape(bp * sp, D_MODEL)
    xs = x_sample.reshape(bs * ss, D_MODEL)
    s0_p = jnp.zeros((bp, HG_HEADS, HG_DK, HG_DV), F32)
    outs_p, outs_s = [], []
    w_att_n, w_att_t, w_u, w_v, w_hg, w_gates = _split_w_in(w_in)
    wg_b, wu_b, wd_b = _experts_to_bf16(w_exp_gate, w_exp_up, w_exp_down)
    for l in range(DEPTH):
        lw = dict(
            w_att_n=w_att_n[l], w_att_t=w_att_t[l], w_u=w_u[l], w_v=w_v[l], w_hg=w_hg[l],
            w_gates=w_gates[l],
            ln_sg_g=vec(ln_sg_g[l]), ln_sg_b=vec(ln_sg_b[l]), w_sg=w_sg[l], b_sg=b_sg[l],
            log_lb=vec(jnp.log(lb_all[l])), log1m_lb=vec(jnp.log1p(-lb_all[l])),
            om_lb=vec(1.0 - lb_all[l]), hg_norm_g=vec(hg_norm_g[l].astype(F32)),
            w_branch_a=w_branch_a[l].astype(BF16), w_branch_b=w_branch_b[l].astype(BF16),
            w_branch_c=w_branch_c[l].astype(BF16), w_out=w_out[l].astype(BF16),
            ln1_g=vec(ln1_g[l]), ln1_b=vec(ln1_b[l]),
            w_router_t=w_router.T.astype(BF16), b_router=b_router.astype(F32).reshape(-1, 1),
            w_exp_gate=wg_b[l], w_exp_up=wu_b[l], w_exp_down=wd_b[l],
            ln2_g=vec(ln2_g[l]), ln2_b=vec(ln2_b[l]))
        xp, k, v, ki, s_new, _ = _layer(xp, bp, sp, lw, pos0=0, topk=topk_p, cache=None, s0=s0_p)
        outs_p.append((k.reshape(bp, sp, KV_HEADS, HEAD_DIM), v.reshape(bp, sp, KV_HEADS, HEAD_DIM),
                       ki.reshape(bp, sp, IDX_DIM), s_new))
        cache = (cache_k[l].reshape(bs, past, KV_WIDTH).astype(BF16),
                 jnp.swapaxes(cache_v[l].reshape(bs, past, KV_WIDTH), 1, 2).astype(BF16),
                 cache_kidx[l].astype(BF16))
        xs, k, v, ki, s_new, v_gm = _layer(xs, bs, ss, lw, pos0=past, topk=topk_s, cache=cache,
                                           s0=state_hgrn[l].astype(F32))
        outs_s.append((k.reshape(bs, ss, KV_HEADS, HEAD_DIM), v.reshape(bs, ss, KV_HEADS, HEAD_DIM),
                       ki.reshape(bs, ss, IDX_DIM), s_new, v_gm.reshape(bs, ss, SG_WIDTH)))

    stack = lambda rows, i: jnp.stack([r[i] for r in rows])
    return (xp.reshape(bp, sp, D_MODEL), xs.reshape(bs, ss, D_MODEL),
            stack(outs_p, 0), stack(outs_p, 1), stack(outs_p, 2), stack(outs_p, 3),
            stack(outs_s, 0), stack(outs_s, 1), stack(outs_s, 2), stack(outs_s, 3),
            stack(outs_s, 4))
```

```python
import functools

import jax
import jax.numpy as jnp
from jax import lax
from jax.experimental import pallas as pl
from jax.experimental.pallas import tpu as pltpu

F32 = jnp.float32
BF16 = jnp.bfloat16
I32 = jnp.int32

D_MODEL = 1024
DEPTH = 4
CHUNK = 64
ATT_HEADS = 8
KV_HEADS = 2
HEAD_DIM = 64
ATT_WIDTH = ATT_HEADS * HEAD_DIM
KV_WIDTH = KV_HEADS * HEAD_DIM
IDX_HEADS = 4
IDX_DIM = 64
TOPK_MAX = 256
SG_LEN = 128
SG_GROUPS = 4
SG_WIDTH = 512
SG_GW = SG_WIDTH // SG_GROUPS
HG_HEADS = 4
HG_DK = 128
HG_DV = 128
HG_WIDTH = HG_HEADS * HG_DV
HG_BLOCK = 16
HG_FACTOR_RANGE = 80.0
N_BRANCH = 3
N_EXPERTS = 16
N_GROUPS = 4
EXP_PER_GROUP = 4
D_FF = 256
ALPHA = (2 * DEPTH) ** 0.25
LN_EPS = 1e-5

SPLIT_SIZES = (ATT_WIDTH, KV_WIDTH, KV_WIDTH, IDX_HEADS * IDX_DIM, IDX_DIM, IDX_HEADS,
               SG_WIDTH, SG_WIDTH, HG_HEADS * HG_DK, HG_HEADS * HG_DK, HG_WIDTH, HG_WIDTH,
               N_BRANCH * D_MODEL)

LANES = 128
INT_MIN = -2 ** 31
NEG_BIG = -1e30
LOG2_E = 1.4426950408889634
DSA_KEY_BLOCK = 512
DSA_QUERY_TILE = 512
VMEM_LIMIT = 56 * 1024 * 1024

_NT = (((1,), (1,)), ((), ()))
_TN = (((0,), (0,)), ((), ()))


def _params(sem):
    return pltpu.CompilerParams(dimension_semantics=sem, vmem_limit_bytes=VMEM_LIMIT)


def _layer_norm(x, g, b):
    mu = jnp.mean(x, axis=-1, keepdims=True)
    d = x - mu
    var = jnp.mean(d * d, axis=-1, keepdims=True)
    return d * lax.rsqrt(var + LN_EPS) * g + b


def _gelu_tanh(x):
    return 0.5 * x * (1.0 + jnp.tanh(0.7978845608028654 * (x + 0.044715 * (x * x * x))))


def _sigmoid(x):
    return 0.5 * jnp.tanh(0.5 * x) + 0.5


def _fold_rows(x, op):
    parts = [x[i:i + 8] for i in range(0, x.shape[0], 8)]
    while len(parts) > 1:
        parts = [op(parts[i], parts[i + 1]) for i in range(0, len(parts), 2)]
    return parts[0]


ATT_T_ROWS = ATT_WIDTH + IDX_HEADS * IDX_DIM + KV_WIDTH + 8
ATT_T_PAD = -(-ATT_T_ROWS // LANES) * LANES
ATT_N_COLS = 3 * LANES


def _attn_proj_kernel(x_ref, wn_ref, wt_ref, k_ref, v_ref, ki_ref, kb_ref, kib_ref,
                      qt_ref, qit_ref, vt_ref, wit_ref):
    xb = x_ref[...].astype(BF16)
    z = jnp.dot(xb, wn_ref[...], preferred_element_type=F32)
    k = z[:, 0:KV_WIDTH]
    ki = z[:, 2 * KV_WIDTH:2 * KV_WIDTH + IDX_DIM]
    k_ref[...] = k
    v_ref[...] = z[:, KV_WIDTH:2 * KV_WIDTH]
    ki_ref[...] = ki
    kb_ref[...] = k.astype(BF16)
    kib_ref[...] = ki.astype(BF16)
    zt = lax.dot_general(wt_ref[...], xb, _NT, preferred_element_type=F32)
    o = 0
    qt_ref[0] = (zt[o:o + ATT_WIDTH] * (HEAD_DIM ** -0.5 * LOG2_E)).astype(BF16)
    o += ATT_WIDTH
    qit_ref[0] = (zt[o:o + IDX_HEADS * IDX_DIM] * IDX_DIM ** -0.5).astype(BF16)
    o += IDX_HEADS * IDX_DIM
    vt_ref[0] = zt[o:o + KV_WIDTH].astype(BF16)
    o += KV_WIDTH
    wit_ref[0] = zt[o:o + 8] * IDX_HEADS ** -0.5


def _attn_proj(x, wn, wt, bsz, t, tm):
    n = bsz * t
    per = t // tm
    row = lambda i: (i, 0)
    tmap = lambda i: (i // per, 0, i % per)
    nat = ((KV_WIDTH, F32), (KV_WIDTH, F32), (IDX_DIM, F32), (KV_WIDTH, BF16), (IDX_DIM, BF16))
    tr = ((ATT_WIDTH, BF16), (IDX_HEADS * IDX_DIM, BF16), (KV_WIDTH, BF16), (8, F32))
    return pl.pallas_call(
        _attn_proj_kernel,
        grid=(n // tm,),
        in_specs=[pl.BlockSpec((tm, D_MODEL), row),
                  pl.BlockSpec((D_MODEL, ATT_N_COLS), lambda i: (0, 0)),
                  pl.BlockSpec((ATT_T_ROWS, D_MODEL), lambda i: (0, 0))],
        out_specs=[pl.BlockSpec((tm, c), row) for c, _ in nat]
                  + [pl.BlockSpec((1, c, tm), tmap) for c, _ in tr],
        out_shape=[jax.ShapeDtypeStruct((n, c), d) for c, d in nat]
                  + [jax.ShapeDtypeStruct((bsz, c, t), d) for c, d in tr],
        compiler_params=_params(("parallel",)),
        name="attn_proj",
    )(x, wn, wt)


def _dsa_kernel(qt_ref, qit_ref, wit_ref, kb_ref, vt_ref, kib_ref, o_ref,
                key_scr, sort_scr, bias_scr, s_scr, acc_scr, m_scr, a_scr,
                *, tq, kblk, pos0, topk):
    j = pl.program_id(1)
    qpos0 = pos0 + j * tq
    qcol = lax.broadcasted_iota(I32, (1, tq), 1)
    chunk_end = ((qpos0 + qcol) // CHUNK + 1) * CHUNK
    lvis = ((qpos0 + tq - 1) // CHUNK + 1) * CHUNK
    nblk = (lvis + kblk - 1) // kblk
    krow = lax.broadcasted_iota(I32, (kblk, tq), 0)

    wi = wit_ref[0]
    qit = qit_ref[0]

    def score_block(i, carry):
        off = pl.multiple_of(i * kblk, kblk)
        kib = kib_ref[0, pl.ds(off, kblk), :]
        sc = jnp.zeros((kblk, tq), F32)
        for h in range(IDX_HEADS):
            raw = jnp.dot(kib, qit[h * IDX_DIM:(h + 1) * IDX_DIM, :], preferred_element_type=F32)
            sc = sc + jnp.maximum(raw, 0.0) * wi[h:h + 1, :]
        bits = pltpu.bitcast(sc, I32)
        key = bits ^ ((bits >> 31) & 0x7FFFFFFF)
        key = jnp.where(off + krow < chunk_end, key, INT_MIN)
        key_scr[i] = key
        for g in range(kblk // 32):
            a, b, c, d = (key[g * 32 + 8 * r:g * 32 + 8 * r + 8] for r in range(4))
            a, b = jnp.maximum(a, b), jnp.minimum(a, b)
            c, d = jnp.maximum(c, d), jnp.minimum(c, d)
            a, c = jnp.maximum(a, c), jnp.minimum(a, c)
            b, d = jnp.maximum(b, d), jnp.minimum(b, d)
            b, c = jnp.maximum(b, c), jnp.minimum(b, c)
            sort_scr[i, g * 32:(g + 1) * 32] = jnp.concatenate([a, b, c, d], axis=0)
        return carry

    lax.fori_loop(0, nblk, score_block, 0)

    def count(hits):
        def body(i, acc):
            return acc + _fold_rows(hits(i, key_scr[i]), jnp.add)
        acc = lax.fori_loop(0, nblk, body, jnp.zeros((8, tq), F32))
        return jnp.sum(acc, axis=0, keepdims=True)

    def key_bit(i, thr):
        cand = thr + jnp.left_shift(jnp.int32(1), 31 - i)

        def body(b, accs):
            ks = sort_scr[b]
            accs = list(accs)
            for g in range(kblk // 32):
                s0, s1, s2, s3 = (ks[g * 32 + 8 * r:g * 32 + 8 * r + 8] for r in range(4))
                n = jnp.where(s3 >= cand, 4.0, jnp.where(s2 >= cand, 3.0, jnp.where(
                    s1 >= cand, 2.0, jnp.where(s0 >= cand, 1.0, 0.0))))
                accs[g % 4] = accs[g % 4] + n
            return tuple(accs)

        accs = lax.fori_loop(0, nblk, body, tuple(jnp.zeros((8, tq), F32) for _ in range(4)))
        cnt = jnp.sum((accs[0] + accs[1]) + (accs[2] + accs[3]), axis=0, keepdims=True)
        return jnp.where(cnt >= topk, cand, thr)

    thr = lax.fori_loop(0, 32, key_bit, jnp.full((1, tq), INT_MIN, I32))

    need = topk - count(lambda _, kb: jnp.where(kb > thr, 1.0, 0.0))
    tri = jnp.where(lax.broadcasted_iota(I32, (kblk, kblk), 0) >= lax.broadcasted_iota(I32, (kblk, kblk), 1),
                    1.0, 0.0).astype(BF16)

    def select_block(i, taken):
        off = pl.multiple_of(i * kblk, kblk)
        kb = key_scr[i]
        tied = jnp.where(kb == thr, 1.0, 0.0)
        rank = taken + jnp.dot(tri, tied.astype(BF16), preferred_element_type=F32)
        tie = jnp.where(rank <= need, 0.0, NEG_BIG)
        bias = jnp.where(kb > thr, 0.0, jnp.where(kb == thr, tie, NEG_BIG))
        bias_scr[i] = jnp.where(off + krow < chunk_end, bias, NEG_BIG)
        return rank[kblk - 1:kblk, :]

    lax.fori_loop(0, nblk, select_block, jnp.zeros((1, tq), F32))

    m_scr[...] = jnp.full(m_scr.shape, NEG_BIG, F32)
    acc_scr[...] = jnp.zeros(acc_scr.shape, F32)
    rep = ATT_HEADS // KV_HEADS
    qt = qt_ref[0]
    qg = [jnp.concatenate([qt[(g * rep + r) * HEAD_DIM:(g * rep + r + 1) * HEAD_DIM, :]
                           for r in range(rep)], axis=1) for g in range(KV_HEADS)]
    ones = jnp.ones((16, kblk), BF16)

    def score_phase(i, buf):
        off = pl.multiple_of(i * kblk, kblk)
        bias = jnp.concatenate([bias_scr[i]] * rep, axis=1)
        kk = kb_ref[0, pl.ds(off, kblk), :]
        for g in range(KV_HEADS):
            s = jnp.dot(kk[:, g * HEAD_DIM:(g + 1) * HEAD_DIM], qg[g],
                        preferred_element_type=F32) + bias
            s_scr[buf, g] = s
            m_blk = jnp.max(_fold_rows(s, jnp.maximum), axis=0, keepdims=True)
            m_old = m_scr[g]
            m_new = jnp.maximum(m_old, m_blk)
            a_scr[g] = jnp.exp2(m_old - m_new)
            m_scr[g] = m_new

    def value_phase(i, buf):
        off = pl.multiple_of(i * kblk, kblk)
        vv = vt_ref[0, :, pl.ds(off, kblk)]
        for g in range(KV_HEADS):
            p = jnp.exp2(s_scr[buf, g] - m_scr[g]).astype(BF16)
            vg = jnp.concatenate([vv[g * HEAD_DIM:(g + 1) * HEAD_DIM, :], ones], axis=0)
            acc_scr[g] = a_scr[g] * acc_scr[g] + jnp.dot(vg, p, preferred_element_type=F32)

    score_phase(0, 0)

    def attn_pair(k, carry):
        i = 2 * k
        value_phase(i, 0)
        score_phase(i + 1, 1)
        value_phase(i + 1, 1)
        score_phase(jnp.minimum(i + 2, nblk - 1), 0)
        return carry

    lax.fori_loop(0, nblk // 2, attn_pair, 0)

    @pl.when(nblk % 2 == 1)
    def _():
        value_phase(nblk - 1, 0)

    heads = []
    for g in range(KV_HEADS):
        acc = acc_scr[g]
        og = acc[:HEAD_DIM] / acc[HEAD_DIM:HEAD_DIM + 1]
        heads += [og[:, r * tq:(r + 1) * tq] for r in range(rep)]
    o_ref[0] = jnp.concatenate(heads, axis=0).T.astype(BF16)


def _dsa(qt, qit, wit, kb, vt, kib, *, tq, pos0, topk):
    b, _, t = qt.shape
    lp = kb.shape[1]
    kblk = DSA_KEY_BLOCK
    assert lp % kblk == 0 and t % tq == 0 and tq % LANES == 0
    kern = functools.partial(_dsa_kernel, tq=tq, kblk=kblk, pos0=pos0, topk=topk)
    rep = ATT_HEADS // KV_HEADS
    qmap = lambda bi, j: (bi, 0, j)
    kmap = lambda bi, j: (bi, 0, 0)
    return pl.pallas_call(
        kern,
        grid=(b, t // tq),
        in_specs=[pl.BlockSpec((1, ATT_WIDTH, tq), qmap),
                  pl.BlockSpec((1, IDX_HEADS * IDX_DIM, tq), qmap),
                  pl.BlockSpec((1, 8, tq), qmap),
                  pl.BlockSpec((1, lp, KV_WIDTH), kmap),
                  pl.BlockSpec((1, KV_WIDTH, lp), kmap),
                  pl.BlockSpec((1, lp, IDX_DIM), kmap)],
        out_specs=pl.BlockSpec((1, tq, ATT_WIDTH), lambda bi, j: (bi, j, 0)),
        out_shape=jax.ShapeDtypeStruct((b, t, ATT_WIDTH), BF16),
        scratch_shapes=[pltpu.VMEM((lp // kblk, kblk, tq), I32),
                        pltpu.VMEM((lp // kblk, kblk, tq), I32),
                        pltpu.VMEM((lp // kblk, kblk, tq), F32),
                        pltpu.VMEM((2, KV_HEADS, kblk, rep * tq), F32),
                        pltpu.VMEM((KV_HEADS, HEAD_DIM + 16, rep * tq), F32),
                        pltpu.VMEM((KV_HEADS, 1, rep * tq), F32),
                        pltpu.VMEM((KV_HEADS, 1, rep * tq), F32)],
        compiler_params=_params(("parallel", "arbitrary")),
        name="dsa",
    )(qt, qit, wit, kb, vt, kib)


def _gmlp_kernel(x_ref, wu_ref, wv_ref, g_ref, b_ref, ws_ref, bst_ref, o_ref, v_ref, *, tm, lc):
    xb = x_ref[...].astype(BF16)
    u = _gelu_tanh(jnp.dot(xb, wu_ref[...], preferred_element_type=F32))
    v = _layer_norm(_gelu_tanh(jnp.dot(xb, wv_ref[...], preferred_element_type=F32)),
                    g_ref[...], b_ref[...])
    v_ref[...] = v
    vb = v.astype(BF16)
    r = lax.broadcasted_iota(I32, (lc, lc), 0)
    c = lax.broadcasted_iota(I32, (lc, lc), 1)
    for g in range(SG_GROUPS):
        w = jnp.where(r >= c, ws_ref[g], 0.0).astype(BF16)
        bias = bst_ref[:, g:g + 1]
        cs = slice(g * SG_GW, (g + 1) * SG_GW)
        for n in range(tm // lc):
            rs = slice(n * lc, (n + 1) * lc)
            mixed = jnp.dot(w, vb[rs, cs], preferred_element_type=F32) + bias
            o_ref[rs, cs] = (u[rs, cs] * mixed).astype(BF16)


def _gmlp(x, wu, wv, ln_g, ln_b, ws, bst, *, tm, lc):
    n = x.shape[0]
    row = lambda i: (i, 0)
    full2 = lambda i: (0, 0)
    return pl.pallas_call(
        functools.partial(_gmlp_kernel, tm=tm, lc=lc),
        grid=(n // tm,),
        in_specs=[pl.BlockSpec((tm, D_MODEL), row),
                  pl.BlockSpec((D_MODEL, SG_WIDTH), full2),
                  pl.BlockSpec((D_MODEL, SG_WIDTH), full2),
                  pl.BlockSpec((1, SG_WIDTH), full2),
                  pl.BlockSpec((1, SG_WIDTH), full2),
                  pl.BlockSpec((SG_GROUPS, lc, lc), lambda i: (0, 0, 0)),
                  pl.BlockSpec((lc, SG_GROUPS), full2)],
        out_specs=[pl.BlockSpec((tm, SG_WIDTH), row), pl.BlockSpec((tm, SG_WIDTH), row)],
        out_shape=[jax.ShapeDtypeStruct((n, SG_WIDTH), BF16),
                   jax.ShapeDtypeStruct((n, SG_WIDTH), F32)],
        compiler_params=_params(("parallel",)),
        name="gmlp",
    )(x, wu, wv, ln_g, ln_b, ws, bst)


def _hgrn_kernel(x_ref, w_ref, loglb_ref, log1mlb_ref, omlb_ref, ng_ref, s0_ref, c_ref, sout_ref,
                 q_scr, k_scr, v_scr, lf_scr, o_scr, st_scr, *, tc):
    t = pl.program_id(1)

    @pl.when(t == 0)
    def _():
        for h in range(HG_HEADS):
            st_scr[h] = s0_ref[0, h].T

    xb = x_ref[0].astype(BF16)
    z = jnp.dot(xb, w_ref[...], preferred_element_type=F32)
    hq = z[:, 0:HG_WIDTH]
    hf = z[:, HG_WIDTH:2 * HG_WIDTH]
    hg = z[:, 3 * HG_WIDTH:4 * HG_WIDTH]
    q_scr[...] = hq * _sigmoid(hq) * HG_DK ** -0.5
    u = jnp.exp(-jnp.abs(hf))
    w = 1.0 + u
    log_sig = jnp.minimum(hf, 0.0) - jnp.log(w)
    y = log1mlb_ref[...] + log_sig
    a = loglb_ref[...]
    lf_scr[...] = jnp.maximum(a, y) + jnp.log(1.0 + jnp.exp(-jnp.abs(a - y)))
    k_scr[...] = omlb_ref[...] * (jnp.where(hf >= 0.0, u, 1.0) / w)
    v_scr[...] = z[:, 2 * HG_WIDTH:3 * HG_WIDTH]

    ch = min(tc, CHUNK)
    ri = lax.broadcasted_iota(I32, (tc, tc), 0)
    ci = lax.broadcasted_iota(I32, (tc, tc), 1)
    same = (ri // ch) == (ci // ch)
    causal = same & (ri >= ci)
    lf = lf_scr[...]
    lf_hi = lf.astype(BF16)
    lf_r = lf - lf_hi.astype(F32)
    lf_mid = lf_r.astype(BF16)
    lf_lo = (lf_r - lf_mid.astype(F32)).astype(BF16)
    cmask = jnp.where(causal, 1.0, 0.0).astype(BF16)
    bcum_all = (jnp.dot(cmask, lf_hi, preferred_element_type=F32)
                + jnp.dot(cmask, lf_mid, preferred_element_type=F32)
                + jnp.dot(cmask, lf_lo, preferred_element_type=F32))
    factored = jnp.min(bcum_all) > -HG_FACTOR_RANGE

    @pl.when(factored)
    def _():
        btot = jnp.concatenate(
            [jnp.broadcast_to(bcum_all[(n + 1) * ch - 1:(n + 1) * ch, :], (ch, HG_WIDTH))
             for n in range(tc // ch)], axis=0)
        qa = q_scr[...]
        ka = k_scr[...]
        qd = (qa * jnp.exp(bcum_all)).astype(BF16)
        ki = (ka * jnp.exp(-bcum_all)).astype(BF16)
        kd = (ka * jnp.exp(btot - bcum_all)).astype(BF16)
        va = v_scr[...].astype(BF16)
        for h in range(HG_HEADS):
            cs = slice(h * HG_DK, (h + 1) * HG_DK)
            att = lax.dot_general(qd[:, cs], ki[:, cs], _NT, preferred_element_type=F32)
            att = jnp.where(causal, att, 0.0).astype(BF16)
            o = jnp.dot(att, va[:, cs], preferred_element_type=F32)
            st = st_scr[h]
            for n in range(tc // ch):
                rs = slice(n * ch, (n + 1) * ch)
                o_scr[rs, cs] = o[rs] + lax.dot_general(qd[rs, cs], st.astype(BF16), _NT,
                                                        preferred_element_type=F32)
                upd = lax.dot_general(va[rs, cs], kd[rs, cs], _TN, preferred_element_type=F32)
                st = jnp.exp(btot[n * ch:n * ch + 1, cs]) * st + upd
            st_scr[h] = st

    nb = HG_BLOCK
    tri = jnp.where(lax.broadcasted_iota(I32, (nb, nb), 0) >= lax.broadcasted_iota(I32, (nb, nb), 1),
                    1.0, 0.0).astype(F32)
    trow = lax.broadcasted_iota(I32, (nb, 1), 0)

    def block(i, carry):
        r0 = pl.multiple_of(i * nb, nb)
        rows = pl.ds(r0, nb)
        for h in range(HG_HEADS):
            cs = slice(h * HG_DK, (h + 1) * HG_DK)
            bcum = jnp.dot(tri, lf_scr[rows, cs], preferred_element_type=F32,
                           precision=lax.Precision.HIGHEST)
            qb = q_scr[rows, cs]
            kb = k_scr[rows, cs]
            vb = v_scr[rows, cs]
            o = jnp.zeros((nb, HG_DV), F32)
            for s in range(nb):
                e = jnp.exp(jnp.where(trow >= s, bcum - bcum[s:s + 1, :], -jnp.inf))
                a_ts = jnp.sum(qb * kb[s:s + 1, :] * e, axis=1, keepdims=True)
                o = o + a_ts * vb[s:s + 1, :]
            st = st_scr[h]
            qd = (qb * jnp.exp(bcum)).astype(BF16)
            o = o + lax.dot_general(qd, st.astype(BF16), _NT, preferred_element_type=F32)
            blast = bcum[nb - 1:nb, :]
            kd = (kb * jnp.exp(blast - bcum)).astype(BF16)
            upd = lax.dot_general(vb.astype(BF16), kd, _TN, preferred_element_type=F32)
            st_scr[h] = jnp.exp(blast) * st + upd
            o_scr[rows, cs] = o
        return carry

    @pl.when(jnp.logical_not(factored))
    def _():
        lax.fori_loop(0, tc // nb, block, 0)

    for h in range(HG_HEADS):
        cs = slice(h * HG_DV, (h + 1) * HG_DV)
        o = o_scr[:, cs]
        o = o * lax.rsqrt(jnp.mean(o * o, axis=-1, keepdims=True) + LN_EPS)
        g = hg[:, cs]
        c_ref[0, :, cs] = (o * ng_ref[:, cs] * (g * _sigmoid(g))).astype(BF16)

    @pl.when(t == pl.num_programs(1) - 1)
    def _():
        for h in range(HG_HEADS):
            sout_ref[0, h] = st_scr[h].T


def _hgrn(x, w, loglb, log1mlb, omlb, ng, s0, *, tc):
    b, t, _ = x.shape
    xmap = lambda bi, j: (bi, j, 0)
    vec = pl.BlockSpec((1, HG_WIDTH), lambda bi, j: (0, 0))
    smap = lambda bi, j: (bi, 0, 0, 0)
    return pl.pallas_call(
        functools.partial(_hgrn_kernel, tc=tc),
        grid=(b, t // tc),
        in_specs=[pl.BlockSpec((1, tc, D_MODEL), xmap),
                  pl.BlockSpec((D_MODEL, 4 * HG_WIDTH), lambda bi, j: (0, 0)),
                  vec, vec, vec, vec,
                  pl.BlockSpec((1, HG_HEADS, HG_DK, HG_DV), smap)],
        out_specs=[pl.BlockSpec((1, tc, HG_WIDTH), xmap),
                   pl.BlockSpec((1, HG_HEADS, HG_DK, HG_DV), smap)],
        out_shape=[jax.ShapeDtypeStruct((b, t, HG_WIDTH), BF16),
                   jax.ShapeDtypeStruct((b, HG_HEADS, HG_DK, HG_DV), F32)],
        scratch_shapes=[pltpu.VMEM((tc, HG_WIDTH), F32)] * 5
                       + [pltpu.VMEM((HG_HEADS, HG_DV, HG_DK), F32)],
        compiler_params=_params(("parallel", "arbitrary")),
        name="hgrn",
    )(x, w, loglb, log1mlb, omlb, ng, s0)


def _merge_kernel(x_ref, a_ref, b_ref, c_ref, wg_ref, wa_ref, wb_ref, wc_ref, wo_ref,
                  g_ref, bt_ref, y_ref):
    x = x_ref[...]
    xb = x.astype(BF16)
    m = None
    for i, (br, w) in enumerate(((a_ref, wa_ref), (b_ref, wb_ref), (c_ref, wc_ref))):
        gate = _sigmoid(jnp.dot(xb, wg_ref[:, i * D_MODEL:(i + 1) * D_MODEL],
                                preferred_element_type=F32))
        term = gate * jnp.dot(br[...], w[...], preferred_element_type=F32)
        m = term if m is None else m + term
    mo = jnp.dot(m.astype(BF16), wo_ref[...], preferred_element_type=F32)
    y_ref[...] = _layer_norm(ALPHA * x + mo, g_ref[...], bt_ref[...])


def _merge(x, a, b, c, wg, wa, wb, wc, wo, ln_g, ln_b, *, tm):
    n = x.shape[0]
    row = lambda i: (i, 0)
    full = lambda i: (0, 0)
    br = pl.BlockSpec((tm, ATT_WIDTH), row)
    bw = pl.BlockSpec((ATT_WIDTH, D_MODEL), full)
    vec = pl.BlockSpec((1, D_MODEL), full)
    return pl.pallas_call(
        _merge_kernel,
        grid=(n // tm,),
        in_specs=[pl.BlockSpec((tm, D_MODEL), row), br, br, br,
                  pl.BlockSpec((D_MODEL, N_BRANCH * D_MODEL), full), bw, bw, bw,
                  pl.BlockSpec((D_MODEL, D_MODEL), full), vec, vec],
        out_specs=pl.BlockSpec((tm, D_MODEL), row),
        out_shape=jax.ShapeDtypeStruct((n, D_MODEL), F32),
        compiler_params=_params(("parallel",)),
        name="merge",
    )(x, a, b, c, wg, wa, wb, wc, wo, ln_g, ln_b)


def _route(logits):
    ex = jnp.exp(logits - jnp.max(logits, axis=0, keepdims=True))
    probs = ex / jnp.sum(ex, axis=0, keepdims=True)
    p = [[probs[g * EXP_PER_GROUP + k:g * EXP_PER_GROUP + k + 1, :] for k in range(EXP_PER_GROUP)]
         for g in range(N_GROUPS)]
    score = []
    for g in range(N_GROUPS):
        best = None
        for k1 in range(EXP_PER_GROUP):
            for k2 in range(k1 + 1, EXP_PER_GROUP):
                pair = p[g][k1] + p[g][k2]
                best = pair if best is None else jnp.maximum(best, pair)
        score.append(best)
    gsel = jnp.zeros(score[0].shape, I32)
    top = score[0]
    for g in range(1, N_GROUPS):
        better = score[g] > top
        top = jnp.where(better, score[g], top)
        gsel = jnp.where(better, g, gsel)
    val = []
    for k in range(EXP_PER_GROUP):
        v = p[0][k]
        for g in range(1, N_GROUPS):
            v = jnp.where(gsel == g, p[g][k], v)
        val.append(v)
    v1, i1 = val[0], jnp.zeros(gsel.shape, I32)
    for k in range(1, EXP_PER_GROUP):
        better = val[k] > v1
        v1 = jnp.where(better, val[k], v1)
        i1 = jnp.where(better, k, i1)
    v2, i2 = jnp.full(v1.shape, -1.0, F32), jnp.zeros(gsel.shape, I32)
    for k in range(EXP_PER_GROUP):
        better = (i1 != k) & (val[k] > v2)
        v2 = jnp.where(better, val[k], v2)
        i2 = jnp.where(better, k, i2)
    den = v1 + v2
    w1, w2 = v1 / den, v2 / den
    out = []
    for g in range(N_GROUPS):
        rows = [jnp.where(gsel == g, jnp.where(i1 == k, w1, jnp.where(i2 == k, w2, 0.0)), 0.0)
                for k in range(EXP_PER_GROUP)]
        out.append(jnp.concatenate(rows, axis=0))
    return out


def _moe_kernel(x_ref, wr_ref, br_ref, wg_ref, wu_ref, wd_ref, g_ref, bt_ref, y_ref,
                xb_scr, gate_scr, acc_scr, *, tm):
    grp = pl.program_id(1)

    @pl.when(grp == 0)
    def _():
        xb = x_ref[...].astype(BF16)
        xb_scr[...] = xb
        logits = lax.dot_general(wr_ref[...], xb, _NT, preferred_element_type=F32) + br_ref[...]
        gates = _route(logits)
        pad = jnp.zeros((LANES - EXP_PER_GROUP, tm), F32)
        for g in range(N_GROUPS):
            gate_scr[g] = jnp.concatenate([gates[g], pad], axis=0).T
        acc_scr[...] = jnp.zeros(acc_scr.shape, F32)

    xb = xb_scr[...]
    gate = gate_scr[grp]
    acc = acc_scr[...]
    for k in range(EXP_PER_GROUP):
        h = jnp.dot(xb, wg_ref[k], preferred_element_type=F32)
        up = jnp.dot(xb, wu_ref[k], preferred_element_type=F32)
        act = h * _sigmoid(h) * up * gate[:, k:k + 1]
        acc = acc + jnp.dot(act.astype(BF16), wd_ref[k], preferred_element_type=F32)
    acc_scr[...] = acc

    @pl.when(grp == N_GROUPS - 1)
    def _():
        y_ref[...] = _layer_norm(ALPHA * x_ref[...] + acc, g_ref[...], bt_ref[...])


def _moe(x, wr_t, br, wg, wu, wd, ln_g, ln_b, *, tm):
    n = x.shape[0]
    row = lambda i, g: (i, 0)
    full = lambda i, g: (0, 0)
    vec = pl.BlockSpec((1, D_MODEL), full)
    return pl.pallas_call(
        functools.partial(_moe_kernel, tm=tm),
        grid=(n // tm, N_GROUPS),
        in_specs=[pl.BlockSpec((tm, D_MODEL), row),
                  pl.BlockSpec((N_EXPERTS, D_MODEL), full),
                  pl.BlockSpec((N_EXPERTS, 1), full),
                  pl.BlockSpec((EXP_PER_GROUP, D_MODEL, D_FF), lambda i, g: (g, 0, 0)),
                  pl.BlockSpec((EXP_PER_GROUP, D_MODEL, D_FF), lambda i, g: (g, 0, 0)),
                  pl.BlockSpec((EXP_PER_GROUP, D_FF, D_MODEL), lambda i, g: (g, 0, 0)),
                  vec, vec],
        out_specs=pl.BlockSpec((tm, D_MODEL), row),
        out_shape=jax.ShapeDtypeStruct((n, D_MODEL), F32),
        scratch_shapes=[pltpu.VMEM((tm, D_MODEL), BF16),
                        pltpu.VMEM((N_GROUPS, tm, LANES), F32),
                        pltpu.VMEM((tm, D_MODEL), F32)],
        compiler_params=_params(("parallel", "arbitrary")),
        name="moe",
    )(x, wr_t, br, wg, wu, wd, ln_g, ln_b)


WPREP_ROWS = 128


def _split_w_in_kernel(w_ref, wn_ref, wt_ref, wu_ref, wv_ref, whg_ref, wg_ref):
    w = w_ref[0]
    offs = [0]
    for n in SPLIT_SIZES:
        offs.append(offs[-1] + n)
    seg = lambda i, j=None: w[:, offs[i]:offs[i + 1 if j is None else j]]
    zeros = lambda c: jnp.zeros((w.shape[0], c), w.dtype)
    wn_ref[0] = jnp.concatenate([seg(1), seg(2), seg(4), zeros(LANES - IDX_DIM)], axis=1).astype(BF16)
    wt = jnp.concatenate([seg(0), seg(3), seg(2), seg(5), zeros(ATT_T_PAD - ATT_T_ROWS + 8 - IDX_HEADS)],
                         axis=1)
    for c in range(ATT_T_PAD // LANES):
        rows = min(LANES, ATT_T_ROWS - c * LANES)
        wt_ref[0, c * LANES:c * LANES + rows, :] = wt[:, c * LANES:(c + 1) * LANES].T[:rows].astype(BF16)
    wu_ref[0] = seg(6).astype(BF16)
    wv_ref[0] = seg(7).astype(BF16)
    whg_ref[0] = seg(8, 12).astype(BF16)
    wg_ref[0] = seg(12).astype(BF16)


def _split_w_in(w_in):
    depth, d, ncol = w_in.shape
    tr = WPREP_ROWS
    cols = (ATT_N_COLS, None, SG_WIDTH, SG_WIDTH, 4 * HG_WIDTH, N_BRANCH * D_MODEL)
    shapes = [(depth, ATT_T_ROWS, d) if c is None else (depth, d, c) for c in cols]
    specs = [pl.BlockSpec((1, ATT_T_ROWS, tr), lambda l, i: (l, 0, i)) if c is None
             else pl.BlockSpec((1, tr, c), lambda l, i: (l, i, 0)) for c in cols]
    return pl.pallas_call(
        _split_w_in_kernel,
        grid=(depth, d // tr),
        in_specs=[pl.BlockSpec((1, tr, ncol), lambda l, i: (l, i, 0))],
        out_specs=specs,
        out_shape=[jax.ShapeDtypeStruct(sh, BF16) for sh in shapes],
        compiler_params=_params(("parallel", "parallel")),
        name="split_w_in",
    )(w_in)


def _cast_kernel(*refs):
    n = len(refs) // 2
    for src, dst in zip(refs[:n], refs[n:]):
        dst[...] = src[...].astype(BF16)


def _experts_to_bf16(*ws):
    depth, e = ws[0].shape[:2]
    spec = lambda a: pl.BlockSpec((1, 1) + a.shape[2:], lambda l, j: (l, j, 0, 0))
    return pl.pallas_call(
        _cast_kernel,
        grid=(depth, e),
        in_specs=[spec(a) for a in ws],
        out_specs=[spec(a) for a in ws],
        out_shape=[jax.ShapeDtypeStruct(a.shape, BF16) for a in ws],
        compiler_params=_params(("parallel", "parallel")),
        name="experts_to_bf16",
    )(*ws)


def _row_tile(n, want):
    while n % want:
        want //= 2
    return want


def _layer(x, bsz, t, lw, *, pos0, topk, cache, s0):
    n = bsz * t
    lc = min(t, SG_LEN)
    k, v, ki, kb, kib, qt, qit, vt, wit = _attn_proj(x, lw["w_att_n"], lw["w_att_t"], bsz, t,
                                                     _row_tile(t, 1024))
    kb3 = kb.reshape(bsz, t, KV_WIDTH)
    kib3 = kib.reshape(bsz, t, IDX_DIM)
    if cache is not None:
        ck, cvt, cki = cache
        kb3 = jnp.concatenate([ck, kb3], axis=1)
        kib3 = jnp.concatenate([cki, kib3], axis=1)
        vt = jnp.concatenate([cvt, vt], axis=2)
    ltot = kb3.shape[1]
    lp = -(-ltot // DSA_KEY_BLOCK) * DSA_KEY_BLOCK
    tq = DSA_QUERY_TILE if t % DSA_QUERY_TILE == 0 else LANES
    tp = -(-t // tq) * tq
    padq = lambda a: jnp.pad(a, ((0, 0), (0, 0), (0, tp - t)))
    a = _dsa(padq(qt), padq(qit), padq(wit), jnp.pad(kb3, ((0, 0), (0, lp - ltot), (0, 0))),
             jnp.pad(vt, ((0, 0), (0, 0), (0, lp - ltot))),
             jnp.pad(kib3, ((0, 0), (0, lp - ltot), (0, 0))), tq=tq, pos0=pos0, topk=topk)[:, :t]
    b, v_gm = _gmlp(x, lw["w_u"], lw["w_v"], lw["ln_sg_g"], lw["ln_sg_b"],
                    lw["w_sg"][:, :lc, :lc], lw["b_sg"][:, :lc].T, tm=_row_tile(n, 512), lc=lc)
    c, s_new = _hgrn(x.reshape(bsz, t, D_MODEL), lw["w_hg"], lw["log_lb"], lw["log1m_lb"],
                     lw["om_lb"], lw["hg_norm_g"], s0, tc=min(t, 256))
    x1 = _merge(x, a.reshape(n, ATT_WIDTH), b, c.reshape(n, HG_WIDTH), lw["w_gates"],
                lw["w_branch_a"], lw["w_branch_b"], lw["w_branch_c"], lw["w_out"],
                lw["ln1_g"], lw["ln1_b"], tm=_row_tile(n, 512))
    x2 = _moe(x1, lw["w_router_t"], lw["b_router"], lw["w_exp_gate"], lw["w_exp_up"],
              lw["w_exp_down"], lw["ln2_g"], lw["ln2_b"], tm=_row_tile(n, 1024))
    return x2, k, v, ki, s_new, v_gm


def kernel(x_prompt, x_sample, cache_k, cache_v, cache_kidx, state_hgrn, w_in, w_sg, b_sg, ln_sg_g, ln_sg_b, hg_lb_logits, hg_norm_g, w_branch_a, w_branch_b, w_branch_c, w_out, ln1_g, ln1_b, w_router, b_router, w_exp_gate, w_exp_up, w_exp_down, ln2_g, ln2_b):
    bp, sp, _ = x_prompt.shape
    bs, ss, _ = x_sample.shape
    past = cache_k.shape[2]
    topk_p = min(TOPK_MAX, sp // 4)
    topk_s = min(TOPK_MAX, (past + ss) // 4)

    lb_all = jnp.cumsum(jax.nn.softmax(hg_lb_logits.astype(F32), axis=0), axis=0)
    lb_all = lb_all - lb_all[0:1]
    vec = lambda a: a.reshape(1, -1)

    xp = x_prompt.reshape(bp * sp, D_MODEL)
    xs = x_sample.reshape(bs * ss, D_MODEL)
    s0_p = jnp.zeros((bp, HG_HEADS, HG_DK, HG_DV), F32)
    outs_p, outs_s = [], []
    w_att_n, w_att_t, w_u, w_v, w_hg, w_gates = _split_w_in(w_in)
    wg_b, wu_b, wd_b = _experts_to_bf16(w_exp_gate, w_exp_up, w_exp_down)
    for l in range(DEPTH):
        lw = dict(
            w_att_n=w_att_n[l], w_att_t=w_att_t[l], w_u=w_u[l], w_v=w_v[l], w_hg=w_hg[l],
            w_gates=w_gates[l],
            ln_sg_g=vec(ln_sg_g[l]), ln_sg_b=vec(ln_sg_b[l]), w_sg=w_sg[l], b_sg=b_sg[l],
            log_lb=vec(jnp.log(lb_all[l])), log1m_lb=vec(jnp.log1p(-lb_all[l])),
            om_lb=vec(1.0 - lb_all[l]), hg_norm_g=vec(hg_norm_g[l].astype(F32)),
            w_branch_a=w_branch_a[l].astype(BF16), w_branch_b=w_branch_b[l].astype(BF16),
            w_branch_c=w_branch_c[l].astype(BF16), w_out=w_out[l].astype(BF16),
            ln1_g=vec(ln1_g[l]), ln1_b=vec(ln1_b[l]),
            w_router_t=w_router.T.astype(BF16), b_router=b_router.astype(F32).reshape(-1, 1),
            w_exp_gate=wg_b[l], w_exp_up=wu_b[l], w_exp_down=wd_b[l],
            ln2_g=vec(ln2_g[l]), ln2_b=vec(ln2_b[l]))
        xp, k, v, ki, s_new, _ = _layer(xp, bp, sp, lw, pos0=0, topk=topk_p, cache=None, s0=s0_p)
        outs_p.append((k.reshape(bp, sp, KV_HEADS, HEAD_DIM), v.reshape(bp, sp, KV_HEADS, HEAD_DIM),
                       ki.reshape(bp, sp, IDX_DIM), s_new))
        cache = (cache_k[l].reshape(bs, past, KV_WIDTH).astype(BF16),
                 jnp.swapaxes(cache_v[l].reshape(bs, past, KV_WIDTH), 1, 2).astype(BF16),
                 cache_kidx[l].astype(BF16))
        xs, k, v, ki, s_new, v_gm = _layer(xs, bs, ss, lw, pos0=past, topk=topk_s, cache=cache,
                                           s0=state_hgrn[l].astype(F32))
        outs_s.append((k.reshape(bs, ss, KV_HEADS, HEAD_DIM), v.reshape(bs, ss, KV_HEADS, HEAD_DIM),
                       ki.reshape(bs, ss, IDX_DIM), s_new, v_gm.reshape(bs, ss, SG_WIDTH)))

    stack = lambda rows, i: jnp.stack([r[i] for r in rows])
    return (xp.reshape(bp, sp, D_MODEL), xs.reshape(bs, ss, D_MODEL),
            stack(outs_p, 0), stack(outs_p, 1), stack(outs_p, 2), stack(outs_p, 3),
            stack(outs_s, 0), stack(outs_s, 1), stack(outs_s, 2), stack(outs_s, 3),
            stack(outs_s, 4))
```

```python
import functools

import jax
import jax.numpy as jnp
from jax import lax
from jax.experimental import pallas as pl
from jax.experimental.pallas import tpu as pltpu

F32 = jnp.float32
BF16 = jnp.bfloat16
I32 = jnp.int32

D_MODEL = 1024
DEPTH = 4
CHUNK = 64
ATT_HEADS = 8
KV_HEADS = 2
HEAD_DIM = 64
ATT_WIDTH = ATT_HEADS * HEAD_DIM
KV_WIDTH = KV_HEADS * HEAD_DIM
IDX_HEADS = 4
IDX_DIM = 64
TOPK_MAX = 256
SG_LEN = 128
SG_GROUPS = 4
SG_WIDTH = 512
SG_GW = SG_WIDTH // SG_GROUPS
HG_HEADS = 4
HG_DK = 128
HG_DV = 128
HG_WIDTH = HG_HEADS * HG_DV
HG_BLOCK = 16
HG_FACTOR_RANGE = 80.0
N_BRANCH = 3
N_EXPERTS = 16
N_GROUPS = 4
EXP_PER_GROUP = 4
D_FF = 256
ALPHA = (2 * DEPTH) ** 0.25
LN_EPS = 1e-5

SPLIT_SIZES = (ATT_WIDTH, KV_WIDTH, KV_WIDTH, IDX_HEADS * IDX_DIM, IDX_DIM, IDX_HEADS,
               SG_WIDTH, SG_WIDTH, HG_HEADS * HG_DK, HG_HEADS * HG_DK, HG_WIDTH, HG_WIDTH,
               N_BRANCH * D_MODEL)

LANES = 128
INT_MIN = -2 ** 31
NEG_BIG = -1e30
LOG2_E = 1.4426950408889634
DSA_KEY_BLOCK = 512
DSA_QUERY_TILE = 512
VMEM_LIMIT = 56 * 1024 * 1024

_NT = (((1,), (1,)), ((), ()))
_TN = (((0,), (0,)), ((), ()))


def _params(sem):
    return pltpu.CompilerParams(dimension_semantics=sem, vmem_limit_bytes=VMEM_LIMIT)


def _layer_norm(x, g, b):
    mu = jnp.mean(x, axis=-1, keepdims=True)
    d = x - mu
    var = jnp.mean(d * d, axis=-1, keepdims=True)
    return d * lax.rsqrt(var + LN_EPS) * g + b


def _gelu_tanh(x):
    return 0.5 * x * (1.0 + jnp.tanh(0.7978845608028654 * (x + 0.044715 * (x * x * x))))


def _sigmoid(x):
    return 0.5 * jnp.tanh(0.5 * x) + 0.5


def _fold_rows(x, op):
    parts = [x[i:i + 8] for i in range(0, x.shape[0], 8)]
    while len(parts) > 1:
        parts = [op(parts[i], parts[i + 1]) for i in range(0, len(parts), 2)]
    return parts[0]


ATT_T_ROWS = ATT_WIDTH + IDX_HEADS * IDX_DIM + KV_WIDTH + 8
ATT_T_PAD = -(-ATT_T_ROWS // LANES) * LANES
ATT_N_COLS = 3 * LANES


def _attn_proj_kernel(x_ref, wn_ref, wt_ref, k_ref, v_ref, ki_ref, kb_ref, kib_ref,
                      qt_ref, qit_ref, vt_ref, wit_ref):
    xb = x_ref[...].astype(BF16)
    z = jnp.dot(xb, wn_ref[...], preferred_element_type=F32)
    k = z[:, 0:KV_WIDTH]
    ki = z[:, 2 * KV_WIDTH:2 * KV_WIDTH + IDX_DIM]
    k_ref[...] = k
    v_ref[...] = z[:, KV_WIDTH:2 * KV_WIDTH]
    ki_ref[...] = ki
    kb_ref[...] = k.astype(BF16)
    kib_ref[...] = ki.astype(BF16)
    zt = lax.dot_general(wt_ref[...], xb, _NT, preferred_element_type=F32)
    o = 0
    qt_ref[0] = (zt[o:o + ATT_WIDTH] * (HEAD_DIM ** -0.5 * LOG2_E)).astype(BF16)
    o += ATT_WIDTH
    qit_ref[0] = (zt[o:o + IDX_HEADS * IDX_DIM] * IDX_DIM ** -0.5).astype(BF16)
    o += IDX_HEADS * IDX_DIM
    vt_ref[0] = zt[o:o + KV_WIDTH].astype(BF16)
    o += KV_WIDTH
    wit_ref[0] = zt[o:o + 8] * IDX_HEADS ** -0.5


def _attn_proj(x, wn, wt, bsz, t, tm):
    n = bsz * t
    per = t // tm
    row = lambda i: (i, 0)
    tmap = lambda i: (i // per, 0, i % per)
    nat = ((KV_WIDTH, F32), (KV_WIDTH, F32), (IDX_DIM, F32), (KV_WIDTH, BF16), (IDX_DIM, BF16))
    tr = ((ATT_WIDTH, BF16), (IDX_HEADS * IDX_DIM, BF16), (KV_WIDTH, BF16), (8, F32))
    return pl.pallas_call(
        _attn_proj_kernel,
        grid=(n // tm,),
        in_specs=[pl.BlockSpec((tm, D_MODEL), row),
                  pl.BlockSpec((D_MODEL, ATT_N_COLS), lambda i: (0, 0)),
                  pl.BlockSpec((ATT_T_ROWS, D_MODEL), lambda i: (0, 0))],
        out_specs=[pl.BlockSpec((tm, c), row) for c, _ in nat]
                  + [pl.BlockSpec((1, c, tm), tmap) for c, _ in tr],
        out_shape=[jax.ShapeDtypeStruct((n, c), d) for c, d in nat]
                  + [jax.ShapeDtypeStruct((bsz, c, t), d) for c, d in tr],
        compiler_params=_params(("parallel",)),
        name="attn_proj",
    )(x, wn, wt)


def _dsa_kernel(qt_ref, qit_ref, wit_ref, kb_ref, vt_ref, kib_ref, o_ref,
                key_scr, sort_scr, bias_scr, s_scr, acc_scr, m_scr, a_scr,
                *, tq, kblk, pos0, topk):
    j = pl.program_id(1)
    qpos0 = pos0 + j * tq
    qcol = lax.broadcasted_iota(I32, (1, tq), 1)
    chunk_end = ((qpos0 + qcol) // CHUNK + 1) * CHUNK
    lvis = ((qpos0 + tq - 1) // CHUNK + 1) * CHUNK
    nblk = (lvis + kblk - 1) // kblk
    krow = lax.broadcasted_iota(I32, (kblk, tq), 0)

    wi = wit_ref[0]
    qit = qit_ref[0]

    def score_block(i, carry):
        off = pl.multiple_of(i * kblk, kblk)
        kib = kib_ref[0, pl.ds(off, kblk), :]
        sc = jnp.zeros((kblk, tq), F32)
        for h in range(IDX_HEADS):
            raw = jnp.dot(kib, qit[h * IDX_DIM:(h + 1) * IDX_DIM, :], preferred_element_type=F32)
            sc = sc + jnp.maximum(raw, 0.0) * wi[h:h + 1, :]
        bits = pltpu.bitcast(sc, I32)
        key = bits ^ ((bits >> 31) & 0x7FFFFFFF)
        key = jnp.where(off + krow < chunk_end, key, INT_MIN)
        key_scr[i] = key
        for g in range(kblk // 32):
            a, b, c, d = (key[g * 32 + 8 * r:g * 32 + 8 * r + 8] for r in range(4))
            a, b = jnp.maximum(a, b), jnp.minimum(a, b)
            c, d = jnp.maximum(c, d), jnp.minimum(c, d)
            a, c = jnp.maximum(a, c), jnp.minimum(a, c)
            b, d = jnp.maximum(b, d), jnp.minimum(b, d)
            b, c = jnp.maximum(b, c), jnp.minimum(b, c)
            sort_scr[i, g * 32:(g + 1) * 32] = jnp.concatenate([a, b, c, d], axis=0)
        return carry

    lax.fori_loop(0, nblk, score_block, 0)

    def count_keys(above):
        def body(b, accs):
            ks = sort_scr[b]
            accs = list(accs)
            for g in range(kblk // 32):
                s0, s1, s2, s3 = (ks[g * 32 + 8 * r:g * 32 + 8 * r + 8] for r in range(4))
                n = jnp.where(above(s3), 4.0, jnp.where(above(s2), 3.0, jnp.where(
                    above(s1), 2.0, jnp.where(above(s0), 1.0, 0.0))))
                accs[g % 4] = accs[g % 4] + n
            return tuple(accs)

        accs = lax.fori_loop(0, nblk, body, tuple(jnp.zeros((8, tq), F32) for _ in range(4)))
        return jnp.sum((accs[0] + accs[1]) + (accs[2] + accs[3]), axis=0, keepdims=True)

    def key_bit(i, thr):
        cand = thr + jnp.left_shift(jnp.int32(1), 31 - i)
        return jnp.where(count_keys(lambda s: s >= cand) >= topk, cand, thr)

    thr = lax.fori_loop(0, 32, key_bit, jnp.full((1, tq), INT_MIN, I32))

    need = topk - count_keys(lambda s: s > thr)
    tri = jnp.where(lax.broadcasted_iota(I32, (kblk, kblk), 0) >= lax.broadcasted_iota(I32, (kblk, kblk), 1),
                    1.0, 0.0).astype(BF16)

    def select_block(i, taken):
        off = pl.multiple_of(i * kblk, kblk)
        kb = key_scr[i]
        tied = jnp.where(kb == thr, 1.0, 0.0)
        rank = taken + jnp.dot(tri, tied.astype(BF16), preferred_element_type=F32)
        tie = jnp.where(rank <= need, 0.0, NEG_BIG)
        bias = jnp.where(kb > thr, 0.0, jnp.where(kb == thr, tie, NEG_BIG))
        bias_scr[i] = jnp.where(off + krow < chunk_end, bias, NEG_BIG)
        return rank[kblk - 1:kblk, :]

    lax.fori_loop(0, nblk, select_block, jnp.zeros((1, tq), F32))

    m_scr[...] = jnp.full(m_scr.shape, NEG_BIG, F32)
    acc_scr[...] = jnp.zeros(acc_scr.shape, F32)
    rep = ATT_HEADS // KV_HEADS
    qt = qt_ref[0]
    qg = [jnp.concatenate([qt[(g * rep + r) * HEAD_DIM:(g * rep + r + 1) * HEAD_DIM, :]
                           for r in range(rep)], axis=1) for g in range(KV_HEADS)]
    ones = jnp.ones((16, kblk), BF16)

    def score_phase(i, buf):
        off = pl.multiple_of(i * kblk, kblk)
        bias = jnp.concatenate([bias_scr[i]] * rep, axis=1)
        kk = kb_ref[0, pl.ds(off, kblk), :]
        for g in range(KV_HEADS):
            s = jnp.dot(kk[:, g * HEAD_DIM:(g + 1) * HEAD_DIM], qg[g],
                        preferred_element_type=F32) + bias
            s_scr[buf, g] = s
            m_blk = jnp.max(_fold_rows(s, jnp.maximum), axis=0, keepdims=True)
            m_old = m_scr[g]
            m_new = jnp.maximum(m_old, m_blk)
            a_scr[g] = jnp.exp2(m_old - m_new)
            m_scr[g] = m_new

    def value_phase(i, buf):
        off = pl.multiple_of(i * kblk, kblk)
        vv = vt_ref[0, :, pl.ds(off, kblk)]
        for g in range(KV_HEADS):
            p = jnp.exp2(s_scr[buf, g] - m_scr[g]).astype(BF16)
            vg = jnp.concatenate([vv[g * HEAD_DIM:(g + 1) * HEAD_DIM, :], ones], axis=0)
            acc_scr[g] = a_scr[g] * acc_scr[g] + jnp.dot(vg, p, preferred_element_type=F32)

    score_phase(0, 0)

    def attn_pair(k, carry):
        i = 2 * k
        value_phase(i, 0)
        score_phase(i + 1, 1)
        value_phase(i + 1, 1)
        score_phase(jnp.minimum(i + 2, nblk - 1), 0)
        return carry

    lax.fori_loop(0, nblk // 2, attn_pair, 0)

    @pl.when(nblk % 2 == 1)
    def _():
        value_phase(nblk - 1, 0)

    heads = []
    for g in range(KV_HEADS):
        acc = acc_scr[g]
        og = acc[:HEAD_DIM] / acc[HEAD_DIM:HEAD_DIM + 1]
        heads += [og[:, r * tq:(r + 1) * tq] for r in range(rep)]
    o_ref[0] = jnp.concatenate(heads, axis=0).T.astype(BF16)


def _dsa(qt, qit, wit, kb, vt, kib, *, tq, pos0, topk):
    b, _, t = qt.shape
    lp = kb.shape[1]
    kblk = DSA_KEY_BLOCK
    assert lp % kblk == 0 and t % tq == 0 and tq % LANES == 0
    kern = functools.partial(_dsa_kernel, tq=tq, kblk=kblk, pos0=pos0, topk=topk)
    rep = ATT_HEADS // KV_HEADS
    qmap = lambda bi, j: (bi, 0, j)
    kmap = lambda bi, j: (bi, 0, 0)
    return pl.pallas_call(
        kern,
        grid=(b, t // tq),
        in_specs=[pl.BlockSpec((1, ATT_WIDTH, tq), qmap),
                  pl.BlockSpec((1, IDX_HEADS * IDX_DIM, tq), qmap),
                  pl.BlockSpec((1, 8, tq), qmap),
                  pl.BlockSpec((1, lp, KV_WIDTH), kmap),
                  pl.BlockSpec((1, KV_WIDTH, lp), kmap),
                  pl.BlockSpec((1, lp, IDX_DIM), kmap)],
        out_specs=pl.BlockSpec((1, tq, ATT_WIDTH), lambda bi, j: (bi, j, 0)),
        out_shape=jax.ShapeDtypeStruct((b, t, ATT_WIDTH), BF16),
        scratch_shapes=[pltpu.VMEM((lp // kblk, kblk, tq), I32),
                        pltpu.VMEM((lp // kblk, kblk, tq), I32),
                        pltpu.VMEM((lp // kblk, kblk, tq), F32),
                        pltpu.VMEM((2, KV_HEADS, kblk, rep * tq), F32),
                        pltpu.VMEM((KV_HEADS, HEAD_DIM + 16, rep * tq), F32),
                        pltpu.VMEM((KV_HEADS, 1, rep * tq), F32),
                        pltpu.VMEM((KV_HEADS, 1, rep * tq), F32)],
        compiler_params=_params(("parallel", "arbitrary")),
        name="dsa",
    )(qt, qit, wit, kb, vt, kib)


def _gmlp_kernel(x_ref, wu_ref, wv_ref, g_ref, b_ref, ws_ref, bst_ref, o_ref, v_ref, *, tm, lc):
    xb = x_ref[...].astype(BF16)
    u = _gelu_tanh(jnp.dot(xb, wu_ref[...], preferred_element_type=F32))
    v = _layer_norm(_gelu_tanh(jnp.dot(xb, wv_ref[...], preferred_element_type=F32)),
                    g_ref[...], b_ref[...])
    v_ref[...] = v
    vb = v.astype(BF16)
    r = lax.broadcasted_iota(I32, (lc, lc), 0)
    c = lax.broadcasted_iota(I32, (lc, lc), 1)
    for g in range(SG_GROUPS):
        w = jnp.where(r >= c, ws_ref[g], 0.0).astype(BF16)
        bias = bst_ref[:, g:g + 1]
        cs = slice(g * SG_GW, (g + 1) * SG_GW)
        for n in range(tm // lc):
            rs = slice(n * lc, (n + 1) * lc)
            mixed = jnp.dot(w, vb[rs, cs], preferred_element_type=F32) + bias
            o_ref[rs, cs] = (u[rs, cs] * mixed).astype(BF16)


def _gmlp(x, wu, wv, ln_g, ln_b, ws, bst, *, tm, lc):
    n = x.shape[0]
    row = lambda i: (i, 0)
    full2 = lambda i: (0, 0)
    return pl.pallas_call(
        functools.partial(_gmlp_kernel, tm=tm, lc=lc),
        grid=(n // tm,),
        in_specs=[pl.BlockSpec((tm, D_MODEL), row),
                  pl.BlockSpec((D_MODEL, SG_WIDTH), full2),
                  pl.BlockSpec((D_MODEL, SG_WIDTH), full2),
                  pl.BlockSpec((1, SG_WIDTH), full2),
                  pl.BlockSpec((1, SG_WIDTH), full2),
                  pl.BlockSpec((SG_GROUPS, lc, lc), lambda i: (0, 0, 0)),
                  pl.BlockSpec((lc, SG_GROUPS), full2)],
        out_specs=[pl.BlockSpec((tm, SG_WIDTH), row), pl.BlockSpec((tm, SG_WIDTH), row)],
        out_shape=[jax.ShapeDtypeStruct((n, SG_WIDTH), BF16),
                   jax.ShapeDtypeStruct((n, SG_WIDTH), F32)],
        compiler_params=_params(("parallel",)),
        name="gmlp",
    )(x, wu, wv, ln_g, ln_b, ws, bst)


def _hgrn_kernel(x_ref, w_ref, loglb_ref, log1mlb_ref, omlb_ref, ng_ref, s0_ref, c_ref, sout_ref,
                 q_scr, k_scr, v_scr, lf_scr, o_scr, st_scr, *, tc):
    t = pl.program_id(1)

    @pl.when(t == 0)
    def _():
        for h in range(HG_HEADS):
            st_scr[h] = s0_ref[0, h].T

    xb = x_ref[0].astype(BF16)
    z = jnp.dot(xb, w_ref[...], preferred_element_type=F32)
    hq = z[:, 0:HG_WIDTH]
    hf = z[:, HG_WIDTH:2 * HG_WIDTH]
    hg = z[:, 3 * HG_WIDTH:4 * HG_WIDTH]
    q_scr[...] = hq * _sigmoid(hq) * HG_DK ** -0.5
    u = jnp.exp(-jnp.abs(hf))
    w = 1.0 + u
    log_sig = jnp.minimum(hf, 0.0) - jnp.log(w)
    y = log1mlb_ref[...] + log_sig
    a = loglb_ref[...]
    lf_scr[...] = jnp.maximum(a, y) + jnp.log(1.0 + jnp.exp(-jnp.abs(a - y)))
    k_scr[...] = omlb_ref[...] * (jnp.where(hf >= 0.0, u, 1.0) / w)
    v_scr[...] = z[:, 2 * HG_WIDTH:3 * HG_WIDTH]

    ch = min(tc, CHUNK)
    ri = lax.broadcasted_iota(I32, (tc, tc), 0)
    ci = lax.broadcasted_iota(I32, (tc, tc), 1)
    same = (ri // ch) == (ci // ch)
    causal = same & (ri >= ci)
    lf = lf_scr[...]
    lf_hi = lf.astype(BF16)
    lf_r = lf - lf_hi.astype(F32)
    lf_mid = lf_r.astype(BF16)
    lf_lo = (lf_r - lf_mid.astype(F32)).astype(BF16)
    cmask = jnp.where(causal, 1.0, 0.0).astype(BF16)
    bcum_all = (jnp.dot(cmask, lf_hi, preferred_element_type=F32)
                + jnp.dot(cmask, lf_mid, preferred_element_type=F32)
                + jnp.dot(cmask, lf_lo, preferred_element_type=F32))
    factored = jnp.min(bcum_all) > -HG_FACTOR_RANGE

    @pl.when(factored)
    def _():
        btot = jnp.concatenate(
            [jnp.broadcast_to(bcum_all[(n + 1) * ch - 1:(n + 1) * ch, :], (ch, HG_WIDTH))
             for n in range(tc // ch)], axis=0)
        qa = q_scr[...]
        ka = k_scr[...]
        qd = (qa * jnp.exp(bcum_all)).astype(BF16)
        ki = (ka * jnp.exp(-bcum_all)).astype(BF16)
        kd = (ka * jnp.exp(btot - bcum_all)).astype(BF16)
        va = v_scr[...].astype(BF16)
        for h in range(HG_HEADS):
            cs = slice(h * HG_DK, (h + 1) * HG_DK)
            att = lax.dot_general(qd[:, cs], ki[:, cs], _NT, preferred_element_type=F32)
            att = jnp.where(causal, att, 0.0).astype(BF16)
            o = jnp.dot(att, va[:, cs], preferred_element_type=F32)
            st = st_scr[h]
            for n in range(tc // ch):
                rs = slice(n * ch, (n + 1) * ch)
                o_scr[rs, cs] = o[rs] + lax.dot_general(qd[rs, cs], st.astype(BF16), _NT,
                                                        preferred_element_type=F32)
                upd = lax.dot_general(va[rs, cs], kd[rs, cs], _TN, preferred_element_type=F32)
                st = jnp.exp(btot[n * ch:n * ch + 1, cs]) * st + upd
            st_scr[h] = st

    nb = HG_BLOCK
    tri = jnp.where(lax.broadcasted_iota(I32, (nb, nb), 0) >= lax.broadcasted_iota(I32, (nb, nb), 1),
                    1.0, 0.0).astype(F32)
    trow = lax.broadcasted_iota(I32, (nb, 1), 0)

    def block(i, carry):
        r0 = pl.multiple_of(i * nb, nb)
        rows = pl.ds(r0, nb)
        for h in range(HG_HEADS):
            cs = slice(h * HG_DK, (h + 1) * HG_DK)
            bcum = jnp.dot(tri, lf_scr[rows, cs], preferred_element_type=F32,
                           precision=lax.Precision.HIGHEST)
            qb = q_scr[rows, cs]
            kb = k_scr[rows, cs]
            vb = v_scr[rows, cs]
            o = jnp.zeros((nb, HG_DV), F32)
            for s in range(nb):
                e = jnp.exp(jnp.where(trow >= s, bcum - bcum[s:s + 1, :], -jnp.inf))
                a_ts = jnp.sum(qb * kb[s:s + 1, :] * e, axis=1, keepdims=True)
                o = o + a_ts * vb[s:s + 1, :]
            st = st_scr[h]
            qd = (qb * jnp.exp(bcum)).astype(BF16)
            o = o + lax.dot_general(qd, st.astype(BF16), _NT, preferred_element_type=F32)
            blast = bcum[nb - 1:nb, :]
            kd = (kb * jnp.exp(blast - bcum)).astype(BF16)
            upd = lax.dot_general(vb.astype(BF16), kd, _TN, preferred_element_type=F32)
            st_scr[h] = jnp.exp(blast) * st + upd
            o_scr[rows, cs] = o
        return carry

    @pl.when(jnp.logical_not(factored))
    def _():
        lax.fori_loop(0, tc // nb, block, 0)

    for h in range(HG_HEADS):
        cs = slice(h * HG_DV, (h + 1) * HG_DV)
        o = o_scr[:, cs]
        o = o * lax.rsqrt(jnp.mean(o * o, axis=-1, keepdims=True) + LN_EPS)
        g = hg[:, cs]
        c_ref[0, :, cs] = (o * ng_ref[:, cs] * (g * _sigmoid(g))).astype(BF16)

    @pl.when(t == pl.num_programs(1) - 1)
    def _():
        for h in range(HG_HEADS):
            sout_ref[0, h] = st_scr[h].T


def _hgrn(x, w, loglb, log1mlb, omlb, ng, s0, *, tc):
    b, t, _ = x.shape
    xmap = lambda bi, j: (bi, j, 0)
    vec = pl.BlockSpec((1, HG_WIDTH), lambda bi, j: (0, 0))
    smap = lambda bi, j: (bi, 0, 0, 0)
    return pl.pallas_call(
        functools.partial(_hgrn_kernel, tc=tc),
        grid=(b, t // tc),
        in_specs=[pl.BlockSpec((1, tc, D_MODEL), xmap),
                  pl.BlockSpec((D_MODEL, 4 * HG_WIDTH), lambda bi, j: (0, 0)),
                  vec, vec, vec, vec,
                  pl.BlockSpec((1, HG_HEADS, HG_DK, HG_DV), smap)],
        out_specs=[pl.BlockSpec((1, tc, HG_WIDTH), xmap),
                   pl.BlockSpec((1, HG_HEADS, HG_DK, HG_DV), smap)],
        out_shape=[jax.ShapeDtypeStruct((b, t, HG_WIDTH), BF16),
                   jax.ShapeDtypeStruct((b, HG_HEADS, HG_DK, HG_DV), F32)],
        scratch_shapes=[pltpu.VMEM((tc, HG_WIDTH), F32)] * 5
                       + [pltpu.VMEM((HG_HEADS, HG_DV, HG_DK), F32)],
        compiler_params=_params(("parallel", "arbitrary")),
        name="hgrn",
    )(x, w, loglb, log1mlb, omlb, ng, s0)


def _merge_kernel(x_ref, a_ref, b_ref, c_ref, wg_ref, wa_ref, wb_ref, wc_ref, wo_ref,
                  g_ref, bt_ref, y_ref):
    x = x_ref[...]
    xb = x.astype(BF16)
    m = None
    for i, (br, w) in enumerate(((a_ref, wa_ref), (b_ref, wb_ref), (c_ref, wc_ref))):
        gate = _sigmoid(jnp.dot(xb, wg_ref[:, i * D_MODEL:(i + 1) * D_MODEL],
                                preferred_element_type=F32))
        term = gate * jnp.dot(br[...], w[...], preferred_element_type=F32)
        m = term if m is None else m + term
    mo = jnp.dot(m.astype(BF16), wo_ref[...], preferred_element_type=F32)
    y_ref[...] = _layer_norm(ALPHA * x + mo, g_ref[...], bt_ref[...])


def _merge(x, a, b, c, wg, wa, wb, wc, wo, ln_g, ln_b, *, tm):
    n = x.shape[0]
    row = lambda i: (i, 0)
    full = lambda i: (0, 0)
    br = pl.BlockSpec((tm, ATT_WIDTH), row)
    bw = pl.BlockSpec((ATT_WIDTH, D_MODEL), full)
    vec = pl.BlockSpec((1, D_MODEL), full)
    return pl.pallas_call(
        _merge_kernel,
        grid=(n // tm,),
        in_specs=[pl.BlockSpec((tm, D_MODEL), row), br, br, br,
                  pl.BlockSpec((D_MODEL, N_BRANCH * D_MODEL), full), bw, bw, bw,
                  pl.BlockSpec((D_MODEL, D_MODEL), full), vec, vec],
        out_specs=pl.BlockSpec((tm, D_MODEL), row),
        out_shape=jax.ShapeDtypeStruct((n, D_MODEL), F32),
        compiler_params=_params(("parallel",)),
        name="merge",
    )(x, a, b, c, wg, wa, wb, wc, wo, ln_g, ln_b)


def _route(logits):
    ex = jnp.exp(logits - jnp.max(logits, axis=0, keepdims=True))
    probs = ex / jnp.sum(ex, axis=0, keepdims=True)
    p = [[probs[g * EXP_PER_GROUP + k:g * EXP_PER_GROUP + k + 1, :] for k in range(EXP_PER_GROUP)]
         for g in range(N_GROUPS)]
    score = []
    for g in range(N_GROUPS):
        best = None
        for k1 in range(EXP_PER_GROUP):
            for k2 in range(k1 + 1, EXP_PER_GROUP):
                pair = p[g][k1] + p[g][k2]
                best = pair if best is None else jnp.maximum(best, pair)
        score.append(best)
    gsel = jnp.zeros(score[0].shape, I32)
    top = score[0]
    for g in range(1, N_GROUPS):
        better = score[g] > top
        top = jnp.where(better, score[g], top)
        gsel = jnp.where(better, g, gsel)
    val = []
    for k in range(EXP_PER_GROUP):
        v = p[0][k]
        for g in range(1, N_GROUPS):
            v = jnp.where(gsel == g, p[g][k], v)
        val.append(v)
    v1, i1 = val[0], jnp.zeros(gsel.shape, I32)
    for k in range(1, EXP_PER_GROUP):
        better = val[k] > v1
        v1 = jnp.where(better, val[k], v1)
        i1 = jnp.where(better, k, i1)
    v2, i2 = jnp.full(v1.shape, -1.0, F32), jnp.zeros(gsel.shape, I32)
    for k in range(EXP_PER_GROUP):
        better = (i1 != k) & (val[k] > v2)
        v2 = jnp.where(better, val[k], v2)
        i2 = jnp.where(better, k, i2)
    den = v1 + v2
    w1, w2 = v1 / den, v2 / den
    out = []
    for g in range(N_GROUPS):
        rows = [jnp.where(gsel == g, jnp.where(i1 == k, w1, jnp.where(i2 == k, w2, 0.0)), 0.0)
                for k in range(EXP_PER_GROUP)]
        out.append(jnp.concatenate(rows, axis=0))
    return out


def _moe_kernel(x_ref, wr_ref, br_ref, wg_ref, wu_ref, wd_ref, g_ref, bt_ref, y_ref,
                xb_scr, gate_scr, acc_scr, *, tm):
    grp = pl.program_id(1)

    @pl.when(grp == 0)
    def _():
        xb = x_ref[...].astype(BF16)
        xb_scr[...] = xb
        logits = lax.dot_general(wr_ref[...], xb, _NT, preferred_element_type=F32) + br_ref[...]
        gates = _route(logits)
        pad = jnp.zeros((LANES - EXP_PER_GROUP, tm), F32)
        for g in range(N_GROUPS):
            gate_scr[g] = jnp.concatenate([gates[g], pad], axis=0).T
        acc_scr[...] = jnp.zeros(acc_scr.shape, F32)

    xb = xb_scr[...]
    gate = gate_scr[grp]
    acc = acc_scr[...]
    for k in range(EXP_PER_GROUP):
        h = jnp.dot(xb, wg_ref[k], preferred_element_type=F32)
        up = jnp.dot(xb, wu_ref[k], preferred_element_type=F32)
        act = h * _sigmoid(h) * up * gate[:, k:k + 1]
        acc = acc + jnp.dot(act.astype(BF16), wd_ref[k], preferred_element_type=F32)
    acc_scr[...] = acc

    @pl.when(grp == N_GROUPS - 1)
    def _():
        y_ref[...] = _layer_norm(ALPHA * x_ref[...] + acc, g_ref[...], bt_ref[...])


def _moe(x, wr_t, br, wg, wu, wd, ln_g, ln_b, *, tm):
    n = x.shape[0]
    row = lambda i, g: (i, 0)
    full = lambda i, g: (0, 0)
    vec = pl.BlockSpec((1, D_MODEL), full)
    return pl.pallas_call(
        functools.partial(_moe_kernel, tm=tm),
        grid=(n // tm, N_GROUPS),
        in_specs=[pl.BlockSpec((tm, D_MODEL), row),
                  pl.BlockSpec((N_EXPERTS, D_MODEL), full),
                  pl.BlockSpec((N_EXPERTS, 1), full),
                  pl.BlockSpec((EXP_PER_GROUP, D_MODEL, D_FF), lambda i, g: (g, 0, 0)),
                  pl.BlockSpec((EXP_PER_GROUP, D_MODEL, D_FF), lambda i, g: (g, 0, 0)),
                  pl.BlockSpec((EXP_PER_GROUP, D_FF, D_MODEL), lambda i, g: (g, 0, 0)),
                  vec, vec],
        out_specs=pl.BlockSpec((tm, D_MODEL), row),
        out_shape=jax.ShapeDtypeStruct((n, D_MODEL), F32),
        scratch_shapes=[pltpu.VMEM((tm, D_MODEL), BF16),
                        pltpu.VMEM((N_GROUPS, tm, LANES), F32),
                        pltpu.VMEM((tm, D_MODEL), F32)],
        compiler_params=_params(("parallel", "arbitrary")),
        name="moe",
    )(x, wr_t, br, wg, wu, wd, ln_g, ln_b)


WPREP_ROWS = 128


def _split_w_in_kernel(w_ref, wn_ref, wt_ref, wu_ref, wv_ref, whg_ref, wg_ref):
    w = w_ref[0]
    offs = [0]
    for n in SPLIT_SIZES:
        offs.append(offs[-1] + n)
    seg = lambda i, j=None: w[:, offs[i]:offs[i + 1 if j is None else j]]
    zeros = lambda c: jnp.zeros((w.shape[0], c), w.dtype)
    wn_ref[0] = jnp.concatenate([seg(1), seg(2), seg(4), zeros(LANES - IDX_DIM)], axis=1).astype(BF16)
    wt = jnp.concatenate([seg(0), seg(3), seg(2), seg(5), zeros(ATT_T_PAD - ATT_T_ROWS + 8 - IDX_HEADS)],
                         axis=1)
    for c in range(ATT_T_PAD // LANES):
        rows = min(LANES, ATT_T_ROWS - c * LANES)
        wt_ref[0, c * LANES:c * LANES + rows, :] = wt[:, c * LANES:(c + 1) * LANES].T[:rows].astype(BF16)
    wu_ref[0] = seg(6).astype(BF16)
    wv_ref[0] = seg(7).astype(BF16)
    whg_ref[0] = seg(8, 12).astype(BF16)
    wg_ref[0] = seg(12).astype(BF16)


def _split_w_in(w_in):
    depth, d, ncol = w_in.shape
    tr = WPREP_ROWS
    cols = (ATT_N_COLS, None, SG_WIDTH, SG_WIDTH, 4 * HG_WIDTH, N_BRANCH * D_MODEL)
    shapes = [(depth, ATT_T_ROWS, d) if c is None else (depth, d, c) for c in cols]
    specs = [pl.BlockSpec((1, ATT_T_ROWS, tr), lambda l, i: (l, 0, i)) if c is None
             else pl.BlockSpec((1, tr, c), lambda l, i: (l, i, 0)) for c in cols]
    return pl.pallas_call(
        _split_w_in_kernel,
        grid=(depth, d // tr),
        in_specs=[pl.BlockSpec((1, tr, ncol), lambda l, i: (l, i, 0))],
        out_specs=specs,
        out_shape=[jax.ShapeDtypeStruct(sh, BF16) for sh in shapes],
        compiler_params=_params(("parallel", "parallel")),
        name="split_w_in",
    )(w_in)


def _cast_kernel(*refs):
    n = len(refs) // 2
    for src, dst in zip(refs[:n], refs[n:]):
        dst[...] = src[...].astype(BF16)


def _experts_to_bf16(*ws):
    depth, e = ws[0].shape[:2]
    spec = lambda a: pl.BlockSpec((1, 1) + a.shape[2:], lambda l, j: (l, j, 0, 0))
    return pl.pallas_call(
        _cast_kernel,
        grid=(depth, e),
        in_specs=[spec(a) for a in ws],
        out_specs=[spec(a) for a in ws],
        out_shape=[jax.ShapeDtypeStruct(a.shape, BF16) for a in ws],
        compiler_params=_params(("parallel", "parallel")),
        name="experts_to_bf16",
    )(*ws)


def _row_tile(n, want):
    while n % want:
        want //= 2
    return want


def _layer(x, bsz, t, lw, *, pos0, topk, cache, s0):
    n = bsz * t
    lc = min(t, SG_LEN)
    k, v, ki, kb, kib, qt, qit, vt, wit = _attn_proj(x, lw["w_att_n"], lw["w_att_t"], bsz, t,
                                                     _row_tile(t, 1024))
    kb3 = kb.reshape(bsz, t, KV_WIDTH)
    kib3 = kib.reshape(bsz, t, IDX_DIM)
    if cache is not None:
        ck, cvt, cki = cache
        kb3 = jnp.concatenate([ck, kb3], axis=1)
        kib3 = jnp.concatenate([cki, kib3], axis=1)
        vt = jnp.concatenate([cvt, vt], axis=2)
    ltot = kb3.shape[1]
    lp = -(-ltot // DSA_KEY_BLOCK) * DSA_KEY_BLOCK
    tq = DSA_QUERY_TILE if t % DSA_QUERY_TILE == 0 else LANES
    tp = -(-t // tq) * tq
    padq = lambda a: jnp.pad(a, ((0, 0), (0, 0), (0, tp - t)))
    a = _dsa(padq(qt), padq(qit), padq(wit), jnp.pad(kb3, ((0, 0), (0, lp - ltot), (0, 0))),
             jnp.pad(vt, ((0, 0), (0, 0), (0, lp - ltot))),
             jnp.pad(kib3, ((0, 0), (0, lp - ltot), (0, 0))), tq=tq, pos0=pos0, topk=topk)[:, :t]
    b, v_gm = _gmlp(x, lw["w_u"], lw["w_v"], lw["ln_sg_g"], lw["ln_sg_b"],
                    lw["w_sg"][:, :lc, :lc], lw["b_sg"][:, :lc].T, tm=_row_tile(n, 512), lc=lc)
    c, s_new = _hgrn(x.reshape(bsz, t, D_MODEL), lw["w_hg"], lw["log_lb"], lw["log1m_lb"],
                     lw["om_lb"], lw["hg_norm_g"], s0, tc=min(t, 256))
    x1 = _merge(x, a.reshape(n, ATT_WIDTH), b, c.reshape(n, HG_WIDTH), lw["w_gates"],
                lw["w_branch_a"], lw["w_branch_b"], lw["w_branch_c"], lw["w_out"],
                lw["ln1_g"], lw["ln1_b"], tm=_row_tile(n, 1024))
    x2 = _moe(x1, lw["w_router_t"], lw["b_router"], lw["w_exp_gate"], lw["w_exp_up"],
              lw["w_exp_down"], lw["ln2_g"], lw["ln2_b"], tm=_row_tile(n, 1024))
    return x2, k, v, ki, s_new, v_gm


def kernel(x_prompt, x_sample, cache_k, cache_v, cache_kidx, state_hgrn, w_in, w_sg, b_sg, ln_sg_g, ln_sg_b, hg_lb_logits, hg_norm_g, w_branch_a, w_branch_b, w_branch_c, w_out, ln1_g, ln1_b, w_router, b_router, w_exp_gate, w_exp_up, w_exp_down, ln2_g, ln2_b):
    bp, sp, _ = x_prompt.shape
    bs, ss, _ = x_sample.shape
    past = cache_k.shape[2]
    topk_p = min(TOPK_MAX, sp // 4)
    topk_s = min(TOPK_MAX, (past + ss) // 4)

    lb_all = jnp.cumsum(jax.nn.softmax(hg_lb_logits.astype(F32), axis=0), axis=0)
    lb_all = lb_all - lb_all[0:1]
    vec = lambda a: a.reshape(1, -1)

    xp = x_prompt.reshape(bp * sp, D_MODEL)
    xs = x_sample.reshape(bs * ss, D_MODEL)
    s0_p = jnp.zeros((bp, HG_HEADS, HG_DK, HG_DV), F32)
    outs_p, outs_s = [], []
    w_att_n, w_att_t, w_u, w_v, w_hg, w_gates = _split_w_in(w_in)
    wg_b, wu_b, wd_b = _experts_to_bf16(w_exp_gate, w_exp_up, w_exp_down)
    for l in range(DEPTH):
        lw = dict(
            w_att_n=w_att_n[l], w_att_t=w_att_t[l], w_u=w_u[l], w_v=w_v[l], w_hg=w_hg[l],
            w_gates=w_gates[l],
            ln_sg_g=vec(ln_sg_g[l]), ln_sg_b=vec(ln_sg_b[l]), w_sg=w_sg[l], b_sg=b_sg[l],
            log_lb=vec(jnp.log(lb_all[l])), log1m_lb=vec(jnp.log1p(-lb_all[l])),
            om_lb=vec(1.0 - lb_all[l]), hg_norm_g=vec(hg_norm_g[l].astype(F32)),
            w_branch_a=w_branch_a[l].astype(BF16), w_branch_b=w_branch_b[l].astype(BF16),
            w_branch_c=w_branch_c[l].astype(BF16), w_out=w_out[l].astype(BF16),
            ln1_g=vec(ln1_g[l]), ln1_b=vec(ln1_b[l]),
            w_router_t=w_router.T.astype(BF16), b_router=b_router.astype(F32).reshape(-1, 1),
            w_exp_gate=wg_b[l], w_exp_up=wu_b[l], w_exp_down=wd_b[l],
            ln2_g=vec(ln2_g[l]), ln2_b=vec(ln2_b[l]))
        xp, k, v, ki, s_new, _ = _layer(xp, bp, sp, lw, pos0=0, topk=topk_p, cache=None, s0=s0_p)
        outs_p.append((k.reshape(bp, sp, KV_HEADS, HEAD_DIM), v.reshape(bp, sp, KV_HEADS, HEAD_DIM),
                       ki.reshape(bp, sp, IDX_DIM), s_new))
        cache = (cache_k[l].reshape(bs, past, KV_WIDTH).astype(BF16),
                 jnp.swapaxes(cache_v[l].reshape(bs, past, KV_WIDTH), 1, 2).astype(BF16),
                 cache_kidx[l].astype(BF16))
        xs, k, v, ki, s_new, v_gm = _layer(xs, bs, ss, lw, pos0=past, topk=topk_s, cache=cache,
                                           s0=state_hgrn[l].astype(F32))
        outs_s.append((k.reshape(bs, ss, KV_HEADS, HEAD_DIM), v.reshape(bs, ss, KV_HEADS, HEAD_DIM),
                       ki.reshape(bs, ss, IDX_DIM), s_new, v_gm.reshape(bs, ss, SG_WIDTH)))

    stack = lambda rows, i: jnp.stack([r[i] for r in rows])
    return (xp.reshape(bp, sp, D_MODEL), xs.reshape(bs, ss, D_MODEL),
            stack(outs_p, 0), stack(outs_p, 1), stack(outs_p, 2), stack(outs_p, 3),
            stack(outs_s, 0), stack(outs_s, 1), stack(outs_s, 2), stack(outs_s, 3),
            stack(outs_s, 4))
```

```python
import functools

import jax
import jax.numpy as jnp
from jax import lax
from jax.experimental import pallas as pl
from jax.experimental.pallas import tpu as pltpu

F32 = jnp.float32
BF16 = jnp.bfloat16
I32 = jnp.int32

D_MODEL = 1024
DEPTH = 4
CHUNK = 64
ATT_HEADS = 8
KV_HEADS = 2
HEAD_DIM = 64
ATT_WIDTH = ATT_HEADS * HEAD_DIM
KV_WIDTH = KV_HEADS * HEAD_DIM
IDX_HEADS = 4
IDX_DIM = 64
TOPK_MAX = 256
SG_LEN = 128
SG_GROUPS = 4
SG_WIDTH = 512
SG_GW = SG_WIDTH // SG_GROUPS
HG_HEADS = 4
HG_DK = 128
HG_DV = 128
HG_WIDTH = HG_HEADS * HG_DV
HG_BLOCK = 16
HG_FACTOR_RANGE = 80.0
N_BRANCH = 3
N_EXPERTS = 16
N_GROUPS = 4
EXP_PER_GROUP = 4
D_FF = 256
ALPHA = (2 * DEPTH) ** 0.25
LN_EPS = 1e-5

SPLIT_SIZES = (ATT_WIDTH, KV_WIDTH, KV_WIDTH, IDX_HEADS * IDX_DIM, IDX_DIM, IDX_HEADS,
               SG_WIDTH, SG_WIDTH, HG_HEADS * HG_DK, HG_HEADS * HG_DK, HG_WIDTH, HG_WIDTH,
               N_BRANCH * D_MODEL)

LANES = 128
INT_MIN = -2 ** 31
NEG_BIG = -1e30
LOG2_E = 1.4426950408889634
DSA_KEY_BLOCK = 512
DSA_QUERY_TILE = 512
VMEM_LIMIT = 56 * 1024 * 1024

_NT = (((1,), (1,)), ((), ()))
_TN = (((0,), (0,)), ((), ()))


def _params(sem):
    return pltpu.CompilerParams(dimension_semantics=sem, vmem_limit_bytes=VMEM_LIMIT)


def _layer_norm(x, g, b):
    mu = jnp.mean(x, axis=-1, keepdims=True)
    d = x - mu
    var = jnp.mean(d * d, axis=-1, keepdims=True)
    return d * lax.rsqrt(var + LN_EPS) * g + b


def _gelu_tanh(x):
    return 0.5 * x * (1.0 + jnp.tanh(0.7978845608028654 * (x + 0.044715 * (x * x * x))))


def _sigmoid(x):
    return 0.5 * jnp.tanh(0.5 * x) + 0.5


def _fold_rows(x, op):
    parts = [x[i:i + 8] for i in range(0, x.shape[0], 8)]
    while len(parts) > 1:
        parts = [op(parts[i], parts[i + 1]) for i in range(0, len(parts), 2)]
    return parts[0]


ATT_T_ROWS = ATT_WIDTH + IDX_HEADS * IDX_DIM + KV_WIDTH + 8
ATT_T_PAD = -(-ATT_T_ROWS // LANES) * LANES
ATT_N_COLS = 3 * LANES


def _attn_proj_kernel(x_ref, wn_ref, wt_ref, k_ref, v_ref, ki_ref, kb_ref, kib_ref,
                      qt_ref, qit_ref, vt_ref, wit_ref):
    xb = x_ref[...].astype(BF16)
    z = jnp.dot(xb, wn_ref[...], preferred_element_type=F32)
    k = z[:, 0:KV_WIDTH]
    ki = z[:, 2 * KV_WIDTH:2 * KV_WIDTH + IDX_DIM]
    k_ref[...] = k
    v_ref[...] = z[:, KV_WIDTH:2 * KV_WIDTH]
    ki_ref[...] = ki
    kb_ref[...] = k.astype(BF16)
    kib_ref[...] = ki.astype(BF16)
    zt = lax.dot_general(wt_ref[...], xb, _NT, preferred_element_type=F32)
    o = 0
    qt_ref[0] = (zt[o:o + ATT_WIDTH] * (HEAD_DIM ** -0.5 * LOG2_E)).astype(BF16)
    o += ATT_WIDTH
    qit_ref[0] = (zt[o:o + IDX_HEADS * IDX_DIM] * IDX_DIM ** -0.5).astype(BF16)
    o += IDX_HEADS * IDX_DIM
    vt_ref[0] = zt[o:o + KV_WIDTH].astype(BF16)
    o += KV_WIDTH
    wit_ref[0] = zt[o:o + 8] * IDX_HEADS ** -0.5


def _attn_proj(x, wn, wt, bsz, t, tm):
    n = bsz * t
    per = t // tm
    row = lambda i: (i, 0)
    tmap = lambda i: (i // per, 0, i % per)
    nat = ((KV_WIDTH, F32), (KV_WIDTH, F32), (IDX_DIM, F32), (KV_WIDTH, BF16), (IDX_DIM, BF16))
    tr = ((ATT_WIDTH, BF16), (IDX_HEADS * IDX_DIM, BF16), (KV_WIDTH, BF16), (8, F32))
    return pl.pallas_call(
        _attn_proj_kernel,
        grid=(n // tm,),
        in_specs=[pl.BlockSpec((tm, D_MODEL), row),
                  pl.BlockSpec((D_MODEL, ATT_N_COLS), lambda i: (0, 0)),
                  pl.BlockSpec((ATT_T_ROWS, D_MODEL), lambda i: (0, 0))],
        out_specs=[pl.BlockSpec((tm, c), row) for c, _ in nat]
                  + [pl.BlockSpec((1, c, tm), tmap) for c, _ in tr],
        out_shape=[jax.ShapeDtypeStruct((n, c), d) for c, d in nat]
                  + [jax.ShapeDtypeStruct((bsz, c, t), d) for c, d in tr],
        compiler_params=_params(("parallel",)),
        name="attn_proj",
    )(x, wn, wt)


def _dsa_kernel(qt_ref, qit_ref, wit_ref, kb_ref, vt_ref, kib_ref, o_ref,
                key_scr, sort_scr, bias_scr, s_scr, acc_scr, m_scr, a_scr,
                *, tq, kblk, pos0, topk):
    j = pl.program_id(1)
    qpos0 = pos0 + j * tq
    qcol = lax.broadcasted_iota(I32, (1, tq), 1)
    chunk_end = ((qpos0 + qcol) // CHUNK + 1) * CHUNK
    lvis = ((qpos0 + tq - 1) // CHUNK + 1) * CHUNK
    nblk = (lvis + kblk - 1) // kblk
    krow = lax.broadcasted_iota(I32, (kblk, tq), 0)

    wi = wit_ref[0]
    qit = qit_ref[0]

    def score_block(i, carry):
        off = pl.multiple_of(i * kblk, kblk)
        kib = kib_ref[0, pl.ds(off, kblk), :]
        sc = jnp.zeros((kblk, tq), F32)
        for h in range(IDX_HEADS):
            raw = jnp.dot(kib, qit[h * IDX_DIM:(h + 1) * IDX_DIM, :], preferred_element_type=F32)
            sc = sc + jnp.maximum(raw, 0.0) * wi[h:h + 1, :]
        bits = pltpu.bitcast(sc, I32)
        key = bits ^ ((bits >> 31) & 0x7FFFFFFF)
        key = jnp.where(off + krow < chunk_end, key, INT_MIN)
        key_scr[i] = key
        for g in range(kblk // 32):
            a, b, c, d = (key[g * 32 + 8 * r:g * 32 + 8 * r + 8] for r in range(4))
            a, b = jnp.maximum(a, b), jnp.minimum(a, b)
            c, d = jnp.maximum(c, d), jnp.minimum(c, d)
            a, c = jnp.maximum(a, c), jnp.minimum(a, c)
            b, d = jnp.maximum(b, d), jnp.minimum(b, d)
            b, c = jnp.maximum(b, c), jnp.minimum(b, c)
            sort_scr[i, g * 32:(g + 1) * 32] = jnp.concatenate([a, b, c, d], axis=0)
        return carry

    lax.fori_loop(0, nblk, score_block, 0)

    def count_keys(above):
        def body(b, accs):
            ks = sort_scr[b]
            accs = list(accs)
            for g in range(kblk // 32):
                s0, s1, s2, s3 = (ks[g * 32 + 8 * r:g * 32 + 8 * r + 8] for r in range(4))
                n = jnp.where(above(s3), 4.0, jnp.where(above(s2), 3.0, jnp.where(
                    above(s1), 2.0, jnp.where(above(s0), 1.0, 0.0))))
                accs[g % 4] = accs[g % 4] + n
            return tuple(accs)

        accs = lax.fori_loop(0, nblk, body, tuple(jnp.zeros((8, tq), F32) for _ in range(4)))
        return jnp.sum((accs[0] + accs[1]) + (accs[2] + accs[3]), axis=0, keepdims=True)

    def key_bit(i, thr):
        cand = thr + jnp.left_shift(jnp.int32(1), 31 - i)
        return jnp.where(count_keys(lambda s: s >= cand) >= topk, cand, thr)

    thr = lax.fori_loop(0, 32, key_bit, jnp.full((1, tq), INT_MIN, I32))

    need = topk - count_keys(lambda s: s > thr)
    tri = jnp.where(lax.broadcasted_iota(I32, (kblk, kblk), 0) >= lax.broadcasted_iota(I32, (kblk, kblk), 1),
                    1.0, 0.0).astype(BF16)

    def select_block(i, taken):
        off = pl.multiple_of(i * kblk, kblk)
        kb = key_scr[i]
        tied = jnp.where(kb == thr, 1.0, 0.0)
        rank = taken + jnp.dot(tri, tied.astype(BF16), preferred_element_type=F32)
        tie = jnp.where(rank <= need, 0.0, NEG_BIG)
        bias = jnp.where(kb > thr, 0.0, jnp.where(kb == thr, tie, NEG_BIG))
        bias_scr[i] = jnp.where(off + krow < chunk_end, bias, NEG_BIG)
        return rank[kblk - 1:kblk, :]

    lax.fori_loop(0, nblk, select_block, jnp.zeros((1, tq), F32))

    m_scr[...] = jnp.full(m_scr.shape, NEG_BIG, F32)
    acc_scr[...] = jnp.zeros(acc_scr.shape, F32)
    rep = ATT_HEADS // KV_HEADS
    qt = qt_ref[0]
    qg = [jnp.concatenate([qt[(g * rep + r) * HEAD_DIM:(g * rep + r + 1) * HEAD_DIM, :]
                           for r in range(rep)], axis=1) for g in range(KV_HEADS)]
    ones = jnp.ones((16, kblk), BF16)

    def score_phase(i, buf):
        off = pl.multiple_of(i * kblk, kblk)
        bias = jnp.concatenate([bias_scr[i]] * rep, axis=1)
        kk = kb_ref[0, pl.ds(off, kblk), :]
        for g in range(KV_HEADS):
            s = jnp.dot(kk[:, g * HEAD_DIM:(g + 1) * HEAD_DIM], qg[g],
                        preferred_element_type=F32) + bias
            s_scr[buf, g] = s
            m_blk = jnp.max(_fold_rows(s, jnp.maximum), axis=0, keepdims=True)
            m_old = m_scr[g]
            m_new = jnp.maximum(m_old, m_blk)
            a_scr[g] = jnp.exp2(m_old - m_new)
            m_scr[g] = m_new

    def value_phase(i, buf):
        off = pl.multiple_of(i * kblk, kblk)
        vv = vt_ref[0, :, pl.ds(off, kblk)]
        for g in range(KV_HEADS):
            p = jnp.exp2(s_scr[buf, g] - m_scr[g]).astype(BF16)
            vg = jnp.concatenate([vv[g * HEAD_DIM:(g + 1) * HEAD_DIM, :], ones], axis=0)
            acc_scr[g] = a_scr[g] * acc_scr[g] + jnp.dot(vg, p, preferred_element_type=F32)

    score_phase(0, 0)

    def attn_pair(k, carry):
        i = 2 * k
        value_phase(i, 0)
        score_phase(i + 1, 1)
        value_phase(i + 1, 1)
        score_phase(jnp.minimum(i + 2, nblk - 1), 0)
        return carry

    lax.fori_loop(0, nblk // 2, attn_pair, 0)

    @pl.when(nblk % 2 == 1)
    def _():
        value_phase(nblk - 1, 0)

    heads = []
    for g in range(KV_HEADS):
        acc = acc_scr[g]
        og = acc[:HEAD_DIM] / acc[HEAD_DIM:HEAD_DIM + 1]
        heads += [og[:, r * tq:(r + 1) * tq] for r in range(rep)]
    o_ref[0] = jnp.concatenate(heads, axis=0).T.astype(BF16)


def _dsa(qt, qit, wit, kb, vt, kib, *, tq, pos0, topk):
    b, _, t = qt.shape
    lp = kb.shape[1]
    kblk = DSA_KEY_BLOCK
    assert lp % kblk == 0 and t % tq == 0 and tq % LANES == 0
    kern = functools.partial(_dsa_kernel, tq=tq, kblk=kblk, pos0=pos0, topk=topk)
    rep = ATT_HEADS // KV_HEADS
    qmap = lambda bi, j: (bi, 0, j)
    kmap = lambda bi, j: (bi, 0, 0)
    return pl.pallas_call(
        kern,
        grid=(b, t // tq),
        in_specs=[pl.BlockSpec((1, ATT_WIDTH, tq), qmap),
                  pl.BlockSpec((1, IDX_HEADS * IDX_DIM, tq), qmap),
                  pl.BlockSpec((1, 8, tq), qmap),
                  pl.BlockSpec((1, lp, KV_WIDTH), kmap),
                  pl.BlockSpec((1, KV_WIDTH, lp), kmap),
                  pl.BlockSpec((1, lp, IDX_DIM), kmap)],
        out_specs=pl.BlockSpec((1, tq, ATT_WIDTH), lambda bi, j: (bi, j, 0)),
        out_shape=jax.ShapeDtypeStruct((b, t, ATT_WIDTH), BF16),
        scratch_shapes=[pltpu.VMEM((lp // kblk, kblk, tq), I32),
                        pltpu.VMEM((lp // kblk, kblk, tq), I32),
                        pltpu.VMEM((lp // kblk, kblk, tq), F32),
                        pltpu.VMEM((2, KV_HEADS, kblk, rep * tq), F32),
                        pltpu.VMEM((KV_HEADS, HEAD_DIM + 16, rep * tq), F32),
                        pltpu.VMEM((KV_HEADS, 1, rep * tq), F32),
                        pltpu.VMEM((KV_HEADS, 1, rep * tq), F32)],
        compiler_params=_params(("parallel", "arbitrary")),
        name="dsa",
    )(qt, qit, wit, kb, vt, kib)


def _gmlp_kernel(x_ref, wu_ref, wv_ref, g_ref, b_ref, ws_ref, bst_ref, o_ref, v_ref, *, tm, lc):
    xb = x_ref[...].astype(BF16)
    u = _gelu_tanh(jnp.dot(xb, wu_ref[...], preferred_element_type=F32))
    v = _layer_norm(_gelu_tanh(jnp.dot(xb, wv_ref[...], preferred_element_type=F32)),
                    g_ref[...], b_ref[...])
    v_ref[...] = v
    vb = v.astype(BF16)
    r = lax.broadcasted_iota(I32, (lc, lc), 0)
    c = lax.broadcasted_iota(I32, (lc, lc), 1)
    for g in range(SG_GROUPS):
        w = jnp.where(r >= c, ws_ref[g], 0.0).astype(BF16)
        bias = bst_ref[:, g:g + 1]
        cs = slice(g * SG_GW, (g + 1) * SG_GW)
        for n in range(tm // lc):
            rs = slice(n * lc, (n + 1) * lc)
            mixed = jnp.dot(w, vb[rs, cs], preferred_element_type=F32) + bias
            o_ref[rs, cs] = (u[rs, cs] * mixed).astype(BF16)


def _gmlp(x, wu, wv, ln_g, ln_b, ws, bst, *, tm, lc):
    n = x.shape[0]
    row = lambda i: (i, 0)
    full2 = lambda i: (0, 0)
    return pl.pallas_call(
        functools.partial(_gmlp_kernel, tm=tm, lc=lc),
        grid=(n // tm,),
        in_specs=[pl.BlockSpec((tm, D_MODEL), row),
                  pl.BlockSpec((D_MODEL, SG_WIDTH), full2),
                  pl.BlockSpec((D_MODEL, SG_WIDTH), full2),
                  pl.BlockSpec((1, SG_WIDTH), full2),
                  pl.BlockSpec((1, SG_WIDTH), full2),
                  pl.BlockSpec((SG_GROUPS, lc, lc), lambda i: (0, 0, 0)),
                  pl.BlockSpec((lc, SG_GROUPS), full2)],
        out_specs=[pl.BlockSpec((tm, SG_WIDTH), row), pl.BlockSpec((tm, SG_WIDTH), row)],
        out_shape=[jax.ShapeDtypeStruct((n, SG_WIDTH), BF16),
                   jax.ShapeDtypeStruct((n, SG_WIDTH), F32)],
        compiler_params=_params(("parallel",)),
        name="gmlp",
    )(x, wu, wv, ln_g, ln_b, ws, bst)


def _hgrn_kernel(x_ref, w_ref, loglb_ref, log1mlb_ref, omlb_ref, ng_ref, s0_ref, c_ref, sout_ref,
                 q_scr, k_scr, v_scr, lf_scr, o_scr, st_scr, *, tc):
    t = pl.program_id(1)

    @pl.when(t == 0)
    def _():
        for h in range(HG_HEADS):
            st_scr[h] = s0_ref[0, h].T

    xb = x_ref[0].astype(BF16)
    z = jnp.dot(xb, w_ref[...], preferred_element_type=F32)
    hq = z[:, 0:HG_WIDTH]
    hf = z[:, HG_WIDTH:2 * HG_WIDTH]
    hg = z[:, 3 * HG_WIDTH:4 * HG_WIDTH]
    q_scr[...] = hq * _sigmoid(hq) * HG_DK ** -0.5
    u = jnp.exp(-jnp.abs(hf))
    w = 1.0 + u
    log_sig = jnp.minimum(hf, 0.0) - jnp.log(w)
    y = log1mlb_ref[...] + log_sig
    a = loglb_ref[...]
    lf_scr[...] = jnp.maximum(a, y) + jnp.log(1.0 + jnp.exp(-jnp.abs(a - y)))
    k_scr[...] = omlb_ref[...] * (jnp.where(hf >= 0.0, u, 1.0) / w)
    v_scr[...] = z[:, 2 * HG_WIDTH:3 * HG_WIDTH]

    ch = min(tc, CHUNK)
    ri = lax.broadcasted_iota(I32, (tc, tc), 0)
    ci = lax.broadcasted_iota(I32, (tc, tc), 1)
    same = (ri // ch) == (ci // ch)
    causal = same & (ri >= ci)
    lf = lf_scr[...]
    lf_hi = lf.astype(BF16)
    lf_r = lf - lf_hi.astype(F32)
    lf_mid = lf_r.astype(BF16)
    lf_lo = (lf_r - lf_mid.astype(F32)).astype(BF16)
    cmask = jnp.where(causal, 1.0, 0.0).astype(BF16)
    bcum_all = (jnp.dot(cmask, lf_hi, preferred_element_type=F32)
                + jnp.dot(cmask, lf_mid, preferred_element_type=F32)
                + jnp.dot(cmask, lf_lo, preferred_element_type=F32))
    factored = jnp.min(bcum_all) > -HG_FACTOR_RANGE

    @pl.when(factored)
    def _():
        btot = jnp.concatenate(
            [jnp.broadcast_to(bcum_all[(n + 1) * ch - 1:(n + 1) * ch, :], (ch, HG_WIDTH))
             for n in range(tc // ch)], axis=0)
        qa = q_scr[...]
        ka = k_scr[...]
        qd = (qa * jnp.exp(bcum_all)).astype(BF16)
        ki = (ka * jnp.exp(-bcum_all)).astype(BF16)
        kd = (ka * jnp.exp(btot - bcum_all)).astype(BF16)
        va = v_scr[...].astype(BF16)
        for h in range(HG_HEADS):
            cs = slice(h * HG_DK, (h + 1) * HG_DK)
            att = lax.dot_general(qd[:, cs], ki[:, cs], _NT, preferred_element_type=F32)
            att = jnp.where(causal, att, 0.0).astype(BF16)
            o = jnp.dot(att, va[:, cs], preferred_element_type=F32)
            st = st_scr[h]
            for n in range(tc // ch):
                rs = slice(n * ch, (n + 1) * ch)
                o_scr[rs, cs] = o[rs] + lax.dot_general(qd[rs, cs], st.astype(BF16), _NT,
                                                        preferred_element_type=F32)
                upd = lax.dot_general(va[rs, cs], kd[rs, cs], _TN, preferred_element_type=F32)
                st = jnp.exp(btot[n * ch:n * ch + 1, cs]) * st + upd
            st_scr[h] = st

    nb = HG_BLOCK
    tri = jnp.where(lax.broadcasted_iota(I32, (nb, nb), 0) >= lax.broadcasted_iota(I32, (nb, nb), 1),
                    1.0, 0.0).astype(F32)
    trow = lax.broadcasted_iota(I32, (nb, 1), 0)

    def block(i, carry):
        r0 = pl.multiple_of(i * nb, nb)
        rows = pl.ds(r0, nb)
        for h in range(HG_HEADS):
            cs = slice(h * HG_DK, (h + 1) * HG_DK)
            bcum = jnp.dot(tri, lf_scr[rows, cs], preferred_element_type=F32,
                           precision=lax.Precision.HIGHEST)
            qb = q_scr[rows, cs]
            kb = k_scr[rows, cs]
            vb = v_scr[rows, cs]
            o = jnp.zeros((nb, HG_DV), F32)
            for s in range(nb):
                e = jnp.exp(jnp.where(trow >= s, bcum - bcum[s:s + 1, :], -jnp.inf))
                a_ts = jnp.sum(qb * kb[s:s + 1, :] * e, axis=1, keepdims=True)
                o = o + a_ts * vb[s:s + 1, :]
            st = st_scr[h]
            qd = (qb * jnp.exp(bcum)).astype(BF16)
            o = o + lax.dot_general(qd, st.astype(BF16), _NT, preferred_element_type=F32)
            blast = bcum[nb - 1:nb, :]
            kd = (kb * jnp.exp(blast - bcum)).astype(BF16)
            upd = lax.dot_general(vb.astype(BF16), kd, _TN, preferred_element_type=F32)
            st_scr[h] = jnp.exp(blast) * st + upd
            o_scr[rows, cs] = o
        return carry

    @pl.when(jnp.logical_not(factored))
    def _():
        lax.fori_loop(0, tc // nb, block, 0)

    for h in range(HG_HEADS):
        cs = slice(h * HG_DV, (h + 1) * HG_DV)
        o = o_scr[:, cs]
        o = o * lax.rsqrt(jnp.mean(o * o, axis=-1, keepdims=True) + LN_EPS)
        g = hg[:, cs]
        c_ref[0, :, cs] = (o * ng_ref[:, cs] * (g * _sigmoid(g))).astype(BF16)

    @pl.when(t == pl.num_programs(1) - 1)
    def _():
        for h in range(HG_HEADS):
            sout_ref[0, h] = st_scr[h].T


def _hgrn(x, w, loglb, log1mlb, omlb, ng, s0, *, tc):
    b, t, _ = x.shape
    xmap = lambda bi, j: (bi, j, 0)
    vec = pl.BlockSpec((1, HG_WIDTH), lambda bi, j: (0, 0))
    smap = lambda bi, j: (bi, 0, 0, 0)
    return pl.pallas_call(
        functools.partial(_hgrn_kernel, tc=tc),
        grid=(b, t // tc),
        in_specs=[pl.BlockSpec((1, tc, D_MODEL), xmap),
                  pl.BlockSpec((D_MODEL, 4 * HG_WIDTH), lambda bi, j: (0, 0)),
                  vec, vec, vec, vec,
                  pl.BlockSpec((1, HG_HEADS, HG_DK, HG_DV), smap)],
        out_specs=[pl.BlockSpec((1, tc, HG_WIDTH), xmap),
                   pl.BlockSpec((1, HG_HEADS, HG_DK, HG_DV), smap)],
        out_shape=[jax.ShapeDtypeStruct((b, t, HG_WIDTH), BF16),
                   jax.ShapeDtypeStruct((b, HG_HEADS, HG_DK, HG_DV), F32)],
        scratch_shapes=[pltpu.VMEM((tc, HG_WIDTH), F32)] * 5
                       + [pltpu.VMEM((HG_HEADS, HG_DV, HG_DK), F32)],
        compiler_params=_params(("parallel", "arbitrary")),
        name="hgrn",
    )(x, w, loglb, log1mlb, omlb, ng, s0)


def _merge_kernel(x_ref, a_ref, b_ref, c_ref, wg_ref, wa_ref, wb_ref, wc_ref, wo_ref,
                  g_ref, bt_ref, y_ref):
    x = x_ref[...]
    xb = x.astype(BF16)
    m = None
    for i, (br, w) in enumerate(((a_ref, wa_ref), (b_ref, wb_ref), (c_ref, wc_ref))):
        gate = _sigmoid(jnp.dot(xb, wg_ref[:, i * D_MODEL:(i + 1) * D_MODEL],
                                preferred_element_type=F32))
        term = gate * jnp.dot(br[...], w[...], preferred_element_type=F32)
        m = term if m is None else m + term
    mo = jnp.dot(m.astype(BF16), wo_ref[...], preferred_element_type=F32)
    y_ref[...] = _layer_norm(ALPHA * x + mo, g_ref[...], bt_ref[...])


def _merge(x, a, b, c, wg, wa, wb, wc, wo, ln_g, ln_b, *, tm):
    n = x.shape[0]
    row = lambda i: (i, 0)
    full = lambda i: (0, 0)
    br = pl.BlockSpec((tm, ATT_WIDTH), row)
    bw = pl.BlockSpec((ATT_WIDTH, D_MODEL), full)
    vec = pl.BlockSpec((1, D_MODEL), full)
    return pl.pallas_call(
        _merge_kernel,
        grid=(n // tm,),
        in_specs=[pl.BlockSpec((tm, D_MODEL), row), br, br, br,
                  pl.BlockSpec((D_MODEL, N_BRANCH * D_MODEL), full), bw, bw, bw,
                  pl.BlockSpec((D_MODEL, D_MODEL), full), vec, vec],
        out_specs=pl.BlockSpec((tm, D_MODEL), row),
        out_shape=jax.ShapeDtypeStruct((n, D_MODEL), F32),
        compiler_params=_params(("parallel",)),
        name="merge",
    )(x, a, b, c, wg, wa, wb, wc, wo, ln_g, ln_b)


def _route(logits):
    ex = jnp.exp(logits - jnp.max(logits, axis=0, keepdims=True))
    probs = ex / jnp.sum(ex, axis=0, keepdims=True)
    p = [[probs[g * EXP_PER_GROUP + k:g * EXP_PER_GROUP + k + 1, :] for k in range(EXP_PER_GROUP)]
         for g in range(N_GROUPS)]
    score = []
    for g in range(N_GROUPS):
        best = None
        for k1 in range(EXP_PER_GROUP):
            for k2 in range(k1 + 1, EXP_PER_GROUP):
                pair = p[g][k1] + p[g][k2]
                best = pair if best is None else jnp.maximum(best, pair)
        score.append(best)
    gsel = jnp.zeros(score[0].shape, I32)
    top = score[0]
    for g in range(1, N_GROUPS):
        better = score[g] > top
        top = jnp.where(better, score[g], top)
        gsel = jnp.where(better, g, gsel)
    val = []
    for k in range(EXP_PER_GROUP):
        v = p[0][k]
        for g in range(1, N_GROUPS):
            v = jnp.where(gsel == g, p[g][k], v)
        val.append(v)
    v1, i1 = val[0], jnp.zeros(gsel.shape, I32)
    for k in range(1, EXP_PER_GROUP):
        better = val[k] > v1
        v1 = jnp.where(better, val[k], v1)
        i1 = jnp.where(better, k, i1)
    v2, i2 = jnp.full(v1.shape, -1.0, F32), jnp.zeros(gsel.shape, I32)
    for k in range(EXP_PER_GROUP):
        better = (i1 != k) & (val[k] > v2)
        v2 = jnp.where(better, val[k], v2)
        i2 = jnp.where(better, k, i2)
    den = v1 + v2
    w1, w2 = v1 / den, v2 / den
    out = []
    for g in range(N_GROUPS):
        rows = [jnp.where(gsel == g, jnp.where(i1 == k, w1, jnp.where(i2 == k, w2, 0.0)), 0.0)
                for k in range(EXP_PER_GROUP)]
        out.append(jnp.concatenate(rows, axis=0))
    return out


def _moe_kernel(x_ref, wr_ref, br_ref, wg_ref, wu_ref, wd_ref, g_ref, bt_ref, y_ref,
                xb_scr, gate_scr, acc_scr, *, tm):
    grp = pl.program_id(1)

    @pl.when(grp == 0)
    def _():
        xb = x_ref[...].astype(BF16)
        xb_scr[...] = xb
        logits = lax.dot_general(wr_ref[...], xb, _NT, preferred_element_type=F32) + br_ref[...]
        gates = _route(logits)
        pad = jnp.zeros((LANES - EXP_PER_GROUP, tm), F32)
        for g in range(N_GROUPS):
            gate_scr[g] = jnp.concatenate([gates[g], pad], axis=0).T
        acc_scr[...] = jnp.zeros(acc_scr.shape, F32)

    xb = xb_scr[...]
    gate = gate_scr[grp]
    acc = acc_scr[...]
    for k in range(EXP_PER_GROUP):
        h = jnp.dot(xb, wg_ref[k], preferred_element_type=F32)
        up = jnp.dot(xb, wu_ref[k], preferred_element_type=F32)
        act = h * _sigmoid(h) * up * gate[:, k:k + 1]
        acc = acc + jnp.dot(act.astype(BF16), wd_ref[k], preferred_element_type=F32)
    acc_scr[...] = acc

    @pl.when(grp == N_GROUPS - 1)
    def _():
        y_ref[...] = _layer_norm(ALPHA * x_ref[...] + acc, g_ref[...], bt_ref[...])


def _moe(x, wr_t, br, wg, wu, wd, ln_g, ln_b, *, tm):
    n = x.shape[0]
    row = lambda i, g: (i, 0)
    full = lambda i, g: (0, 0)
    vec = pl.BlockSpec((1, D_MODEL), full)
    return pl.pallas_call(
        functools.partial(_moe_kernel, tm=tm),
        grid=(n // tm, N_GROUPS),
        in_specs=[pl.BlockSpec((tm, D_MODEL), row),
                  pl.BlockSpec((N_EXPERTS, D_MODEL), full),
                  pl.BlockSpec((N_EXPERTS, 1), full),
                  pl.BlockSpec((EXP_PER_GROUP, D_MODEL, D_FF), lambda i, g: (g, 0, 0)),
                  pl.BlockSpec((EXP_PER_GROUP, D_MODEL, D_FF), lambda i, g: (g, 0, 0)),
                  pl.BlockSpec((EXP_PER_GROUP, D_FF, D_MODEL), lambda i, g: (g, 0, 0)),
                  vec, vec],
        out_specs=pl.BlockSpec((tm, D_MODEL), row),
        out_shape=jax.ShapeDtypeStruct((n, D_MODEL), F32),
        scratch_shapes=[pltpu.VMEM((tm, D_MODEL), BF16),
                        pltpu.VMEM((N_GROUPS, tm, LANES), F32),
                        pltpu.VMEM((tm, D_MODEL), F32)],
        compiler_params=_params(("parallel", "arbitrary")),
        name="moe",
    )(x, wr_t, br, wg, wu, wd, ln_g, ln_b)


WPREP_ROWS = 128


def _split_w_in_kernel(w_ref, wn_ref, wt_ref, wu_ref, wv_ref, whg_ref, wg_ref):
    w = w_ref[0]
    offs = [0]
    for n in SPLIT_SIZES:
        offs.append(offs[-1] + n)
    seg = lambda i, j=None: w[:, offs[i]:offs[i + 1 if j is None else j]]
    zeros = lambda c: jnp.zeros((w.shape[0], c), w.dtype)
    wn_ref[0] = jnp.concatenate([seg(1), seg(2), seg(4), zeros(LANES - IDX_DIM)], axis=1).astype(BF16)
    wt = jnp.concatenate([seg(0), seg(3), seg(2), seg(5), zeros(ATT_T_PAD - ATT_T_ROWS + 8 - IDX_HEADS)],
                         axis=1)
    for c in range(ATT_T_PAD // LANES):
        rows = min(LANES, ATT_T_ROWS - c * LANES)
        wt_ref[0, c * LANES:c * LANES + rows, :] = wt[:, c * LANES:(c + 1) * LANES].T[:rows].astype(BF16)
    wu_ref[0] = seg(6).astype(BF16)
    wv_ref[0] = seg(7).astype(BF16)
    whg_ref[0] = seg(8, 12).astype(BF16)
    wg_ref[0] = seg(12).astype(BF16)


def _split_w_in(w_in):
    depth, d, ncol = w_in.shape
    tr = WPREP_ROWS
    cols = (ATT_N_COLS, None, SG_WIDTH, SG_WIDTH, 4 * HG_WIDTH, N_BRANCH * D_MODEL)
    shapes = [(depth, ATT_T_ROWS, d) if c is None else (depth, d, c) for c in cols]
    specs = [pl.BlockSpec((1, ATT_T_ROWS, tr), lambda l, i: (l, 0, i)) if c is None
             else pl.BlockSpec((1, tr, c), lambda l, i: (l, i, 0)) for c in cols]
    return pl.pallas_call(
        _split_w_in_kernel,
        grid=(depth, d // tr),
        in_specs=[pl.BlockSpec((1, tr, ncol), lambda l, i: (l, i, 0))],
        out_specs=specs,
        out_shape=[jax.ShapeDtypeStruct(sh, BF16) for sh in shapes],
        compiler_params=_params(("parallel", "parallel")),
        name="split_w_in",
    )(w_in)


def _proj_fused_kernel(x_ref, wn_ref, wt_ref, wu_ref, wv_ref, g_ref, b_ref, ws_ref, bst_ref,
                       *outs, tm, lc):
    _attn_proj_kernel(x_ref, wn_ref, wt_ref, *outs[:9])
    _gmlp_kernel(x_ref, wu_ref, wv_ref, g_ref, b_ref, ws_ref, bst_ref, *outs[9:], tm=tm, lc=lc)


def _proj_fused(x, wn, wt, wu, wv, ln_g, ln_b, ws, bst, bsz, t, tm, lc):
    n = bsz * t
    per = t // tm
    row = lambda i: (i, 0)
    full2 = lambda i: (0, 0)
    tmap = lambda i: (i // per, 0, i % per)
    nat = ((KV_WIDTH, F32), (KV_WIDTH, F32), (IDX_DIM, F32), (KV_WIDTH, BF16), (IDX_DIM, BF16))
    tr = ((ATT_WIDTH, BF16), (IDX_HEADS * IDX_DIM, BF16), (KV_WIDTH, BF16), (8, F32))
    gm = ((SG_WIDTH, BF16), (SG_WIDTH, F32))
    return pl.pallas_call(
        functools.partial(_proj_fused_kernel, tm=tm, lc=lc),
        grid=(n // tm,),
        in_specs=[pl.BlockSpec((tm, D_MODEL), row),
                  pl.BlockSpec((D_MODEL, ATT_N_COLS), full2),
                  pl.BlockSpec((ATT_T_ROWS, D_MODEL), full2),
                  pl.BlockSpec((D_MODEL, SG_WIDTH), full2),
                  pl.BlockSpec((D_MODEL, SG_WIDTH), full2),
                  pl.BlockSpec((1, SG_WIDTH), full2),
                  pl.BlockSpec((1, SG_WIDTH), full2),
                  pl.BlockSpec((SG_GROUPS, lc, lc), lambda i: (0, 0, 0)),
                  pl.BlockSpec((lc, SG_GROUPS), full2)],
        out_specs=[pl.BlockSpec((tm, c), row) for c, _ in nat]
                  + [pl.BlockSpec((1, c, tm), tmap) for c, _ in tr]
                  + [pl.BlockSpec((tm, c), row) for c, _ in gm],
        out_shape=[jax.ShapeDtypeStruct((n, c), d) for c, d in nat]
                  + [jax.ShapeDtypeStruct((bsz, c, t), d) for c, d in tr]
                  + [jax.ShapeDtypeStruct((n, c), d) for c, d in gm],
        compiler_params=_params(("parallel",)),
        name="proj_fused",
    )(x, wn, wt, wu, wv, ln_g, ln_b, ws, bst)


def _cast_kernel(*refs):
    n = len(refs) // 2
    for src, dst in zip(refs[:n], refs[n:]):
        dst[...] = src[...].astype(BF16)


def _experts_to_bf16(*ws):
    depth, e = ws[0].shape[:2]
    spec = lambda a: pl.BlockSpec((1, 1) + a.shape[2:], lambda l, j: (l, j, 0, 0))
    return pl.pallas_call(
        _cast_kernel,
        grid=(depth, e),
        in_specs=[spec(a) for a in ws],
        out_specs=[spec(a) for a in ws],
        out_shape=[jax.ShapeDtypeStruct(a.shape, BF16) for a in ws],
        compiler_params=_params(("parallel", "parallel")),
        name="experts_to_bf16",
    )(*ws)


def _row_tile(n, want):
    while n % want:
        want //= 2
    return want


def _layer(x, bsz, t, lw, *, pos0, topk, cache, s0):
    n = bsz * t
    lc = min(t, SG_LEN)
    k, v, ki, kb, kib, qt, qit, vt, wit, b, v_gm = _proj_fused(
        x, lw["w_att_n"], lw["w_att_t"], lw["w_u"], lw["w_v"], lw["ln_sg_g"], lw["ln_sg_b"],
        lw["w_sg"][:, :lc, :lc], lw["b_sg"][:, :lc].T, bsz, t, _row_tile(t, 512), lc)
    kb3 = kb.reshape(bsz, t, KV_WIDTH)
    kib3 = kib.reshape(bsz, t, IDX_DIM)
    if cache is not None:
        ck, cvt, cki = cache
        kb3 = jnp.concatenate([ck, kb3], axis=1)
        kib3 = jnp.concatenate([cki, kib3], axis=1)
        vt = jnp.concatenate([cvt, vt], axis=2)
    ltot = kb3.shape[1]
    lp = -(-ltot // DSA_KEY_BLOCK) * DSA_KEY_BLOCK
    tq = DSA_QUERY_TILE if t % DSA_QUERY_TILE == 0 else LANES
    tp = -(-t // tq) * tq
    padq = lambda a: jnp.pad(a, ((0, 0), (0, 0), (0, tp - t)))
    a = _dsa(padq(qt), padq(qit), padq(wit), jnp.pad(kb3, ((0, 0), (0, lp - ltot), (0, 0))),
             jnp.pad(vt, ((0, 0), (0, 0), (0, lp - ltot))),
             jnp.pad(kib3, ((0, 0), (0, lp - ltot), (0, 0))), tq=tq, pos0=pos0, topk=topk)[:, :t]
    c, s_new = _hgrn(x.reshape(bsz, t, D_MODEL), lw["w_hg"], lw["log_lb"], lw["log1m_lb"],
                     lw["om_lb"], lw["hg_norm_g"], s0, tc=min(t, 256))
    x1 = _merge(x, a.reshape(n, ATT_WIDTH), b, c.reshape(n, HG_WIDTH), lw["w_gates"],
                lw["w_branch_a"], lw["w_branch_b"], lw["w_branch_c"], lw["w_out"],
                lw["ln1_g"], lw["ln1_b"], tm=_row_tile(n, 1024))
    x2 = _moe(x1, lw["w_router_t"], lw["b_router"], lw["w_exp_gate"], lw["w_exp_up"],
              lw["w_exp_down"], lw["ln2_g"], lw["ln2_b"], tm=_row_tile(n, 1024))
    return x2, k, v, ki, s_new, v_gm


def kernel(x_prompt, x_sample, cache_k, cache_v, cache_kidx, state_hgrn, w_in, w_sg, b_sg, ln_sg_g, ln_sg_b, hg_lb_logits, hg_norm_g, w_branch_a, w_branch_b, w_branch_c, w_out, ln1_g, ln1_b, w_router, b_router, w_exp_gate, w_exp_up, w_exp_down, ln2_g, ln2_b):
    bp, sp, _ = x_prompt.shape
    bs, ss, _ = x_sample.shape
    past = cache_k.shape[2]
    topk_p = min(TOPK_MAX, sp // 4)
    topk_s = min(TOPK_MAX, (past + ss) // 4)

    lb_all = jnp.cumsum(jax.nn.softmax(hg_lb_logits.astype(F32), axis=0), axis=0)
    lb_all = lb_all - lb_all[0:1]
    vec = lambda a: a.reshape(1, -1)

    xp = x_prompt.reshape(bp * sp, D_MODEL)
    xs = x_sample.reshape(bs * ss, D_MODEL)
    s0_p = jnp.zeros((bp, HG_HEADS, HG_DK, HG_DV), F32)
    outs_p, outs_s = [], []
    w_att_n, w_att_t, w_u, w_v, w_hg, w_gates = _split_w_in(w_in)
    wg_b, wu_b, wd_b = _experts_to_bf16(w_exp_gate, w_exp_up, w_exp_down)
    for l in range(DEPTH):
        lw = dict(
            w_att_n=w_att_n[l], w_att_t=w_att_t[l], w_u=w_u[l], w_v=w_v[l], w_hg=w_hg[l],
            w_gates=w_gates[l],
            ln_sg_g=vec(ln_sg_g[l]), ln_sg_b=vec(ln_sg_b[l]), w_sg=w_sg[l], b_sg=b_sg[l],
            log_lb=vec(jnp.log(lb_all[l])), log1m_lb=vec(jnp.log1p(-lb_all[l])),
            om_lb=vec(1.0 - lb_all[l]), hg_norm_g=vec(hg_norm_g[l].astype(F32)),
            w_branch_a=w_branch_a[l].astype(BF16), w_branch_b=w_branch_b[l].astype(BF16),
            w_branch_c=w_branch_c[l].astype(BF16), w_out=w_out[l].astype(BF16),
            ln1_g=vec(ln1_g[l]), ln1_b=vec(ln1_b[l]),
            w_router_t=w_router.T.astype(BF16), b_router=b_router.astype(F32).reshape(-1, 1),
            w_exp_gate=wg_b[l], w_exp_up=wu_b[l], w_exp_down=wd_b[l],
            ln2_g=vec(ln2_g[l]), ln2_b=vec(ln2_b[l]))
        xp, k, v, ki, s_new, _ = _layer(xp, bp, sp, lw, pos0=0, topk=topk_p, cache=None, s0=s0_p)
        outs_p.append((k.reshape(bp, sp, KV_HEADS, HEAD_DIM), v.reshape(bp, sp, KV_HEADS, HEAD_DIM),
                       ki.reshape(bp, sp, IDX_DIM), s_new))
        cache = (cache_k[l].reshape(bs, past, KV_WIDTH).astype(BF16),
                 jnp.swapaxes(cache_v[l].reshape(bs, past, KV_WIDTH), 1, 2).astype(BF16),
                 cache_kidx[l].astype(BF16))
        xs, k, v, ki, s_new, v_gm = _layer(xs, bs, ss, lw, pos0=past, topk=topk_s, cache=cache,
                                           s0=state_hgrn[l].astype(F32))
        outs_s.append((k.reshape(bs, ss, KV_HEADS, HEAD_DIM), v.reshape(bs, ss, KV_HEADS, HEAD_DIM),
                       ki.reshape(bs, ss, IDX_DIM), s_new, v_gm.reshape(bs, ss, SG_WIDTH)))

    stack = lambda rows, i: jnp.stack([r[i] for r in rows])
    return (xp.reshape(bp, sp, D_MODEL), xs.reshape(bs, ss, D_MODEL),
            stack(outs_p, 0), stack(outs_p, 1), stack(outs_p, 2), stack(outs_p, 3),
            stack(outs_s, 0), stack(outs_s, 1), stack(outs_s, 2), stack(outs_s, 3),
            stack(outs_s, 4))
```
